```python
import jax, jax.numpy as jnp
from jax import lax
import numpy as np

D_MODEL = 1024
BATCH = 16
SEQ = 2048
DEPTH = 1

PLE_DIM = 256
MIX_WIDTH = D_MODEL
NORM_EPS = 1e-6
D_FF = 4 * D_MODEL

GLA_WIDTH = MIX_WIDTH // 2
GLA_HEADS = 4
GLA_DV = GLA_WIDTH // GLA_HEADS
GLA_DK = GLA_DV // 2
GLA_KEY_WIDTH = GLA_HEADS * GLA_DK
GLA_GATE_RANK = 16
GLA_GATE_NORMALIZER = 16.0
GLA_CONV = 4
GLA_CHUNK = 64
GLA_COLS = 2 * GLA_KEY_WIDTH + 2 * GLA_WIDTH + GLA_GATE_RANK
GLA_SPLITS = (GLA_KEY_WIDTH, 2 * GLA_KEY_WIDTH, 2 * GLA_KEY_WIDTH + GLA_WIDTH, 2 * GLA_KEY_WIDTH + 2 * GLA_WIDTH)

RWKV_WIDTH = MIX_WIDTH - GLA_WIDTH
RWKV_HEAD = 64
RWKV_HEADS = RWKV_WIDTH // RWKV_HEAD
RWKV_DECAY_RANK = 32
RWKV_AAA_RANK = 32
RWKV_GATE_RANK = 96
RWKV_GN_EPS = 64e-5
RWKV_COLS = 3 * RWKV_WIDTH + RWKV_DECAY_RANK + RWKV_AAA_RANK + RWKV_GATE_RANK
RWKV_SPLITS = (RWKV_WIDTH, 2 * RWKV_WIDTH, 3 * RWKV_WIDTH, 3 * RWKV_WIDTH + RWKV_DECAY_RANK, 3 * RWKV_WIDTH + RWKV_DECAY_RANK + RWKV_AAA_RANK)

D_IN = GLA_COLS + RWKV_COLS

kernel_name = "hymba_gla_rwkv7_hybrid_block"


def _rmsnorm(x, g):
    xf = x.astype(jnp.float32)
    y = xf * lax.rsqrt(jnp.mean(xf * xf, axis=-1, keepdims=True) + NORM_EPS)
    return (y * g.astype(jnp.float32)).astype(x.dtype)


def _causal_dwconv(x, w):
    k = w.shape[0]
    return lax.conv_general_dilated(
        x, w[:, None, :].astype(x.dtype), window_strides=(1,), padding=[(k - 1, 0)],
        dimension_numbers=('NWC', 'WIO', 'NWC'), feature_group_count=x.shape[-1])


def _token_shift(y, mu):
    y_prev = jnp.pad(y, ((0, 0), (1, 0), (0, 0)))[:, :-1]
    return y + mu * (y_prev - y)


def _gla_mixer(z, conv_w, gk_w, gk_b, norm_g):
    B, S, _ = z.shape
    f32 = jnp.float32
    q, k, v, g, gk_low = jnp.split(z, GLA_SPLITS, axis=-1)
    qkv = jax.nn.silu(_causal_dwconv(jnp.concatenate([q, k, v], axis=-1), conv_w))
    q, k, v = jnp.split(qkv, (GLA_KEY_WIDTH, 2 * GLA_KEY_WIDTH), axis=-1)
    gk = jax.nn.log_sigmoid((gk_low @ gk_w + gk_b).astype(f32)) / GLA_GATE_NORMALIZER
    n_chunks = S // GLA_CHUNK

    def chunks(t, d):
        t = t.astype(f32).reshape(B, n_chunks, GLA_CHUNK, GLA_HEADS, d)
        return jnp.transpose(t, (1, 0, 3, 2, 4))

    qc = chunks(q, GLA_DK) * (GLA_DK ** -0.5)
    kc = chunks(k, GLA_DK)
    vc = chunks(v, GLA_DV)
    gc = chunks(gk, GLA_DK)
    causal = jnp.tril(jnp.ones((GLA_CHUNK, GLA_CHUNK), dtype=bool))

    def step(state, inp):
        qb, kb, vb, gb = inp
        b = jnp.cumsum(gb, axis=2)
        diff = b[:, :, :, None, :] - b[:, :, None, :, :]
        decay = jnp.exp(jnp.where(causal[:, :, None], diff, -jnp.inf))
        scores = jnp.einsum('bhid,bhjd,bhijd->bhij', qb, kb, decay)
        o = (jnp.einsum('bhij,bhje->bhie', scores, vb)
             + jnp.einsum('bhid,bhde->bhie', qb * jnp.exp(b), state))
        b_last = b[:, :, -1:, :]
        state = (jnp.exp(b_last[:, :, 0, :])[..., None] * state
                 + jnp.einsum('bhjd,bhje->bhde', kb * jnp.exp(b_last - b), vb))
        return state, o

    state0 = jnp.zeros((B, GLA_HEADS, GLA_DK, GLA_DV), f32)
    _, o = lax.scan(step, state0, (qc, kc, vc, gc))
    o = jnp.transpose(o, (1, 0, 3, 2, 4)).reshape(B, S, GLA_HEADS, GLA_DV)
    o = o * lax.rsqrt(jnp.mean(o * o, axis=-1, keepdims=True) + NORM_EPS) * norm_g.astype(f32)
    return o.reshape(B, S, GLA_WIDTH).astype(z.dtype) * jax.nn.silu(g)


def _rwkv7_mixer(z, mu, w0, w2, a0, a2, g2, k_k, k_a, r_k, gn_w, gn_b):
    B, S, _ = z.shape
    f32 = jnp.float32
    z = _token_shift(z, mu)
    r, k, v, w_low, a_low, g_low = jnp.split(z, RWKV_SPLITS, axis=-1)
    w_log = -jax.nn.softplus(-(w0 + jnp.tanh(w_low) @ w2).astype(f32)) - 0.5
    decay = jnp.exp(-jnp.exp(w_log))
    a = jax.nn.sigmoid((a0 + a_low @ a2).astype(f32))
    g = jax.nn.sigmoid(g_low) @ g2

    def heads(t):
        return t.astype(f32).reshape(B, S, RWKV_HEADS, RWKV_HEAD)

    kk = heads(k * k_k)
    kk = kk / jnp.maximum(jnp.sqrt(jnp.sum(kk * kk, axis=-1, keepdims=True)), 1e-12)
    kf = k.astype(f32) * (1.0 + (a - 1.0) * k_a.astype(f32))
    rh, kh, vh, wh, ah = heads(r), heads(kf), heads(v), heads(decay), heads(a)

    def tm(t):
        return jnp.moveaxis(t, 1, 0)

    def step(state, inp):
        r_t, w_t, k_t, v_t, a_t, b_t = inp
        sa = jnp.einsum('bhij,bhj->bhi', state, a_t)
        state = (state * w_t[:, :, None, :] + sa[..., None] * b_t[:, :, None, :]
                 + v_t[..., None] * k_t[:, :, None, :])
        return state, jnp.einsum('bhij,bhj->bhi', state, r_t)

    state0 = jnp.zeros((B, RWKV_HEADS, RWKV_HEAD, RWKV_HEAD), f32)
    _, y = lax.scan(step, state0, (tm(rh), tm(wh), tm(kh), tm(vh), tm(-kk), tm(kk * ah)))
    y = jnp.moveaxis(y, 0, 1)
    mean = jnp.mean(y, axis=-1, keepdims=True)
    var = jnp.mean(jnp.square(y - mean), axis=-1, keepdims=True)
    y = ((y - mean) * lax.rsqrt(var + RWKV_GN_EPS)).reshape(B, S, RWKV_WIDTH)
    y = y * gn_w.astype(f32) + gn_b.astype(f32)
    bonus = jnp.sum(rh * kh * r_k.astype(f32), axis=-1, keepdims=True) * vh
    y = y + bonus.reshape(B, S, RWKV_WIDTH)
    return y.astype(z.dtype) * g


def setup_inputs(seed: int = 0) -> dict:
    key = jax.random.key(seed)
    ks = jax.random.split(key, 32)
    nrm = lambda k, shape, scale: jax.random.normal(k, shape, jnp.float32) * scale
    gain = lambda k, shape: 1.0 + 0.02 * jax.random.normal(k, shape, jnp.float32)
    L = DEPTH
    return {
        "x": nrm(ks[0], (BATCH, SEQ, D_MODEL), 1.0),
        "p": nrm(ks[1], (DEPTH, BATCH, SEQ, PLE_DIM), 1.0),
        "ln_mix": gain(ks[2], (L, D_MODEL)),
        "w_in": nrm(ks[3], (L, D_MODEL, D_IN), D_MODEL ** -0.5),
        "gla_conv_w": nrm(ks[4], (L, GLA_CONV, 2 * GLA_KEY_WIDTH + GLA_WIDTH), GLA_CONV ** -0.5),
        "gla_gk_w": nrm(ks[5], (L, GLA_GATE_RANK, GLA_KEY_WIDTH), GLA_GATE_RANK ** -0.5),
        "gla_gk_b": nrm(ks[6], (L, GLA_KEY_WIDTH), 0.1),
        "gla_norm_g": gain(ks[7], (L, GLA_DV)),
        "rwkv_mu": jax.random.uniform(ks[8], (L, RWKV_COLS), jnp.float32, 0.0, 1.0),
        "rwkv_w0": jax.random.uniform(ks[9], (L, RWKV_WIDTH), jnp.float32, -6.0, -1.0),
        "rwkv_w2": nrm(ks[10], (L, RWKV_DECAY_RANK, RWKV_WIDTH), 0.5 * RWKV_DECAY_RANK ** -0.5),
        "rwkv_a0": nrm(ks[11], (L, RWKV_WIDTH), 0.1),
        "rwkv_a2": nrm(ks[12], (L, RWKV_AAA_RANK, RWKV_WIDTH), RWKV_AAA_RANK ** -0.5),
        "rwkv_g2": nrm(ks[13], (L, RWKV_GATE_RANK, RWKV_WIDTH), RWKV_GATE_RANK ** -0.5),
        "rwkv_k_k": 0.85 + nrm(ks[14], (L, RWKV_WIDTH), 0.02),
        "rwkv_k_a": gain(ks[15], (L, RWKV_WIDTH)),
        "rwkv_r_k": nrm(ks[16], (L, RWKV_HEADS, RWKV_HEAD), 0.1),
        "rwkv_gn_w": gain(ks[17], (L, RWKV_WIDTH)),
        "rwkv_gn_b": nrm(ks[18], (L, RWKV_WIDTH), 0.02),
        "w_out": nrm(ks[19], (L, MIX_WIDTH, D_MODEL), MIX_WIDTH ** -0.5),
        "ln_mlp": gain(ks[20], (L, D_MODEL)),
        "w_ff1": nrm(ks[21], (L, D_MODEL, D_FF), D_MODEL ** -0.5),
        "w_ff2": nrm(ks[22], (L, D_FF, D_MODEL), D_FF ** -0.5),
        "ln_ple": gain(ks[23], (L, D_MODEL)),
        "w_ple_gate": nrm(ks[24], (L, D_MODEL, D_MODEL), D_MODEL ** -0.5),
        "w_ple_proj": nrm(ks[25], (L, PLE_DIM, D_MODEL), PLE_DIM ** -0.5),
        "ln_final": gain(ks[26], (D_MODEL,)),
    }


def reference(x, p, ln_mix, w_in, gla_conv_w, gla_gk_w, gla_gk_b, gla_norm_g,
              rwkv_mu, rwkv_w0, rwkv_w2, rwkv_a0, rwkv_a2, rwkv_g2, rwkv_k_k, rwkv_k_a,
              rwkv_r_k, rwkv_gn_w, rwkv_gn_b, w_out, ln_mlp, w_ff1, w_ff2,
              ln_ple, w_ple_gate, w_ple_proj, ln_final):
    h = x
    for i in range(DEPTH):
        z = _rmsnorm(h, ln_mix[i]) @ w_in[i]
        z_gla, z_rwkv = z[..., :GLA_COLS], z[..., GLA_COLS:]
        o_gla = _gla_mixer(z_gla, gla_conv_w[i], gla_gk_w[i], gla_gk_b[i], gla_norm_g[i])
        o_rwkv = _rwkv7_mixer(z_rwkv, rwkv_mu[i], rwkv_w0[i], rwkv_w2[i], rwkv_a0[i],
                              rwkv_a2[i], rwkv_g2[i], rwkv_k_k[i], rwkv_k_a[i],
                              rwkv_r_k[i], rwkv_gn_w[i], rwkv_gn_b[i])
        h = h + jnp.concatenate([o_gla, o_rwkv], axis=-1) @ w_out[i]
        h = h + jnp.square(jax.nn.relu(_rmsnorm(h, ln_mlp[i]) @ w_ff1[i])) @ w_ff2[i]
        gate = jax.nn.sigmoid(_rmsnorm(h, ln_ple[i]) @ w_ple_gate[i])
        h = h + gate * (p[i] @ w_ple_proj[i])
    return _rmsnorm(h, ln_final)
```

```python
import functools

import jax
import jax.numpy as jnp
from jax import lax
from jax.experimental import pallas as pl
from jax.experimental.pallas import tpu as pltpu

F32 = jnp.float32
BF16 = jnp.bfloat16

NORM_EPS = 1e-6
GLA_GATE_NORMALIZER = 16.0
RWKV_GN_EPS = 64e-5
RWKV_HEAD = 64
CHUNK = 64
LANES = 128
HALO = 8
VMEM_LIMIT_BYTES = 56 * 1024 * 1024


def _round_up(n, m):
    return (n + m - 1) // m * m


def _sigmoid(x):
    return 1.0 / (1.0 + jnp.exp(-x))


def _softplus(x):
    return jnp.maximum(x, 0.0) + jnp.log(1.0 + jnp.exp(-jnp.abs(x)))


def _dot(a, b):
    return jnp.dot(a, b, preferred_element_type=F32)


def _dot_nt(a, b):
    return lax.dot_general(a, b, (((1,), (1,)), ((), ())), preferred_element_type=F32)


def _dot_tn(a, b):
    return lax.dot_general(a, b, (((0,), (0,)), ((), ())), preferred_element_type=F32)


def _split2(x):
    hi = x.astype(BF16)
    lo = (x - hi.astype(F32)).astype(BF16)
    return hi, lo


def _split3(x):
    hi = x.astype(BF16)
    r1 = x - hi.astype(F32)
    mid = r1.astype(BF16)
    lo = (r1 - mid.astype(F32)).astype(BF16)
    return hi, mid, lo


def _dot_exact_lhs(a_bf16, x):
    hi, mid, lo = _split3(x)
    return _dot(a_bf16, hi) + _dot(a_bf16, mid) + _dot(a_bf16, lo)


def _dot_exact_rhs(x, b_bf16):
    hi, lo = _split2(x)
    return _dot(hi, b_bf16) + _dot(lo, b_bf16)


def _dot_f32(a, b):
    a_hi, a_lo = _split2(a)
    b_hi, b_lo = _split2(b)
    return _dot(a_hi, b_hi) + _dot(a_hi, b_lo) + _dot(a_lo, b_hi)


def _rmsnorm(x, g):
    ms = jnp.mean(x * x, axis=-1, keepdims=True)
    return x * lax.rsqrt(ms + NORM_EPS) * g


def _tri_masks(n):
    row = lax.broadcasted_iota(jnp.int32, (n, n), 0)
    col = lax.broadcasted_iota(jnp.int32, (n, n), 1)
    return col <= row, col < row


def _in_proj_kernel(x_ref, g_ref, wg_ref, wr_ref, zg_ref, zr_ref):
    xn = _rmsnorm(x_ref[...], g_ref[...]).astype(BF16)
    zg_ref[...] = _dot(xn, wg_ref[...])
    zr_ref[...] = _dot(xn, wr_ref[...])


def _in_proj(x2, ln, wg, wr, tm):
    t, d = x2.shape
    ng, nr = wg.shape[1], wr.shape[1]
    const = lambda i: (0, 0)
    return pl.pallas_call(
        _in_proj_kernel,
        grid=(t // tm,),
        in_specs=[
            pl.BlockSpec((tm, d), lambda i: (i, 0)),
            pl.BlockSpec((1, d), const),
            pl.BlockSpec((d, ng), const, pipeline_mode=pl.Buffered(1)),
            pl.BlockSpec((d, nr), const, pipeline_mode=pl.Buffered(1)),
        ],
        out_specs=[
            pl.BlockSpec((tm, ng), lambda i: (i, 0)),
            pl.BlockSpec((tm, nr), lambda i: (i, 0)),
        ],
        out_shape=[
            jax.ShapeDtypeStruct((t, ng), F32),
            jax.ShapeDtypeStruct((t, nr), F32),
        ],
        compiler_params=pltpu.CompilerParams(
            dimension_semantics=("arbitrary",), vmem_limit_bytes=VMEM_LIMIT_BYTES),
        name="in_proj",
    )(x2, ln, wg, wr)


def _gla_kernel(z_ref, cw_ref, gkw_ref, gkb_ref, ng_ref, tri_ref, o_ref,
                xb_ref, st_ref, *, heads, dk, dv):
    kw = heads * dk
    vw = heads * dv
    nqkv = 2 * kw + vw
    taps = cw_ref.shape[0]
    pairs = kw // LANES

    @pl.when(pl.program_id(1) == 0)
    def _():
        xb_ref[0:HALO, :] = jnp.zeros((HALO, nqkv), F32)
        st_ref[...] = jnp.zeros_like(st_ref)

    z = z_ref[0]
    xb_ref[HALO:HALO + CHUNK, :] = z[:, :nqkv]
    conv = jnp.zeros((CHUNK, nqkv), F32)
    for t in range(taps):
        off = HALO - (taps - 1) + t
        conv = conv + cw_ref[t:t + 1, :] * xb_ref[off:off + CHUNK, :]
    xb_ref[0:HALO, :] = xb_ref[CHUNK:CHUNK + HALO, :]
    qkv = conv * _sigmoid(conv)
    q = qkv[:, :kw] * (dk ** -0.5)
    k = qkv[:, kw:2 * kw]
    v = qkv[:, 2 * kw:nqkv]
    g = z[:, nqkv:nqkv + vw]
    low = z[:, nqkv + vw:]

    gk_pre = _dot(low, gkw_ref[...]) + gkb_ref[...]
    gk = -_softplus(-gk_pre) * (1.0 / GLA_GATE_NORMALIZER)
    b = _dot_exact_lhs(tri_ref[...], gk)
    b_last = b[CHUNK - 1:CHUNK, :]
    b_ref = 0.5 * b_last
    qe = q * jnp.exp(b - b_ref)
    ke = k * jnp.exp(b_ref - b)
    qb = q * jnp.exp(b)
    kd = k * jnp.exp(b_last - b)
    e_last = jnp.exp(b_last)

    causal, _ = _tri_masks(CHUNK)
    lane = lax.broadcasted_iota(jnp.int32, (1, LANES), 1)
    head_mask = [(lane < dk).astype(F32), (lane >= dk).astype(F32)]
    srow = lax.broadcasted_iota(jnp.int32, (2 * dv, LANES), 0)
    scol = lax.broadcasted_iota(jnp.int32, (2 * dv, LANES), 1)
    st_mask = ((srow < dv) == (scol < dk)).astype(F32)

    o_parts = []
    for p in range(pairs):
        ls = slice(p * LANES, (p + 1) * LANES)
        vs = slice(2 * p * dv, 2 * (p + 1) * dv)
        st = st_ref[p]
        o_inter = _dot_nt(qb[:, ls], st)
        for e in range(2):
            h = 2 * p + e
            s = _dot_nt(qe[:, ls] * head_mask[e], ke[:, ls])
            s = jnp.where(causal, s, 0.0)
            o_h = _dot(s, v[:, h * dv:(h + 1) * dv]) + o_inter[:, e * dv:(e + 1) * dv]
            ms = jnp.mean(o_h * o_h, axis=-1, keepdims=True)
            o_parts.append(o_h * lax.rsqrt(ms + NORM_EPS) * ng_ref[...])
        st_ref[p] = st * e_last[:, ls] + st_mask * _dot_tn(v[:, vs], kd[:, ls])
    o = jnp.concatenate(o_parts, axis=1)
    o_ref[0] = (o * (g * _sigmoid(g))).astype(o_ref.dtype)


def _gla_mixer(zg, conv_w, gkw_p, gk_b, norm_g, tri, *, heads, dk, dv):
    bsz, s, cols = zg.shape
    kw, vw = heads * dk, heads * dv
    nqkv = 2 * kw + vw
    const2 = lambda b, j: (0, 0)
    kern = functools.partial(_gla_kernel, heads=heads, dk=dk, dv=dv)
    return pl.pallas_call(
        kern,
        grid=(bsz, s // CHUNK),
        in_specs=[
            pl.BlockSpec((1, CHUNK, cols), lambda b, j: (b, j, 0)),
            pl.BlockSpec(conv_w.shape, const2),
            pl.BlockSpec(gkw_p.shape, const2),
            pl.BlockSpec(gk_b.shape, const2),
            pl.BlockSpec(norm_g.shape, const2),
            pl.BlockSpec(tri.shape, const2),
        ],
        out_specs=pl.BlockSpec((1, CHUNK, vw), lambda b, j: (b, j, 0)),
        out_shape=jax.ShapeDtypeStruct((bsz, s, vw), BF16),
        scratch_shapes=[
            pltpu.VMEM((CHUNK + HALO, nqkv), F32),
            pltpu.VMEM((kw // LANES, 2 * dv, LANES), F32),
        ],
        compiler_params=pltpu.CompilerParams(
            dimension_semantics=("arbitrary", "arbitrary"), vmem_limit_bytes=VMEM_LIMIT_BYTES),
        name="gla_mixer",
    )(zg, conv_w, gkw_p, gk_b, norm_g, tri)


def _rwkv_kernel(z_ref, mu_ref, w0_ref, w2_ref, a0_ref, a2_ref, g2_ref, kk_ref, ka_ref,
                 rk_ref, gnw_ref, gnb_ref, tri_ref, bd_ref, o_ref,
                 xb_ref, st_ref, *, width):
    n = RWKV_HEAD
    pairs = width // LANES
    levels = (CHUNK - 1).bit_length()

    @pl.when(pl.program_id(1) == 0)
    def _():
        xb_ref[0:HALO, :] = jnp.zeros((HALO, xb_ref.shape[1]), F32)
        st_ref[...] = jnp.zeros_like(st_ref)

    z = z_ref[0]
    xb_ref[HALO:HALO + CHUNK, :] = z
    zp = xb_ref[HALO - 1:HALO - 1 + CHUNK, :]
    xb_ref[0:HALO, :] = xb_ref[CHUNK:CHUNK + HALO, :]
    zs = z + mu_ref[...] * (zp - z)
    r = zs[:, 0:width]
    k = zs[:, width:2 * width]
    v = zs[:, 2 * width:3 * width]
    low = zs[:, 3 * width:]

    w_pre = w0_ref[...] + _dot(jnp.tanh(low), w2_ref[...])
    logw = -jnp.exp(-_softplus(-w_pre) - 0.5)
    a_sig = _sigmoid(a0_ref[...] + _dot(low, a2_ref[...]))
    g = _dot(_sigmoid(low), g2_ref[...])

    bd = bd_ref[...]
    kk = k * kk_ref[...]
    kk = kk / jnp.maximum(jnp.sqrt(_dot_exact_rhs(kk * kk, bd)), 1e-12)
    kf = k * (1.0 + (a_sig - 1.0) * ka_ref[...])
    av = -kk
    bv = kk * a_sig
    bonus = _dot_exact_rhs(r * kf * rk_ref[...], bd)

    lg = _dot_exact_lhs(tri_ref[...], logw)
    lg_last = lg[CHUNK - 1:CHUNK, :]
    e_inv = jnp.exp(-lg)
    e_rem = jnp.exp(lg_last - lg)
    at = av * jnp.exp(lg - logw)
    rt = r * jnp.exp(lg)
    bt = bv * e_inv
    kt = kf * e_inv
    bh = bv * e_rem
    kh = kf * e_rem
    gam = jnp.exp(lg_last)

    incl, strict = _tri_masks(CHUNK)
    lane = lax.broadcasted_iota(jnp.int32, (1, LANES), 1)
    head_mask = [(lane < n).astype(F32), (lane >= n).astype(F32)]
    prow = lax.broadcasted_iota(jnp.int32, (LANES, LANES), 0)
    pcol = lax.broadcasted_iota(jnp.int32, (LANES, LANES), 1)
    blk = ((prow < n) == (pcol < n)).astype(F32)
    eye = (prow == pcol).astype(F32)

    y_parts = []
    for p in range(pairs):
        ls = slice(p * LANES, (p + 1) * LANES)
        v_p = v[:, ls]
        ahat = uzero = rhat = yzero = None
        for e in range(2):
            m = head_mask[e]
            at_m = at[:, ls] * m
            rt_m = rt[:, ls] * m
            v_m = v_p * m
            lhs = jnp.concatenate([at_m, rt_m], axis=0)
            sb = _dot_nt(lhs, bt[:, ls])
            sk = _dot_nt(lhs, kt[:, ls])
            a_ab = jnp.where(strict, sb[:CHUNK], 0.0)
            a_rb = jnp.where(incl, sb[CHUNK:], 0.0)
            a_ak = jnp.where(strict, sk[:CHUNK], 0.0)
            a_rk = jnp.where(incl, sk[CHUNK:], 0.0)
            x = jnp.concatenate([at_m, _dot(a_ak, v_m)], axis=1)
            pw = a_ab
            for lvl in range(levels):
                x = x + _dot(pw, x)
                if lvl + 1 < levels:
                    pw = _dot(pw, pw)
            zz = _dot(a_rb, x)
            rh_e = rt_m + zz[:, :LANES]
            y0_e = zz[:, LANES:] + _dot(a_rk, v_m)
            if e == 0:
                ahat, uzero, rhat, yzero = x[:, :LANES], x[:, LANES:], rh_e, y0_e
            else:
                ahat, uzero = ahat + x[:, :LANES], uzero + x[:, LANES:]
                rhat, yzero = rhat + rh_e, yzero + y0_e
        lb = jnp.concatenate([bh[:, ls], kh[:, ls]], axis=0)
        rb = jnp.concatenate(
            [jnp.concatenate([ahat, uzero], axis=1),
             jnp.concatenate([jnp.zeros_like(v_p), v_p], axis=1)], axis=0)
        mn = _dot_tn(lb, rb)
        m_mat = blk * mn[:, :LANES] + eye * gam[:, ls]
        n_mat = blk * mn[:, LANES:]
        t_old = st_ref[p]
        y_parts.append(_dot_f32(rhat, t_old) + yzero)
        st_ref[p] = _dot_f32(m_mat, t_old) + n_mat
    y = jnp.concatenate(y_parts, axis=1)

    mean = _dot_exact_rhs(y, bd) * (1.0 / n)
    d = y - mean
    var = _dot_exact_rhs(d * d, bd) * (1.0 / n)
    yn = d * lax.rsqrt(var + RWKV_GN_EPS) * gnw_ref[...] + gnb_ref[...]
    o_ref[0] = ((yn + bonus * v) * g).astype(o_ref.dtype)


def _rwkv_mixer(zr, params, tri, bd, *, width):
    bsz, s, cols = zr.shape
    const2 = lambda b, j: (0, 0)
    kern = functools.partial(_rwkv_kernel, width=width)
    return pl.pallas_call(
        kern,
        grid=(bsz, s // CHUNK),
        in_specs=[pl.BlockSpec((1, CHUNK, cols), lambda b, j: (b, j, 0))]
        + [pl.BlockSpec(a.shape, const2) for a in params]
        + [pl.BlockSpec(tri.shape, const2), pl.BlockSpec(bd.shape, const2)],
        out_specs=pl.BlockSpec((1, CHUNK, width), lambda b, j: (b, j, 0)),
        out_shape=jax.ShapeDtypeStruct((bsz, s, width), BF16),
        scratch_shapes=[
            pltpu.VMEM((CHUNK + HALO, cols), F32),
            pltpu.VMEM((width // LANES, LANES, LANES), F32),
        ],
        compiler_params=pltpu.CompilerParams(
            dimension_semantics=("arbitrary", "arbitrary"), vmem_limit_bytes=VMEM_LIMIT_BYTES),
        name="rwkv_mixer",
    )(zr, *params, tri, bd)


def _tail_kernel(x_ref, og_ref, or_ref, p_ref, wo_ref, lnm_ref, w1_ref, w2_ref,
                 lnp_ref, wg_ref, wp_ref, lnf_ref, out_ref):
    gw = og_ref.shape[1]
    h = x_ref[...] + _dot(og_ref[...], wo_ref[0:gw, :]) + _dot(or_ref[...], wo_ref[gw:, :])
    n1 = _rmsnorm(h, lnm_ref[...]).astype(BF16)
    hid = jnp.maximum(_dot(n1, w1_ref[...]), 0.0)
    h = h + _dot((hid * hid).astype(BF16), w2_ref[...])
    n2 = _rmsnorm(h, lnp_ref[...]).astype(BF16)
    gate = _sigmoid(_dot(n2, wg_ref[...]))
    h = h + gate * _dot(p_ref[...].astype(BF16), wp_ref[...])
    out_ref[...] = _rmsnorm(h, lnf_ref[...])


def _tail(x2, og, orw, p2, wo, lnm, w1, w2, lnp, wg, wp, lnf, tm):
    t, d = x2.shape
    const = lambda i: (0, 0)
    rows = lambda a: pl.BlockSpec((tm, a.shape[1]), lambda i: (i, 0))
    whole = lambda a: pl.BlockSpec(a.shape, const, pipeline_mode=pl.Buffered(1))
    return pl.pallas_call(
        _tail_kernel,
        grid=(t // tm,),
        in_specs=[rows(x2), rows(og), rows(orw), rows(p2), whole(wo), whole(lnm), whole(w1),
                  whole(w2), whole(lnp), whole(wg), whole(wp), whole(lnf)],
        out_specs=pl.BlockSpec((tm, d), lambda i: (i, 0)),
        out_shape=jax.ShapeDtypeStruct((t, d), F32),
        compiler_params=pltpu.CompilerParams(
            dimension_semantics=("arbitrary",), vmem_limit_bytes=VMEM_LIMIT_BYTES),
        name="tail",
    )(x2, og, orw, p2, wo, lnm, w1, w2, lnp, wg, wp, lnf)


def _pad_cols(w, n):
    return jnp.pad(w, ((0, 0), (0, n - w.shape[1])))


def _place_rows(w, start, total):
    return jnp.pad(w, ((start, total - start - w.shape[0]), (0, 0)))


def _layer(h, p_i, ln_mix, w_in, conv_w, gk_w, gk_b, norm_g, mu, w0, w2, a0, a2, g2, k_k, k_a,
           r_k, gn_w, gn_b, w_out, ln_mlp, w_ff1, w_ff2, ln_ple, w_gate, w_proj, ln_next, tm):
    bsz, s, d = h.shape
    t = bsz * s
    row = lambda a: a.reshape(1, -1)

    gate_rank, kw = gk_w.shape
    dv = norm_g.shape[0]
    vw = (conv_w.shape[1] - 2 * kw)
    heads = vw // dv
    dk = kw // heads
    gla_cols = 2 * kw + 2 * vw + gate_rank
    gla_pad = _round_up(gla_cols, LANES)
    width = w0.shape[0]
    rwkv_cols = w_in.shape[1] - gla_cols
    rwkv_pad = _round_up(rwkv_cols, LANES)
    low_pad = rwkv_pad - 3 * width
    dr, ar = w2.shape[0], a2.shape[0]

    wg_in = _pad_cols(w_in[:, :gla_cols], gla_pad).astype(BF16)
    wr_in = _pad_cols(w_in[:, gla_cols:], rwkv_pad).astype(BF16)
    zg, zr = _in_proj(h.reshape(t, d), row(ln_mix), wg_in, wr_in, tm)

    tri = jnp.tril(jnp.ones((CHUNK, CHUNK), F32)).astype(BF16)
    hid = jnp.arange(width) // RWKV_HEAD
    bd = (hid[:, None] == hid[None, :]).astype(BF16)

    gkw_p = _place_rows(gk_w, 0, gla_pad - (gla_cols - gate_rank))
    o_gla = _gla_mixer(zg.reshape(bsz, s, gla_pad), conv_w, gkw_p, row(gk_b),
                       row(norm_g), tri, heads=heads, dk=dk, dv=dv)

    rwkv_params = (
        _pad_cols(row(mu), rwkv_pad), row(w0), _place_rows(w2, 0, low_pad), row(a0),
        _place_rows(a2, dr, low_pad), _place_rows(g2, dr + ar, low_pad), row(k_k), row(k_a),
        row(r_k), row(gn_w), row(gn_b))
    o_rwkv = _rwkv_mixer(zr.reshape(bsz, s, rwkv_pad), rwkv_params, tri, bd, width=width)

    out = _tail(h.reshape(t, d), o_gla.reshape(t, vw), o_rwkv.reshape(t, width),
                p_i.reshape(t, -1), w_out.astype(BF16), row(ln_mlp), w_ff1.astype(BF16),
                w_ff2.astype(BF16), row(ln_ple), w_gate.astype(BF16), w_proj.astype(BF16),
                row(ln_next), tm)
    return out.reshape(bsz, s, d)


def kernel(x, p, ln_mix, w_in, gla_conv_w, gla_gk_w, gla_gk_b, gla_norm_g, rwkv_mu, rwkv_w0,
           rwkv_w2, rwkv_a0, rwkv_a2, rwkv_g2, rwkv_k_k, rwkv_k_a, rwkv_r_k, rwkv_gn_w,
           rwkv_gn_b, w_out, ln_mlp, w_ff1, w_ff2, ln_ple, w_ple_gate, w_ple_proj, ln_final):
    depth = w_in.shape[0]
    assert depth == 1, "the fused tail applies the final norm, so exactly one layer is supported"
    tm = 512
    assert (x.shape[0] * x.shape[1]) % tm == 0 and x.shape[1] % CHUNK == 0
    i = 0
    return _layer(x, p[i], ln_mix[i], w_in[i], gla_conv_w[i], gla_gk_w[i], gla_gk_b[i],
                  gla_norm_g[i], rwkv_mu[i], rwkv_w0[i], rwkv_w2[i], rwkv_a0[i], rwkv_a2[i],
                  rwkv_g2[i], rwkv_k_k[i], rwkv_k_a[i], rwkv_r_k[i].reshape(-1), rwkv_gn_w[i],
                  rwkv_gn_b[i], w_out[i], ln_mlp[i], w_ff1[i], w_ff2[i], ln_ple[i],
                  w_ple_gate[i], w_ple_proj[i], ln_final, tm)
```

```python
import functools

import jax
import jax.numpy as jnp
from jax import lax
from jax.experimental import pallas as pl
from jax.experimental.pallas import tpu as pltpu

F32 = jnp.float32
BF16 = jnp.bfloat16

NORM_EPS = 1e-6
GLA_GATE_NORMALIZER = 16.0
RWKV_GN_EPS = 64e-5
RWKV_HEAD = 64
CHUNK = 64
LANES = 128
HALO = 8
VMEM_LIMIT_BYTES = 56 * 1024 * 1024


def _round_up(n, m):
    return (n + m - 1) // m * m


def _sigmoid(x):
    return 1.0 / (1.0 + jnp.exp(-x))


def _softplus(x):
    return jnp.maximum(x, 0.0) + jnp.log(1.0 + jnp.exp(-jnp.abs(x)))


def _dot(a, b):
    return jnp.dot(a, b, preferred_element_type=F32)


def _dot_nt(a, b):
    return lax.dot_general(a, b, (((1,), (1,)), ((), ())), preferred_element_type=F32)


def _dot_tn(a, b):
    return lax.dot_general(a, b, (((0,), (0,)), ((), ())), preferred_element_type=F32)


def _split2(x):
    hi = x.astype(BF16)
    lo = (x - hi.astype(F32)).astype(BF16)
    return hi, lo


def _split3(x):
    hi = x.astype(BF16)
    r1 = x - hi.astype(F32)
    mid = r1.astype(BF16)
    lo = (r1 - mid.astype(F32)).astype(BF16)
    return hi, mid, lo


def _dot_exact_lhs(a_bf16, x):
    hi, mid, lo = _split3(x)
    return _dot(a_bf16, hi) + _dot(a_bf16, mid) + _dot(a_bf16, lo)


def _dot_exact_rhs(x, b_bf16):
    hi, lo = _split2(x)
    return _dot(hi, b_bf16) + _dot(lo, b_bf16)


def _dot_f32(a, b):
    a_hi, a_lo = _split2(a)
    b_hi, b_lo = _split2(b)
    return _dot(a_hi, b_hi) + _dot(a_hi, b_lo) + _dot(a_lo, b_hi)


def _rmsnorm(x, g):
    ms = jnp.mean(x * x, axis=-1, keepdims=True)
    return x * lax.rsqrt(ms + NORM_EPS) * g


def _tri_masks(n):
    row = lax.broadcasted_iota(jnp.int32, (n, n), 0)
    col = lax.broadcasted_iota(jnp.int32, (n, n), 1)
    return col <= row, col < row


def _in_proj_kernel(x_ref, g_ref, wg_ref, wr_ref, zg_ref, zr_ref):
    xn = _rmsnorm(x_ref[...], g_ref[...]).astype(BF16)
    zg_ref[...] = _dot(xn, wg_ref[...])
    zr_ref[...] = _dot(xn, wr_ref[...])


def _in_proj(x2, ln, wg, wr, tm):
    t, d = x2.shape
    ng, nr = wg.shape[1], wr.shape[1]
    const = lambda i: (0, 0)
    return pl.pallas_call(
        _in_proj_kernel,
        grid=(t // tm,),
        in_specs=[
            pl.BlockSpec((tm, d), lambda i: (i, 0)),
            pl.BlockSpec((1, d), const),
            pl.BlockSpec((d, ng), const, pipeline_mode=pl.Buffered(1)),
            pl.BlockSpec((d, nr), const, pipeline_mode=pl.Buffered(1)),
        ],
        out_specs=[
            pl.BlockSpec((tm, ng), lambda i: (i, 0)),
            pl.BlockSpec((tm, nr), lambda i: (i, 0)),
        ],
        out_shape=[
            jax.ShapeDtypeStruct((t, ng), F32),
            jax.ShapeDtypeStruct((t, nr), F32),
        ],
        compiler_params=pltpu.CompilerParams(
            dimension_semantics=("arbitrary",), vmem_limit_bytes=VMEM_LIMIT_BYTES),
        name="in_proj",
    )(x2, ln, wg, wr)


def _gla_kernel(z_ref, cw_ref, gkw_ref, gkb_ref, ng_ref, tri_ref, o_ref,
                xb_ref, st_ref, *, heads, dk, dv):
    kw = heads * dk
    vw = heads * dv
    nqkv = 2 * kw + vw
    taps = cw_ref.shape[0]
    pairs = kw // LANES

    @pl.when(pl.program_id(1) == 0)
    def _():
        xb_ref[0:HALO, :] = jnp.zeros((HALO, nqkv), F32)
        st_ref[...] = jnp.zeros_like(st_ref)

    z = z_ref[0]
    xb_ref[HALO:HALO + CHUNK, :] = z[:, :nqkv]
    conv = jnp.zeros((CHUNK, nqkv), F32)
    for t in range(taps):
        off = HALO - (taps - 1) + t
        conv = conv + cw_ref[t:t + 1, :] * xb_ref[off:off + CHUNK, :]
    xb_ref[0:HALO, :] = xb_ref[CHUNK:CHUNK + HALO, :]
    qkv = conv * _sigmoid(conv)
    q = qkv[:, :kw] * (dk ** -0.5)
    k = qkv[:, kw:2 * kw]
    v = qkv[:, 2 * kw:nqkv]
    g = z[:, nqkv:nqkv + vw]
    low = z[:, nqkv + vw:]

    gk_pre = _dot(low, gkw_ref[...]) + gkb_ref[...]
    gk = -_softplus(-gk_pre) * (1.0 / GLA_GATE_NORMALIZER)
    b = _dot_exact_lhs(tri_ref[...], gk)
    b_last = b[CHUNK - 1:CHUNK, :]
    b_ref = 0.5 * b_last
    qe = q * jnp.exp(b - b_ref)
    ke = k * jnp.exp(b_ref - b)
    qb = q * jnp.exp(b)
    kd = k * jnp.exp(b_last - b)
    e_last = jnp.exp(b_last)

    causal, _ = _tri_masks(CHUNK)
    lane = lax.broadcasted_iota(jnp.int32, (1, LANES), 1)
    head_mask = [(lane < dk).astype(F32), (lane >= dk).astype(F32)]
    srow = lax.broadcasted_iota(jnp.int32, (2 * dv, LANES), 0)
    scol = lax.broadcasted_iota(jnp.int32, (2 * dv, LANES), 1)
    st_mask = ((srow < dv) == (scol < dk)).astype(F32)

    o_parts = []
    for p in range(pairs):
        ls = slice(p * LANES, (p + 1) * LANES)
        vs = slice(2 * p * dv, 2 * (p + 1) * dv)
        st = st_ref[p]
        o_inter = _dot_nt(qb[:, ls], st)
        for e in range(2):
            h = 2 * p + e
            s = _dot_nt(qe[:, ls] * head_mask[e], ke[:, ls])
            s = jnp.where(causal, s, 0.0)
            o_h = _dot(s, v[:, h * dv:(h + 1) * dv]) + o_inter[:, e * dv:(e + 1) * dv]
            ms = jnp.mean(o_h * o_h, axis=-1, keepdims=True)
            o_parts.append(o_h * lax.rsqrt(ms + NORM_EPS) * ng_ref[...])
        st_ref[p] = st * e_last[:, ls] + st_mask * _dot_tn(v[:, vs], kd[:, ls])
    o = jnp.concatenate(o_parts, axis=1)
    o_ref[0] = (o * (g * _sigmoid(g))).astype(o_ref.dtype)


def _gla_mixer(zg, conv_w, gkw_p, gk_b, norm_g, tri, *, heads, dk, dv):
    bsz, s, cols = zg.shape
    kw, vw = heads * dk, heads * dv
    nqkv = 2 * kw + vw
    const2 = lambda b, j: (0, 0)
    kern = functools.partial(_gla_kernel, heads=heads, dk=dk, dv=dv)
    return pl.pallas_call(
        kern,
        grid=(bsz, s // CHUNK),
        in_specs=[
            pl.BlockSpec((1, CHUNK, cols), lambda b, j: (b, j, 0)),
            pl.BlockSpec(conv_w.shape, const2),
            pl.BlockSpec(gkw_p.shape, const2),
            pl.BlockSpec(gk_b.shape, const2),
            pl.BlockSpec(norm_g.shape, const2),
            pl.BlockSpec(tri.shape, const2),
        ],
        out_specs=pl.BlockSpec((1, CHUNK, vw), lambda b, j: (b, j, 0)),
        out_shape=jax.ShapeDtypeStruct((bsz, s, vw), BF16),
        scratch_shapes=[
            pltpu.VMEM((CHUNK + HALO, nqkv), F32),
            pltpu.VMEM((kw // LANES, 2 * dv, LANES), F32),
        ],
        compiler_params=pltpu.CompilerParams(
            dimension_semantics=("arbitrary", "arbitrary"), vmem_limit_bytes=VMEM_LIMIT_BYTES),
        name="gla_mixer",
    )(zg, conv_w, gkw_p, gk_b, norm_g, tri)


def _rwkv_kernel(z_ref, mu_ref, w0_ref, w2_ref, a0_ref, a2_ref, g2_ref, kk_ref, ka_ref,
                 rk_ref, gnw_ref, gnb_ref, tri_ref, bd_ref, o_ref,
                 xb_ref, st_ref, *, width):
    n = RWKV_HEAD
    pairs = width // LANES
    levels = (CHUNK - 1).bit_length()

    @pl.when(pl.program_id(1) == 0)
    def _():
        xb_ref[0:HALO, :] = jnp.zeros((HALO, xb_ref.shape[1]), F32)
        st_ref[...] = jnp.zeros_like(st_ref)

    z = z_ref[0]
    xb_ref[HALO:HALO + CHUNK, :] = z
    zp = xb_ref[HALO - 1:HALO - 1 + CHUNK, :]
    xb_ref[0:HALO, :] = xb_ref[CHUNK:CHUNK + HALO, :]
    zs = z + mu_ref[...] * (zp - z)
    r = zs[:, 0:width]
    k = zs[:, width:2 * width]
    v = zs[:, 2 * width:3 * width]
    low = zs[:, 3 * width:]

    w_pre = w0_ref[...] + _dot(jnp.tanh(low), w2_ref[...])
    logw = -jnp.exp(-_softplus(-w_pre) - 0.5)
    a_sig = _sigmoid(a0_ref[...] + _dot(low, a2_ref[...]))
    g = _dot(_sigmoid(low), g2_ref[...])

    bd = bd_ref[...]
    kk = k * kk_ref[...]
    kk = kk / jnp.maximum(jnp.sqrt(_dot_exact_rhs(kk * kk, bd)), 1e-12)
    kf = k * (1.0 + (a_sig - 1.0) * ka_ref[...])
    av = -kk
    bv = kk * a_sig
    bonus = _dot_exact_rhs(r * kf * rk_ref[...], bd)

    lg = _dot_exact_lhs(tri_ref[...], logw)
    lg_last = lg[CHUNK - 1:CHUNK, :]
    e_inv = jnp.exp(-lg)
    e_rem = jnp.exp(lg_last - lg)
    at = av * jnp.exp(lg - logw)
    rt = r * jnp.exp(lg)
    bt = bv * e_inv
    kt = kf * e_inv
    bh = bv * e_rem
    kh = kf * e_rem
    gam = jnp.exp(lg_last)

    incl, strict = _tri_masks(CHUNK)
    lane = lax.broadcasted_iota(jnp.int32, (1, LANES), 1)
    head_mask = [(lane < n).astype(F32), (lane >= n).astype(F32)]
    prow = lax.broadcasted_iota(jnp.int32, (LANES, LANES), 0)
    pcol = lax.broadcasted_iota(jnp.int32, (LANES, LANES), 1)
    blk = ((prow < n) == (pcol < n)).astype(F32)
    eye = (prow == pcol).astype(F32)

    hl = [(p, e) for p in range(pairs) for e in range(2)]
    lsl = [slice(p * LANES, (p + 1) * LANES) for p in range(pairs)]
    bt_b = [bt[:, ls].astype(BF16) for ls in lsl]
    kt_b = [kt[:, ls].astype(BF16) for ls in lsl]
    at_m = [at[:, lsl[p]] * head_mask[e] for p, e in hl]
    rt_m = [rt[:, lsl[p]] * head_mask[e] for p, e in hl]
    v_m = [(v[:, lsl[p]] * head_mask[e]).astype(BF16) for p, e in hl]
    lhs = [jnp.concatenate([a_, r_], axis=0).astype(BF16) for a_, r_ in zip(at_m, rt_m)]
    sb = [_dot_nt(lhs[i], bt_b[p]) for i, (p, e) in enumerate(hl)]
    sk = [_dot_nt(lhs[i], kt_b[p]) for i, (p, e) in enumerate(hl)]
    a_ab = [jnp.where(strict, s_[:CHUNK], 0.0).astype(BF16) for s_ in sb]
    a_rb = [jnp.where(incl, s_[CHUNK:], 0.0).astype(BF16) for s_ in sb]
    a_ak = [jnp.where(strict, s_[:CHUNK], 0.0).astype(BF16) for s_ in sk]
    a_rk = [jnp.where(incl, s_[CHUNK:], 0.0).astype(BF16) for s_ in sk]
    x = [jnp.concatenate([at_m[i], _dot(a_ak[i], v_m[i])], axis=1) for i in range(len(hl))]
    pw = a_ab
    for lvl in range(levels):
        x = [x_ + _dot(p_, x_.astype(BF16)) for x_, p_ in zip(x, pw)]
        if lvl + 1 < levels:
            pw = [_dot(p_, p_).astype(BF16) for p_ in pw]
    zz = [_dot(a_, x_.astype(BF16)) for a_, x_ in zip(a_rb, x)]
    rk_v = [_dot(a_, v_) for a_, v_ in zip(a_rk, v_m)]

    y_parts = []
    for p in range(pairs):
        ls = lsl[p]
        i0, i1 = 2 * p, 2 * p + 1
        xp = x[i0] + x[i1]
        rhat = rt[:, ls] + zz[i0][:, :LANES] + zz[i1][:, :LANES]
        yzero = zz[i0][:, LANES:] + zz[i1][:, LANES:] + rk_v[i0] + rk_v[i1]
        v_p = v[:, ls]
        lb = jnp.concatenate([bh[:, ls], kh[:, ls]], axis=0).astype(BF16)
        rb = jnp.concatenate(
            [xp, jnp.concatenate([jnp.zeros_like(v_p), v_p], axis=1)], axis=0).astype(BF16)
        mn = _dot_tn(lb, rb)
        m_mat = blk * mn[:, :LANES] + eye * gam[:, ls]
        n_mat = blk * mn[:, LANES:]
        t_old = st_ref[p]
        y_parts.append(_dot_f32(rhat, t_old) + yzero)
        st_ref[p] = _dot_f32(m_mat, t_old) + n_mat
    y = jnp.concatenate(y_parts, axis=1)

    mean = _dot_exact_rhs(y, bd) * (1.0 / n)
    d = y - mean
    var = _dot_exact_rhs(d * d, bd) * (1.0 / n)
    yn = d * lax.rsqrt(var + RWKV_GN_EPS) * gnw_ref[...] + gnb_ref[...]
    o_ref[0] = ((yn + bonus * v) * g).astype(o_ref.dtype)


def _rwkv_mixer(zr, params, tri, bd, *, width):
    bsz, s, cols = zr.shape
    const2 = lambda b, j: (0, 0)
    kern = functools.partial(_rwkv_kernel, width=width)
    return pl.pallas_call(
        kern,
        grid=(bsz, s // CHUNK),
        in_specs=[pl.BlockSpec((1, CHUNK, cols), lambda b, j: (b, j, 0))]
        + [pl.BlockSpec(a.shape, const2) for a in params]
        + [pl.BlockSpec(tri.shape, const2), pl.BlockSpec(bd.shape, const2)],
        out_specs=pl.BlockSpec((1, CHUNK, width), lambda b, j: (b, j, 0)),
        out_shape=jax.ShapeDtypeStruct((bsz, s, width), BF16),
        scratch_shapes=[
            pltpu.VMEM((CHUNK + HALO, cols), F32),
            pltpu.VMEM((width // LANES, LANES, LANES), F32),
        ],
        compiler_params=pltpu.CompilerParams(
            dimension_semantics=("arbitrary", "arbitrary"), vmem_limit_bytes=VMEM_LIMIT_BYTES),
        name="rwkv_mixer",
    )(zr, *params, tri, bd)


def _tail_kernel(x_ref, og_ref, or_ref, p_ref, wo_ref, lnm_ref, w1_ref, w2_ref,
                 lnp_ref, wg_ref, wp_ref, lnf_ref, out_ref):
    gw = og_ref.shape[1]
    h = x_ref[...] + _dot(og_ref[...], wo_ref[0:gw, :]) + _dot(or_ref[...], wo_ref[gw:, :])
    n1 = _rmsnorm(h, lnm_ref[...]).astype(BF16)
    hid = jnp.maximum(_dot(n1, w1_ref[...]), 0.0)
    h = h + _dot((hid * hid).astype(BF16), w2_ref[...])
    n2 = _rmsnorm(h, lnp_ref[...]).astype(BF16)
    gate = _sigmoid(_dot(n2, wg_ref[...]))
    h = h + gate * _dot(p_ref[...].astype(BF16), wp_ref[...])
    out_ref[...] = _rmsnorm(h, lnf_ref[...])


def _tail(x2, og, orw, p2, wo, lnm, w1, w2, lnp, wg, wp, lnf, tm):
    t, d = x2.shape
    const = lambda i: (0, 0)
    rows = lambda a: pl.BlockSpec((tm, a.shape[1]), lambda i: (i, 0))
    whole = lambda a: pl.BlockSpec(a.shape, const, pipeline_mode=pl.Buffered(1))
    return pl.pallas_call(
        _tail_kernel,
        grid=(t // tm,),
        in_specs=[rows(x2), rows(og), rows(orw), rows(p2), whole(wo), whole(lnm), whole(w1),
                  whole(w2), whole(lnp), whole(wg), whole(wp), whole(lnf)],
        out_specs=pl.BlockSpec((tm, d), lambda i: (i, 0)),
        out_shape=jax.ShapeDtypeStruct((t, d), F32),
        compiler_params=pltpu.CompilerParams(
            dimension_semantics=("arbitrary",), vmem_limit_bytes=VMEM_LIMIT_BYTES),
        name="tail",
    )(x2, og, orw, p2, wo, lnm, w1, w2, lnp, wg, wp, lnf)


def _pad_cols(w, n):
    return jnp.pad(w, ((0, 0), (0, n - w.shape[1])))


def _place_rows(w, start, total):
    return jnp.pad(w, ((start, total - start - w.shape[0]), (0, 0)))


def _layer(h, p_i, ln_mix, w_in, conv_w, gk_w, gk_b, norm_g, mu, w0, w2, a0, a2, g2, k_k, k_a,
           r_k, gn_w, gn_b, w_out, ln_mlp, w_ff1, w_ff2, ln_ple, w_gate, w_proj, ln_next, tm):
    bsz, s, d = h.shape
    t = bsz * s
    row = lambda a: a.reshape(1, -1)

    gate_rank, kw = gk_w.shape
    dv = norm_g.shape[0]
    vw = (conv_w.shape[1] - 2 * kw)
    heads = vw // dv
    dk = kw // heads
    gla_cols = 2 * kw + 2 * vw + gate_rank
    gla_pad = _round_up(gla_cols, LANES)
    width = w0.shape[0]
    rwkv_cols = w_in.shape[1] - gla_cols
    rwkv_pad = _round_up(rwkv_cols, LANES)
    low_pad = rwkv_pad - 3 * width
    dr, ar = w2.shape[0], a2.shape[0]

    wg_in = _pad_cols(w_in[:, :gla_cols], gla_pad).astype(BF16)
    wr_in = _pad_cols(w_in[:, gla_cols:], rwkv_pad).astype(BF16)
    zg, zr = _in_proj(h.reshape(t, d), row(ln_mix), wg_in, wr_in, tm)

    tri = jnp.tril(jnp.ones((CHUNK, CHUNK), F32)).astype(BF16)
    hid = jnp.arange(width) // RWKV_HEAD
    bd = (hid[:, None] == hid[None, :]).astype(BF16)

    gkw_p = _place_rows(gk_w, 0, gla_pad - (gla_cols - gate_rank))
    o_gla = _gla_mixer(zg.reshape(bsz, s, gla_pad), conv_w, gkw_p, row(gk_b),
                       row(norm_g), tri, heads=heads, dk=dk, dv=dv)

    rwkv_params = (
        _pad_cols(row(mu), rwkv_pad), row(w0), _place_rows(w2, 0, low_pad), row(a0),
        _place_rows(a2, dr, low_pad), _place_rows(g2, dr + ar, low_pad), row(k_k), row(k_a),
        row(r_k), row(gn_w), row(gn_b))
    o_rwkv = _rwkv_mixer(zr.reshape(bsz, s, rwkv_pad), rwkv_params, tri, bd, width=width)

    out = _tail(h.reshape(t, d), o_gla.reshape(t, vw), o_rwkv.reshape(t, width),
                p_i.reshape(t, -1), w_out.astype(BF16), row(ln_mlp), w_ff1.astype(BF16),
                w_ff2.astype(BF16), row(ln_ple), w_gate.astype(BF16), w_proj.astype(BF16),
                row(ln_next), tm)
    return out.reshape(bsz, s, d)


def kernel(x, p, ln_mix, w_in, gla_conv_w, gla_gk_w, gla_gk_b, gla_norm_g, rwkv_mu, rwkv_w0,
           rwkv_w2, rwkv_a0, rwkv_a2, rwkv_g2, rwkv_k_k, rwkv_k_a, rwkv_r_k, rwkv_gn_w,
           rwkv_gn_b, w_out, ln_mlp, w_ff1, w_ff2, ln_ple, w_ple_gate, w_ple_proj, ln_final):
    depth = w_in.shape[0]
    assert depth == 1, "the fused tail applies the final norm, so exactly one layer is supported"
    tm = 512
    assert (x.shape[0] * x.shape[1]) % tm == 0 and x.shape[1] % CHUNK == 0
    i = 0
    return _layer(x, p[i], ln_mix[i], w_in[i], gla_conv_w[i], gla_gk_w[i], gla_gk_b[i],
                  gla_norm_g[i], rwkv_mu[i], rwkv_w0[i], rwkv_w2[i], rwkv_a0[i], rwkv_a2[i],
                  rwkv_g2[i], rwkv_k_k[i], rwkv_k_a[i], rwkv_r_k[i].reshape(-1), rwkv_gn_w[i],
                  rwkv_gn_b[i], w_out[i], ln_mlp[i], w_ff1[i], w_ff2[i], ln_ple[i],
                  w_ple_gate[i], w_ple_proj[i], ln_final, tm)
```

```python
import functools

import jax
import jax.numpy as jnp
from jax import lax
from jax.experimental import pallas as pl
from jax.experimental.pallas import tpu as pltpu

F32 = jnp.float32
BF16 = jnp.bfloat16

NORM_EPS = 1e-6
GLA_GATE_NORMALIZER = 16.0
RWKV_GN_EPS = 64e-5
RWKV_HEAD = 64
CHUNK = 64
LANES = 128
HALO = 8
RWKV_TILE = 256
VMEM_LIMIT_BYTES = 56 * 1024 * 1024


def _round_up(n, m):
    return (n + m - 1) // m * m


def _sigmoid(x):
    return 1.0 / (1.0 + jnp.exp(-x))


def _softplus(x):
    return jnp.maximum(x, 0.0) + jnp.log(1.0 + jnp.exp(-jnp.abs(x)))


def _dot(a, b):
    return jnp.dot(a, b, preferred_element_type=F32)


def _dot_nt(a, b):
    return lax.dot_general(a, b, (((1,), (1,)), ((), ())), preferred_element_type=F32)


def _dot_tn(a, b):
    return lax.dot_general(a, b, (((0,), (0,)), ((), ())), preferred_element_type=F32)


def _split2(x):
    hi = x.astype(BF16)
    lo = (x - hi.astype(F32)).astype(BF16)
    return hi, lo


def _split3(x):
    hi = x.astype(BF16)
    r1 = x - hi.astype(F32)
    mid = r1.astype(BF16)
    lo = (r1 - mid.astype(F32)).astype(BF16)
    return hi, mid, lo


def _dot_exact_lhs(a_bf16, x):
    hi, mid, lo = _split3(x)
    return _dot(a_bf16, hi) + _dot(a_bf16, mid) + _dot(a_bf16, lo)


def _dot_f32(a, b):
    a_hi, a_lo = _split2(a)
    b_hi, b_lo = _split2(b)
    return _dot(a_hi, b_hi) + _dot(a_hi, b_lo) + _dot(a_lo, b_hi)


def _rmsnorm(x, g):
    ms = jnp.mean(x * x, axis=-1, keepdims=True)
    return x * lax.rsqrt(ms + NORM_EPS) * g


def _tri_masks(n):
    row = lax.broadcasted_iota(jnp.int32, (n, n), 0)
    col = lax.broadcasted_iota(jnp.int32, (n, n), 1)
    return col <= row, col < row


def _in_proj_kernel(x_ref, g_ref, wg_ref, wr_ref, zg_ref, zr_ref):
    xn = _rmsnorm(x_ref[...], g_ref[...]).astype(BF16)
    zg_ref[...] = _dot(xn, wg_ref[...])
    zr_ref[...] = _dot(xn, wr_ref[...])


def _in_proj(x2, ln, wg, wr, tm):
    t, d = x2.shape
    ng, nr = wg.shape[1], wr.shape[1]
    const = lambda i: (0, 0)
    return pl.pallas_call(
        _in_proj_kernel,
        grid=(t // tm,),
        in_specs=[
            pl.BlockSpec((tm, d), lambda i: (i, 0)),
            pl.BlockSpec((1, d), const),
            pl.BlockSpec((d, ng), const, pipeline_mode=pl.Buffered(1)),
            pl.BlockSpec((d, nr), const, pipeline_mode=pl.Buffered(1)),
        ],
        out_specs=[
            pl.BlockSpec((tm, ng), lambda i: (i, 0)),
            pl.BlockSpec((tm, nr), lambda i: (i, 0)),
        ],
        out_shape=[
            jax.ShapeDtypeStruct((t, ng), F32),
            jax.ShapeDtypeStruct((t, nr), F32),
        ],
        compiler_params=pltpu.CompilerParams(
            dimension_semantics=("arbitrary",), vmem_limit_bytes=VMEM_LIMIT_BYTES),
        name="in_proj",
    )(x2, ln, wg, wr)


def _gla_kernel(z_ref, cw_ref, gkw_ref, gkb_ref, ng_ref, tri_ref, o_ref,
                xb_ref, st_ref, *, heads, dk, dv):
    kw = heads * dk
    vw = heads * dv
    nqkv = 2 * kw + vw
    taps = cw_ref.shape[0]
    pairs = kw // LANES

    @pl.when(pl.program_id(1) == 0)
    def _():
        xb_ref[0:HALO, :] = jnp.zeros((HALO, nqkv), F32)
        st_ref[...] = jnp.zeros_like(st_ref)

    z = z_ref[0]
    xb_ref[HALO:HALO + CHUNK, :] = z[:, :nqkv]
    conv = jnp.zeros((CHUNK, nqkv), F32)
    for t in range(taps):
        off = HALO - (taps - 1) + t
        conv = conv + cw_ref[t:t + 1, :] * xb_ref[off:off + CHUNK, :]
    xb_ref[0:HALO, :] = xb_ref[CHUNK:CHUNK + HALO, :]
    qkv = conv * _sigmoid(conv)
    q = qkv[:, :kw] * (dk ** -0.5)
    k = qkv[:, kw:2 * kw]
    v = qkv[:, 2 * kw:nqkv]
    g = z[:, nqkv:nqkv + vw]
    low = z[:, nqkv + vw:]

    gk_pre = _dot(low, gkw_ref[...]) + gkb_ref[...]
    gk = -_softplus(-gk_pre) * (1.0 / GLA_GATE_NORMALIZER)
    b = _dot_exact_lhs(tri_ref[...], gk)
    b_last = b[CHUNK - 1:CHUNK, :]
    b_ref = 0.5 * b_last
    qe = q * jnp.exp(b - b_ref)
    ke = k * jnp.exp(b_ref - b)
    qb = q * jnp.exp(b)
    kd = k * jnp.exp(b_last - b)
    e_last = jnp.exp(b_last)

    causal, _ = _tri_masks(CHUNK)
    lane = lax.broadcasted_iota(jnp.int32, (1, LANES), 1)
    head_mask = [(lane < dk).astype(F32), (lane >= dk).astype(F32)]
    srow = lax.broadcasted_iota(jnp.int32, (2 * dv, LANES), 0)
    scol = lax.broadcasted_iota(jnp.int32, (2 * dv, LANES), 1)
    st_mask = ((srow < dv) == (scol < dk)).astype(F32)

    o_parts = []
    for p in range(pairs):
        ls = slice(p * LANES, (p + 1) * LANES)
        vs = slice(2 * p * dv, 2 * (p + 1) * dv)
        st = st_ref[p]
        o_inter = _dot_nt(qb[:, ls], st)
        for e in range(2):
            h = 2 * p + e
            s = _dot_nt(qe[:, ls] * head_mask[e], ke[:, ls])
            s = jnp.where(causal, s, 0.0)
            o_h = _dot(s, v[:, h * dv:(h + 1) * dv]) + o_inter[:, e * dv:(e + 1) * dv]
            ms = jnp.mean(o_h * o_h, axis=-1, keepdims=True)
            o_parts.append(o_h * lax.rsqrt(ms + NORM_EPS) * ng_ref[...])
        st_ref[p] = st * e_last[:, ls] + st_mask * _dot_tn(v[:, vs], kd[:, ls])
    o = jnp.concatenate(o_parts, axis=1)
    o_ref[0] = (o * (g * _sigmoid(g))).astype(o_ref.dtype)


def _gla_mixer(zg, conv_w, gkw_p, gk_b, norm_g, tri, *, heads, dk, dv):
    bsz, s, cols = zg.shape
    kw, vw = heads * dk, heads * dv
    nqkv = 2 * kw + vw
    const2 = lambda b, j: (0, 0)
    kern = functools.partial(_gla_kernel, heads=heads, dk=dk, dv=dv)
    return pl.pallas_call(
        kern,
        grid=(bsz, s // CHUNK),
        in_specs=[
            pl.BlockSpec((1, CHUNK, cols), lambda b, j: (b, j, 0)),
            pl.BlockSpec(conv_w.shape, const2),
            pl.BlockSpec(gkw_p.shape, const2),
            pl.BlockSpec(gk_b.shape, const2),
            pl.BlockSpec(norm_g.shape, const2),
            pl.BlockSpec(tri.shape, const2),
        ],
        out_specs=pl.BlockSpec((1, CHUNK, vw), lambda b, j: (b, j, 0)),
        out_shape=jax.ShapeDtypeStruct((bsz, s, vw), BF16),
        scratch_shapes=[
            pltpu.VMEM((CHUNK + HALO, nqkv), F32),
            pltpu.VMEM((kw // LANES, 2 * dv, LANES), F32),
        ],
        compiler_params=pltpu.CompilerParams(
            dimension_semantics=("arbitrary", "arbitrary"), vmem_limit_bytes=VMEM_LIMIT_BYTES),
        name="gla_mixer",
    )(zg, conv_w, gkw_p, gk_b, norm_g, tri)


def _headsum(x, bd_pair):
    rows, width = x.shape
    groups = width // LANES
    hi, lo = _split2(x)
    stacked = jnp.concatenate(
        [t[:, i * LANES:(i + 1) * LANES] for t in (hi, lo) for i in range(groups)], axis=0)
    s = _dot(stacked, bd_pair)
    return jnp.concatenate(
        [s[i * rows:(i + 1) * rows] + s[(groups + i) * rows:(groups + i + 1) * rows]
         for i in range(groups)], axis=1)


def _rwkv_kernel(z_ref, mu_ref, w0_ref, w2_ref, a0_ref, a2_ref, g2_ref, kk_ref, ka_ref,
                 rk_ref, gnw_ref, gnb_ref, tri_ref, bdp_ref, o_ref,
                 xb_ref, st_ref, at_s, rt_s, bt_s, kt_s, bh_s, kh_s, v_s, rtf_s, gam_s, y_s,
                 *, width):
    n = RWKV_HEAD
    pairs = width // LANES
    tl = z_ref.shape[1]
    nchunks = tl // CHUNK
    levels = (CHUNK - 1).bit_length()

    @pl.when(pl.program_id(1) == 0)
    def _():
        xb_ref[0:HALO, :] = jnp.zeros((HALO, xb_ref.shape[1]), F32)
        st_ref[...] = jnp.zeros_like(st_ref)

    z = z_ref[0]
    xb_ref[HALO:HALO + tl, :] = z
    zp = xb_ref[HALO - 1:HALO - 1 + tl, :]
    xb_ref[0:HALO, :] = xb_ref[tl:tl + HALO, :]
    zs = z + mu_ref[...] * (zp - z)
    r = zs[:, 0:width]
    k = zs[:, width:2 * width]
    v = zs[:, 2 * width:3 * width]
    low = zs[:, 3 * width:]

    w_pre = w0_ref[...] + _dot(jnp.tanh(low), w2_ref[...])
    logw = -jnp.exp(-_softplus(-w_pre) - 0.5)
    a_sig = _sigmoid(a0_ref[...] + _dot(low, a2_ref[...]))
    g = _dot(_sigmoid(low), g2_ref[...])

    bdp = bdp_ref[...]
    kk = k * kk_ref[...]
    kk = kk / jnp.maximum(jnp.sqrt(_headsum(kk * kk, bdp)), 1e-12)
    kf = k * (1.0 + (a_sig - 1.0) * ka_ref[...])
    bv = kk * a_sig
    bonus = _headsum(r * kf * rk_ref[...], bdp)

    lg = _dot_exact_lhs(tri_ref[...], logw)
    last_rows = [lg[c * CHUNK + CHUNK - 1:(c + 1) * CHUNK, :] for c in range(nchunks)]
    lg_last = jnp.concatenate([jnp.broadcast_to(row, (CHUNK, width)) for row in last_rows], axis=0)
    for c, row in enumerate(last_rows):
        gam_s[c] = jnp.broadcast_to(jnp.exp(row), (HALO, width))
    e_inv = jnp.exp(-lg)
    e_rem = jnp.exp(lg_last - lg)
    rt = r * jnp.exp(lg)
    at_s[...] = (-kk * jnp.exp(lg - logw)).astype(BF16)
    rt_s[...] = rt.astype(BF16)
    rtf_s[...] = rt
    bt_s[...] = (bv * e_inv).astype(BF16)
    kt_s[...] = (kf * e_inv).astype(BF16)
    bh_s[...] = (bv * e_rem).astype(BF16)
    kh_s[...] = (kf * e_rem).astype(BF16)
    v_s[...] = v.astype(BF16)

    incl, strict = _tri_masks(CHUNK)
    lane = lax.broadcasted_iota(jnp.int32, (CHUNK, LANES), 1)
    head_mask = [(lane < n).astype(BF16), (lane >= n).astype(BF16)]
    prow = lax.broadcasted_iota(jnp.int32, (LANES, LANES), 0)
    pcol = lax.broadcasted_iota(jnp.int32, (LANES, LANES), 1)
    blk = ((prow < n) == (pcol < n)).astype(F32)
    eye = (prow == pcol).astype(F32)
    hl = [(p, e) for p in range(pairs) for e in range(2)]
    lsl = [slice(p * LANES, (p + 1) * LANES) for p in range(pairs)]
    nh = len(hl)

    def chunk_body(c, carry):
        rows = pl.ds(pl.multiple_of(c * CHUNK, CHUNK), CHUNK)
        gam = gam_s[c][0:1, :]
        bt_b = [bt_s[rows, ls] for ls in lsl]
        kt_b = [kt_s[rows, ls] for ls in lsl]
        v_b = [v_s[rows, ls] for ls in lsl]
        at_m = [at_s[rows, lsl[p]] * head_mask[e] for p, e in hl]
        rt_m = [rt_s[rows, lsl[p]] * head_mask[e] for p, e in hl]
        v_m = [v_b[p] * head_mask[e] for p, e in hl]
        lhs = [jnp.concatenate([a_, r_], axis=0) for a_, r_ in zip(at_m, rt_m)]
        sb = [_dot_nt(lhs[i], bt_b[p]) for i, (p, e) in enumerate(hl)]
        sk = [_dot_nt(lhs[i], kt_b[p]) for i, (p, e) in enumerate(hl)]
        a_ab = [jnp.where(strict, s_[:CHUNK], 0.0).astype(BF16) for s_ in sb]
        a_rb = [jnp.where(incl, s_[CHUNK:], 0.0).astype(BF16) for s_ in sb]
        a_ak = [jnp.where(strict, s_[:CHUNK], 0.0).astype(BF16) for s_ in sk]
        a_rk = [jnp.where(incl, s_[CHUNK:], 0.0).astype(BF16) for s_ in sk]
        x = [jnp.concatenate([at_m[i].astype(F32), _dot(a_ak[i], v_m[i])], axis=1)
             for i in range(nh)]
        pw = a_ab
        for lvl in range(levels):
            x = [x_ + _dot(p_, x_.astype(BF16)) for x_, p_ in zip(x, pw)]
            if lvl + 1 < levels:
                pw = [_dot(p_, p_).astype(BF16) for p_ in pw]
        zz = [_dot(a_, x_.astype(BF16)) for a_, x_ in zip(a_rb, x)]
        rk_v = [_dot(a_, v_) for a_, v_ in zip(a_rk, v_m)]
        for p in range(pairs):
            ls = lsl[p]
            i0, i1 = 2 * p, 2 * p + 1
            xp = (x[i0] + x[i1]).astype(BF16)
            rhat = rtf_s[rows, ls] + zz[i0][:, :LANES] + zz[i1][:, :LANES]
            yzero = zz[i0][:, LANES:] + zz[i1][:, LANES:] + rk_v[i0] + rk_v[i1]
            lb = jnp.concatenate([bh_s[rows, ls], kh_s[rows, ls]], axis=0)
            rb = jnp.concatenate(
                [xp, jnp.concatenate([jnp.zeros_like(v_b[p]), v_b[p]], axis=1)], axis=0)
            mn = _dot_tn(lb, rb)
            m_mat = blk * mn[:, :LANES] + eye * gam[:, ls]
            n_mat = blk * mn[:, LANES:]
            t_old = st_ref[p]
            y_s[rows, ls] = _dot_f32(rhat, t_old) + yzero
            st_ref[p] = _dot_f32(m_mat, t_old) + n_mat
        return carry

    lax.fori_loop(0, nchunks, chunk_body, 0)

    y = y_s[...]
    mean = _headsum(y, bdp) * (1.0 / n)
    d = y - mean
    var = _headsum(d * d, bdp) * (1.0 / n)
    yn = d * lax.rsqrt(var + RWKV_GN_EPS) * gnw_ref[...] + gnb_ref[...]
    o_ref[0] = ((yn + bonus * v) * g).astype(o_ref.dtype)


def _rwkv_mixer(zr, params, *, width):
    bsz, s, cols = zr.shape
    tl = RWKV_TILE
    nchunks = tl // CHUNK
    idx = jnp.arange(tl)
    tri = ((idx[:, None] >= idx[None, :]) & (idx[:, None] // CHUNK == idx[None, :] // CHUNK)).astype(BF16)
    hid = jnp.arange(LANES) // RWKV_HEAD
    bdp = (hid[:, None] == hid[None, :]).astype(BF16)
    const2 = lambda b, j: (0, 0)
    kern = functools.partial(_rwkv_kernel, width=width)
    return pl.pallas_call(
        kern,
        grid=(bsz, s // tl),
        in_specs=[pl.BlockSpec((1, tl, cols), lambda b, j: (b, j, 0))]
        + [pl.BlockSpec(a.shape, const2) for a in params]
        + [pl.BlockSpec(tri.shape, const2), pl.BlockSpec(bdp.shape, const2)],
        out_specs=pl.BlockSpec((1, tl, width), lambda b, j: (b, j, 0)),
        out_shape=jax.ShapeDtypeStruct((bsz, s, width), BF16),
        scratch_shapes=[
            pltpu.VMEM((tl + HALO, cols), F32),
            pltpu.VMEM((width // LANES, LANES, LANES), F32),
        ] + [pltpu.VMEM((tl, width), BF16)] * 7 + [
            pltpu.VMEM((tl, width), F32),
            pltpu.VMEM((nchunks, HALO, width), F32),
            pltpu.VMEM((tl, width), F32),
        ],
        compiler_params=pltpu.CompilerParams(
            dimension_semantics=("arbitrary", "arbitrary"), vmem_limit_bytes=VMEM_LIMIT_BYTES),
        name="rwkv_mixer",
    )(zr, *params, tri, bdp)


def _tail_kernel(x_ref, og_ref, or_ref, p_ref, wo_ref, lnm_ref, w1_ref, w2_ref,
                 lnp_ref, wg_ref, wp_ref, lnf_ref, out_ref):
    gw = og_ref.shape[1]
    h = x_ref[...] + _dot(og_ref[...], wo_ref[0:gw, :]) + _dot(or_ref[...], wo_ref[gw:, :])
    n1 = _rmsnorm(h, lnm_ref[...]).astype(BF16)
    hid = jnp.maximum(_dot(n1, w1_ref[...]), 0.0)
    h = h + _dot((hid * hid).astype(BF16), w2_ref[...])
    n2 = _rmsnorm(h, lnp_ref[...]).astype(BF16)
    gate = _sigmoid(_dot(n2, wg_ref[...]))
    h = h + gate * _dot(p_ref[...].astype(BF16), wp_ref[...])
    out_ref[...] = _rmsnorm(h, lnf_ref[...])


def _tail(x2, og, orw, p2, wo, lnm, w1, w2, lnp, wg, wp, lnf, tm):
    t, d = x2.shape
    const = lambda i: (0, 0)
    rows = lambda a: pl.BlockSpec((tm, a.shape[1]), lambda i: (i, 0))
    whole = lambda a: pl.BlockSpec(a.shape, const, pipeline_mode=pl.Buffered(1))
    return pl.pallas_call(
        _tail_kernel,
        grid=(t // tm,),
        in_specs=[rows(x2), rows(og), rows(orw), rows(p2), whole(wo), whole(lnm), whole(w1),
                  whole(w2), whole(lnp), whole(wg), whole(wp), whole(lnf)],
        out_specs=pl.BlockSpec((tm, d), lambda i: (i, 0)),
        out_shape=jax.ShapeDtypeStruct((t, d), F32),
        compiler_params=pltpu.CompilerParams(
            dimension_semantics=("arbitrary",), vmem_limit_bytes=VMEM_LIMIT_BYTES),
        name="tail",
    )(x2, og, orw, p2, wo, lnm, w1, w2, lnp, wg, wp, lnf)


def _pad_cols(w, n):
    return jnp.pad(w, ((0, 0), (0, n - w.shape[1])))


def _place_rows(w, start, total):
    return jnp.pad(w, ((start, total - start - w.shape[0]), (0, 0)))


def _layer(h, p_i, ln_mix, w_in, conv_w, gk_w, gk_b, norm_g, mu, w0, w2, a0, a2, g2, k_k, k_a,
           r_k, gn_w, gn_b, w_out, ln_mlp, w_ff1, w_ff2, ln_ple, w_gate, w_proj, ln_next, tm):
    bsz, s, d = h.shape
    t = bsz * s
    row = lambda a: a.reshape(1, -1)

    gate_rank, kw = gk_w.shape
    dv = norm_g.shape[0]
    vw = (conv_w.shape[1] - 2 * kw)
    heads = vw // dv
    dk = kw // heads
    gla_cols = 2 * kw + 2 * vw + gate_rank
    gla_pad = _round_up(gla_cols, LANES)
    width = w0.shape[0]
    rwkv_cols = w_in.shape[1] - gla_cols
    rwkv_pad = _round_up(rwkv_cols, LANES)
    low_pad = rwkv_pad - 3 * width
    dr, ar = w2.shape[0], a2.shape[0]

    wg_in = _pad_cols(w_in[:, :gla_cols], gla_pad).astype(BF16)
    wr_in = _pad_cols(w_in[:, gla_cols:], rwkv_pad).astype(BF16)
    zg, zr = _in_proj(h.reshape(t, d), row(ln_mix), wg_in, wr_in, tm)

    tri = jnp.tril(jnp.ones((CHUNK, CHUNK), F32)).astype(BF16)

    gkw_p = _place_rows(gk_w, 0, gla_pad - (gla_cols - gate_rank))
    o_gla = _gla_mixer(zg.reshape(bsz, s, gla_pad), conv_w, gkw_p, row(gk_b),
                       row(norm_g), tri, heads=heads, dk=dk, dv=dv)

    rwkv_params = (
        _pad_cols(row(mu), rwkv_pad), row(w0), _place_rows(w2, 0, low_pad), row(a0),
        _place_rows(a2, dr, low_pad), _place_rows(g2, dr + ar, low_pad), row(k_k), row(k_a),
        row(r_k), row(gn_w), row(gn_b))
    o_rwkv = _rwkv_mixer(zr.reshape(bsz, s, rwkv_pad), rwkv_params, width=width)

    out = _tail(h.reshape(t, d), o_gla.reshape(t, vw), o_rwkv.reshape(t, width),
                p_i.reshape(t, -1), w_out.astype(BF16), row(ln_mlp), w_ff1.astype(BF16),
                w_ff2.astype(BF16), row(ln_ple), w_gate.astype(BF16), w_proj.astype(BF16),
                row(ln_next), tm)
    return out.reshape(bsz, s, d)


def kernel(x, p, ln_mix, w_in, gla_conv_w, gla_gk_w, gla_gk_b, gla_norm_g, rwkv_mu, rwkv_w0,
           rwkv_w2, rwkv_a0, rwkv_a2, rwkv_g2, rwkv_k_k, rwkv_k_a, rwkv_r_k, rwkv_gn_w,
           rwkv_gn_b, w_out, ln_mlp, w_ff1, w_ff2, ln_ple, w_ple_gate, w_ple_proj, ln_final):
    depth = w_in.shape[0]
    assert depth == 1, "the fused tail applies the final norm, so exactly one layer is supported"
    tm = 512
    assert (x.shape[0] * x.shape[1]) % tm == 0 and x.shape[1] % RWKV_TILE == 0
    i = 0
    return _layer(x, p[i], ln_mix[i], w_in[i], gla_conv_w[i], gla_gk_w[i], gla_gk_b[i],
                  gla_norm_g[i], rwkv_mu[i], rwkv_w0[i], rwkv_w2[i], rwkv_a0[i], rwkv_a2[i],
                  rwkv_g2[i], rwkv_k_k[i], rwkv_k_a[i], rwkv_r_k[i].reshape(-1), rwkv_gn_w[i],
                  rwkv_gn_b[i], w_out[i], ln_mlp[i], w_ff1[i], w_ff2[i], ln_ple[i],
                  w_ple_gate[i], w_ple_proj[i], ln_final, tm)
```

```python
import functools

import jax
import jax.numpy as jnp
from jax import lax
from jax.experimental import pallas as pl
from jax.experimental.pallas import tpu as pltpu

F32 = jnp.float32
BF16 = jnp.bfloat16

NORM_EPS = 1e-6
GLA_GATE_NORMALIZER = 16.0
RWKV_GN_EPS = 64e-5
RWKV_HEAD = 64
CHUNK = 64
LANES = 128
HALO = 8
RWKV_TILE = 256
RWKV_UNROLL = 2
VMEM_LIMIT_BYTES = 56 * 1024 * 1024


def _round_up(n, m):
    return (n + m - 1) // m * m


def _sigmoid(x):
    return 1.0 / (1.0 + jnp.exp(-x))


def _softplus(x):
    return jnp.maximum(x, 0.0) + jnp.log(1.0 + jnp.exp(-jnp.abs(x)))


def _dot(a, b):
    return jnp.dot(a, b, preferred_element_type=F32)


def _dot_nt(a, b):
    return lax.dot_general(a, b, (((1,), (1,)), ((), ())), preferred_element_type=F32)


def _dot_tn(a, b):
    return lax.dot_general(a, b, (((0,), (0,)), ((), ())), preferred_element_type=F32)


def _split2(x):
    hi = x.astype(BF16)
    lo = (x - hi.astype(F32)).astype(BF16)
    return hi, lo


def _split3(x):
    hi = x.astype(BF16)
    r1 = x - hi.astype(F32)
    mid = r1.astype(BF16)
    lo = (r1 - mid.astype(F32)).astype(BF16)
    return hi, mid, lo


def _dot_exact_lhs(a_bf16, x):
    hi, mid, lo = _split3(x)
    return _dot(a_bf16, hi) + _dot(a_bf16, mid) + _dot(a_bf16, lo)


def _dot_f32(a, b):
    a_hi, a_lo = _split2(a)
    b_hi, b_lo = _split2(b)
    return _dot(a_hi, b_hi) + _dot(a_hi, b_lo) + _dot(a_lo, b_hi)


def _rmsnorm(x, g):
    ms = jnp.mean(x * x, axis=-1, keepdims=True)
    return x * lax.rsqrt(ms + NORM_EPS) * g


def _tri_masks(n):
    row = lax.broadcasted_iota(jnp.int32, (n, n), 0)
    col = lax.broadcasted_iota(jnp.int32, (n, n), 1)
    return col <= row, col < row


def _in_proj_kernel(x_ref, g_ref, wg_ref, wr_ref, zg_ref, zr_ref):
    xn = _rmsnorm(x_ref[...], g_ref[...]).astype(BF16)
    zg_ref[...] = _dot(xn, wg_ref[...])
    zr_ref[...] = _dot(xn, wr_ref[...])


def _in_proj(x2, ln, wg, wr, tm):
    t, d = x2.shape
    ng, nr = wg.shape[1], wr.shape[1]
    const = lambda i: (0, 0)
    return pl.pallas_call(
        _in_proj_kernel,
        grid=(t // tm,),
        in_specs=[
            pl.BlockSpec((tm, d), lambda i: (i, 0)),
            pl.BlockSpec((1, d), const),
            pl.BlockSpec((d, ng), const, pipeline_mode=pl.Buffered(1)),
            pl.BlockSpec((d, nr), const, pipeline_mode=pl.Buffered(1)),
        ],
        out_specs=[
            pl.BlockSpec((tm, ng), lambda i: (i, 0)),
            pl.BlockSpec((tm, nr), lambda i: (i, 0)),
        ],
        out_shape=[
            jax.ShapeDtypeStruct((t, ng), F32),
            jax.ShapeDtypeStruct((t, nr), F32),
        ],
        compiler_params=pltpu.CompilerParams(
            dimension_semantics=("arbitrary",), vmem_limit_bytes=VMEM_LIMIT_BYTES),
        name="in_proj",
    )(x2, ln, wg, wr)


def _gla_kernel(z_ref, cw_ref, gkw_ref, gkb_ref, ng_ref, tri_ref, o_ref,
                xb_ref, st_ref, *, heads, dk, dv):
    kw = heads * dk
    vw = heads * dv
    nqkv = 2 * kw + vw
    taps = cw_ref.shape[0]
    pairs = kw // LANES

    @pl.when(pl.program_id(1) == 0)
    def _():
        xb_ref[0:HALO, :] = jnp.zeros((HALO, nqkv), F32)
        st_ref[...] = jnp.zeros_like(st_ref)

    z = z_ref[0]
    xb_ref[HALO:HALO + CHUNK, :] = z[:, :nqkv]
    conv = jnp.zeros((CHUNK, nqkv), F32)
    for t in range(taps):
        off = HALO - (taps - 1) + t
        conv = conv + cw_ref[t:t + 1, :] * xb_ref[off:off + CHUNK, :]
    xb_ref[0:HALO, :] = xb_ref[CHUNK:CHUNK + HALO, :]
    qkv = conv * _sigmoid(conv)
    q = qkv[:, :kw] * (dk ** -0.5)
    k = qkv[:, kw:2 * kw]
    v = qkv[:, 2 * kw:nqkv]
    g = z[:, nqkv:nqkv + vw]
    low = z[:, nqkv + vw:]

    gk_pre = _dot(low, gkw_ref[...]) + gkb_ref[...]
    gk = -_softplus(-gk_pre) * (1.0 / GLA_GATE_NORMALIZER)
    b = _dot_exact_lhs(tri_ref[...], gk)
    b_last = b[CHUNK - 1:CHUNK, :]
    b_ref = 0.5 * b_last
    qe = q * jnp.exp(b - b_ref)
    ke = k * jnp.exp(b_ref - b)
    qb = q * jnp.exp(b)
    kd = k * jnp.exp(b_last - b)
    e_last = jnp.exp(b_last)

    causal, _ = _tri_masks(CHUNK)
    lane = lax.broadcasted_iota(jnp.int32, (1, LANES), 1)
    head_mask = [(lane < dk).astype(F32), (lane >= dk).astype(F32)]
    srow = lax.broadcasted_iota(jnp.int32, (2 * dv, LANES), 0)
    scol = lax.broadcasted_iota(jnp.int32, (2 * dv, LANES), 1)
    st_mask = ((srow < dv) == (scol < dk)).astype(F32)

    o_parts = []
    for p in range(pairs):
        ls = slice(p * LANES, (p + 1) * LANES)
        vs = slice(2 * p * dv, 2 * (p + 1) * dv)
        st = st_ref[p]
        o_inter = _dot_nt(qb[:, ls], st)
        for e in range(2):
            h = 2 * p + e
            s = _dot_nt(qe[:, ls] * head_mask[e], ke[:, ls])
            s = jnp.where(causal, s, 0.0)
            o_h = _dot(s, v[:, h * dv:(h + 1) * dv]) + o_inter[:, e * dv:(e + 1) * dv]
            ms = jnp.mean(o_h * o_h, axis=-1, keepdims=True)
            o_parts.append(o_h * lax.rsqrt(ms + NORM_EPS) * ng_ref[...])
        st_ref[p] = st * e_last[:, ls] + st_mask * _dot_tn(v[:, vs], kd[:, ls])
    o = jnp.concatenate(o_parts, axis=1)
    o_ref[0] = (o * (g * _sigmoid(g))).astype(o_ref.dtype)


def _gla_mixer(zg, conv_w, gkw_p, gk_b, norm_g, tri, *, heads, dk, dv):
    bsz, s, cols = zg.shape
    kw, vw = heads * dk, heads * dv
    nqkv = 2 * kw + vw
    const2 = lambda b, j: (0, 0)
    kern = functools.partial(_gla_kernel, heads=heads, dk=dk, dv=dv)
    return pl.pallas_call(
        kern,
        grid=(bsz, s // CHUNK),
        in_specs=[
            pl.BlockSpec((1, CHUNK, cols), lambda b, j: (b, j, 0)),
            pl.BlockSpec(conv_w.shape, const2),
            pl.BlockSpec(gkw_p.shape, const2),
            pl.BlockSpec(gk_b.shape, const2),
            pl.BlockSpec(norm_g.shape, const2),
            pl.BlockSpec(tri.shape, const2),
        ],
        out_specs=pl.BlockSpec((1, CHUNK, vw), lambda b, j: (b, j, 0)),
        out_shape=jax.ShapeDtypeStruct((bsz, s, vw), BF16),
        scratch_shapes=[
            pltpu.VMEM((CHUNK + HALO, nqkv), F32),
            pltpu.VMEM((kw // LANES, 2 * dv, LANES), F32),
        ],
        compiler_params=pltpu.CompilerParams(
            dimension_semantics=("arbitrary", "arbitrary"), vmem_limit_bytes=VMEM_LIMIT_BYTES),
        name="gla_mixer",
    )(zg, conv_w, gkw_p, gk_b, norm_g, tri)


def _headsum(x, bd_pair):
    rows, width = x.shape
    groups = width // LANES
    hi, lo = _split2(x)
    stacked = jnp.concatenate(
        [t[:, i * LANES:(i + 1) * LANES] for t in (hi, lo) for i in range(groups)], axis=0)
    s = _dot(stacked, bd_pair)
    return jnp.concatenate(
        [s[i * rows:(i + 1) * rows] + s[(groups + i) * rows:(groups + i + 1) * rows]
         for i in range(groups)], axis=1)


def _rwkv_kernel(z_ref, mu_ref, w0_ref, w2_ref, a0_ref, a2_ref, g2_ref, kk_ref, ka_ref,
                 rk_ref, gnw_ref, gnb_ref, tri_ref, bdp_ref, o_ref,
                 xb_ref, st_ref, at_s, rt_s, bt_s, kt_s, bh_s, kh_s, v_s, rtf_s, gam_s, y_s,
                 *, width):
    n = RWKV_HEAD
    pairs = width // LANES
    tl = z_ref.shape[1]
    nchunks = tl // CHUNK
    levels = (CHUNK - 1).bit_length()

    @pl.when(pl.program_id(1) == 0)
    def _():
        xb_ref[0:HALO, :] = jnp.zeros((HALO, xb_ref.shape[1]), F32)
        st_ref[...] = jnp.zeros_like(st_ref)

    z = z_ref[0]
    xb_ref[HALO:HALO + tl, :] = z
    zp = xb_ref[HALO - 1:HALO - 1 + tl, :]
    xb_ref[0:HALO, :] = xb_ref[tl:tl + HALO, :]
    zs = z + mu_ref[...] * (zp - z)
    r = zs[:, 0:width]
    k = zs[:, width:2 * width]
    v = zs[:, 2 * width:3 * width]
    low = zs[:, 3 * width:]

    w_pre = w0_ref[...] + _dot(jnp.tanh(low), w2_ref[...])
    logw = -jnp.exp(-_softplus(-w_pre) - 0.5)
    a_sig = _sigmoid(a0_ref[...] + _dot(low, a2_ref[...]))
    g = _dot(_sigmoid(low), g2_ref[...])

    bdp = bdp_ref[...]
    kk = k * kk_ref[...]
    kk = kk / jnp.maximum(jnp.sqrt(_headsum(kk * kk, bdp)), 1e-12)
    kf = k * (1.0 + (a_sig - 1.0) * ka_ref[...])
    bv = kk * a_sig
    bonus = _headsum(r * kf * rk_ref[...], bdp)

    lg = _dot_exact_lhs(tri_ref[...], logw)
    last_rows = [lg[c * CHUNK + CHUNK - 1:(c + 1) * CHUNK, :] for c in range(nchunks)]
    lg_last = jnp.concatenate([jnp.broadcast_to(row, (CHUNK, width)) for row in last_rows], axis=0)
    for c, row in enumerate(last_rows):
        gam_s[c] = jnp.broadcast_to(jnp.exp(row), (HALO, width))
    e_inv = jnp.exp(-lg)
    e_rem = jnp.exp(lg_last - lg)
    rt = r * jnp.exp(lg)
    at_s[...] = (-kk * jnp.exp(lg - logw)).astype(BF16)
    rt_s[...] = rt.astype(BF16)
    rtf_s[...] = rt
    bt_s[...] = (bv * e_inv).astype(BF16)
    kt_s[...] = (kf * e_inv).astype(BF16)
    bh_s[...] = (bv * e_rem).astype(BF16)
    kh_s[...] = (kf * e_rem).astype(BF16)
    v_s[...] = v.astype(BF16)

    lane = lax.broadcasted_iota(jnp.int32, (CHUNK, LANES), 1)
    trow = lax.broadcasted_iota(jnp.int32, (CHUNK, LANES), 0)
    tcol = jnp.where(lane < n, lane, lane - n)
    incl, strict = tcol <= trow, tcol < trow
    m0 = (lane < n).astype(BF16)
    m1 = (lane >= n).astype(BF16)
    m0w = jnp.concatenate([m0, m0], axis=1)
    m1w = jnp.concatenate([m1, m1], axis=1)
    prow = lax.broadcasted_iota(jnp.int32, (LANES, LANES), 0)
    pcol = lax.broadcasted_iota(jnp.int32, (LANES, LANES), 1)
    blk = ((prow < n) == (pcol < n)).astype(F32)
    eye = (prow == pcol).astype(F32)
    lsl = [slice(p * LANES, (p + 1) * LANES) for p in range(pairs)]

    def per_head(xb, wide=False):
        return jnp.concatenate([xb * (m0w if wide else m0), xb * (m1w if wide else m1)], axis=0)

    def step_body(it, carry):
        items = [(u, p) for u in range(RWKV_UNROLL) for p in range(pairs)]
        rows = [pl.ds(pl.multiple_of((it * RWKV_UNROLL + u) * CHUNK, CHUNK), CHUNK)
                for u in range(RWKV_UNROLL)]
        at_b = [at_s[rows[u], lsl[p]] for u, p in items]
        v_b = [v_s[rows[u], lsl[p]] for u, p in items]
        lhs = [jnp.concatenate([a_, rt_s[rows[u], lsl[p]]], axis=0)
               for a_, (u, p) in zip(at_b, items)]
        sb = [_dot_nt(l_, per_head(bt_s[rows[u], lsl[p]])) for l_, (u, p) in zip(lhs, items)]
        sk = [_dot_nt(l_, per_head(kt_s[rows[u], lsl[p]])) for l_, (u, p) in zip(lhs, items)]
        a_ab = [jnp.where(strict, s_[:CHUNK], 0.0).astype(BF16) for s_ in sb]
        a_rb = [jnp.where(incl, s_[CHUNK:], 0.0).astype(BF16) for s_ in sb]
        a_kk = [jnp.concatenate([jnp.where(strict, s_[:CHUNK], 0.0),
                                 jnp.where(incl, s_[CHUNK:], 0.0)], axis=0).astype(BF16) for s_ in sk]
        akv = [_dot(a_, per_head(v_)) for a_, v_ in zip(a_kk, v_b)]
        x = [jnp.concatenate([a_.astype(F32), k_[:CHUNK]], axis=1) for a_, k_ in zip(at_b, akv)]
        pw = a_ab
        for lvl in range(levels):
            x = [x_ + _dot(p_, per_head(x_.astype(BF16), wide=True)) for x_, p_ in zip(x, pw)]
            if lvl + 1 < levels:
                pw = [_dot(p_, per_head(p_)).astype(BF16) for p_ in pw]
        xb = [x_.astype(BF16) for x_ in x]
        zz = [_dot(a_, per_head(x_, wide=True)) for a_, x_ in zip(a_rb, xb)]
        mn = []
        for i, (u, p) in enumerate(items):
            lb = jnp.concatenate([bh_s[rows[u], lsl[p]], kh_s[rows[u], lsl[p]]], axis=0)
            rb = jnp.concatenate(
                [xb[i], jnp.concatenate([jnp.zeros_like(v_b[i]), v_b[i]], axis=1)], axis=0)
            mn.append(_dot_tn(lb, rb))
        t_cur = [st_ref[p] for p in range(pairs)]
        for i, (u, p) in enumerate(items):
            ls = lsl[p]
            gam = gam_s[it * RWKV_UNROLL + u][0:1, ls]
            rhat = rtf_s[rows[u], ls] + zz[i][:, :LANES]
            yzero = zz[i][:, LANES:] + akv[i][CHUNK:]
            m_mat = blk * mn[i][:, :LANES] + eye * gam
            n_mat = blk * mn[i][:, LANES:]
            y_s[rows[u], ls] = _dot_f32(rhat, t_cur[p]) + yzero
            t_cur[p] = _dot_f32(m_mat, t_cur[p]) + n_mat
        for p in range(pairs):
            st_ref[p] = t_cur[p]
        return carry

    lax.fori_loop(0, nchunks // RWKV_UNROLL, step_body, 0)

    y = y_s[...]
    mean = _headsum(y, bdp) * (1.0 / n)
    d = y - mean
    var = _headsum(d * d, bdp) * (1.0 / n)
    yn = d * lax.rsqrt(var + RWKV_GN_EPS) * gnw_ref[...] + gnb_ref[...]
    o_ref[0] = ((yn + bonus * v) * g).astype(o_ref.dtype)


def _rwkv_mixer(zr, params, *, width):
    bsz, s, cols = zr.shape
    tl = RWKV_TILE
    nchunks = tl // CHUNK
    idx = jnp.arange(tl)
    tri = ((idx[:, None] >= idx[None, :]) & (idx[:, None] // CHUNK == idx[None, :] // CHUNK)).astype(BF16)
    hid = jnp.arange(LANES) // RWKV_HEAD
    bdp = (hid[:, None] == hid[None, :]).astype(BF16)
    const2 = lambda b, j: (0, 0)
    kern = functools.partial(_rwkv_kernel, width=width)
    return pl.pallas_call(
        kern,
        grid=(bsz, s // tl),
        in_specs=[pl.BlockSpec((1, tl, cols), lambda b, j: (b, j, 0))]
        + [pl.BlockSpec(a.shape, const2) for a in params]
        + [pl.BlockSpec(tri.shape, const2), pl.BlockSpec(bdp.shape, const2)],
        out_specs=pl.BlockSpec((1, tl, width), lambda b, j: (b, j, 0)),
        out_shape=jax.ShapeDtypeStruct((bsz, s, width), BF16),
        scratch_shapes=[
            pltpu.VMEM((tl + HALO, cols), F32),
            pltpu.VMEM((width // LANES, LANES, LANES), F32),
        ] + [pltpu.VMEM((tl, width), BF16)] * 7 + [
            pltpu.VMEM((tl, width), F32),
            pltpu.VMEM((nchunks, HALO, width), F32),
            pltpu.VMEM((tl, width), F32),
        ],
        compiler_params=pltpu.CompilerParams(
            dimension_semantics=("arbitrary", "arbitrary"), vmem_limit_bytes=VMEM_LIMIT_BYTES),
        name="rwkv_mixer",
    )(zr, *params, tri, bdp)


def _tail_kernel(x_ref, og_ref, or_ref, p_ref, wo_ref, lnm_ref, w1_ref, w2_ref,
                 lnp_ref, wg_ref, wp_ref, lnf_ref, out_ref):
    gw = og_ref.shape[1]
    h = x_ref[...] + _dot(og_ref[...], wo_ref[0:gw, :]) + _dot(or_ref[...], wo_ref[gw:, :])
    n1 = _rmsnorm(h, lnm_ref[...]).astype(BF16)
    hid = jnp.maximum(_dot(n1, w1_ref[...]), 0.0)
    h = h + _dot((hid * hid).astype(BF16), w2_ref[...])
    n2 = _rmsnorm(h, lnp_ref[...]).astype(BF16)
    gate = _sigmoid(_dot(n2, wg_ref[...]))
    h = h + gate * _dot(p_ref[...].astype(BF16), wp_ref[...])
    out_ref[...] = _rmsnorm(h, lnf_ref[...])


def _tail(x2, og, orw, p2, wo, lnm, w1, w2, lnp, wg, wp, lnf, tm):
    t, d = x2.shape
    const = lambda i: (0, 0)
    rows = lambda a: pl.BlockSpec((tm, a.shape[1]), lambda i: (i, 0))
    whole = lambda a: pl.BlockSpec(a.shape, const, pipeline_mode=pl.Buffered(1))
    return pl.pallas_call(
        _tail_kernel,
        grid=(t // tm,),
        in_specs=[rows(x2), rows(og), rows(orw), rows(p2), whole(wo), whole(lnm), whole(w1),
                  whole(w2), whole(lnp), whole(wg), whole(wp), whole(lnf)],
        out_specs=pl.BlockSpec((tm, d), lambda i: (i, 0)),
        out_shape=jax.ShapeDtypeStruct((t, d), F32),
        compiler_params=pltpu.CompilerParams(
            dimension_semantics=("arbitrary",), vmem_limit_bytes=VMEM_LIMIT_BYTES),
        name="tail",
    )(x2, og, orw, p2, wo, lnm, w1, w2, lnp, wg, wp, lnf)


def _pad_cols(w, n):
    return jnp.pad(w, ((0, 0), (0, n - w.shape[1])))


def _place_rows(w, start, total):
    return jnp.pad(w, ((start, total - start - w.shape[0]), (0, 0)))


def _layer(h, p_i, ln_mix, w_in, conv_w, gk_w, gk_b, norm_g, mu, w0, w2, a0, a2, g2, k_k, k_a,
           r_k, gn_w, gn_b, w_out, ln_mlp, w_ff1, w_ff2, ln_ple, w_gate, w_proj, ln_next, tm):
    bsz, s, d = h.shape
    t = bsz * s
    row = lambda a: a.reshape(1, -1)

    gate_rank, kw = gk_w.shape
    dv = norm_g.shape[0]
    vw = (conv_w.shape[1] - 2 * kw)
    heads = vw // dv
    dk = kw // heads
    gla_cols = 2 * kw + 2 * vw + gate_rank
    gla_pad = _round_up(gla_cols, LANES)
    width = w0.shape[0]
    rwkv_cols = w_in.shape[1] - gla_cols
    rwkv_pad = _round_up(rwkv_cols, LANES)
    low_pad = rwkv_pad - 3 * width
    dr, ar = w2.shape[0], a2.shape[0]

    wg_in = _pad_cols(w_in[:, :gla_cols], gla_pad).astype(BF16)
    wr_in = _pad_cols(w_in[:, gla_cols:], rwkv_pad).astype(BF16)
    zg, zr = _in_proj(h.reshape(t, d), row(ln_mix), wg_in, wr_in, tm)

    tri = jnp.tril(jnp.ones((CHUNK, CHUNK), F32)).astype(BF16)

    gkw_p = _place_rows(gk_w, 0, gla_pad - (gla_cols - gate_rank))
    o_gla = _gla_mixer(zg.reshape(bsz, s, gla_pad), conv_w, gkw_p, row(gk_b),
                       row(norm_g), tri, heads=heads, dk=dk, dv=dv)

    rwkv_params = (
        _pad_cols(row(mu), rwkv_pad), row(w0), _place_rows(w2, 0, low_pad), row(a0),
        _place_rows(a2, dr, low_pad), _place_rows(g2, dr + ar, low_pad), row(k_k), row(k_a),
        row(r_k), row(gn_w), row(gn_b))
    o_rwkv = _rwkv_mixer(zr.reshape(bsz, s, rwkv_pad), rwkv_params, width=width)

    out = _tail(h.reshape(t, d), o_gla.reshape(t, vw), o_rwkv.reshape(t, width),
                p_i.reshape(t, -1), w_out.astype(BF16), row(ln_mlp), w_ff1.astype(BF16),
                w_ff2.astype(BF16), row(ln_ple), w_gate.astype(BF16), w_proj.astype(BF16),
                row(ln_next), tm)
    return out.reshape(bsz, s, d)


def kernel(x, p, ln_mix, w_in, gla_conv_w, gla_gk_w, gla_gk_b, gla_norm_g, rwkv_mu, rwkv_w0,
           rwkv_w2, rwkv_a0, rwkv_a2, rwkv_g2, rwkv_k_k, rwkv_k_a, rwkv_r_k, rwkv_gn_w,
           rwkv_gn_b, w_out, ln_mlp, w_ff1, w_ff2, ln_ple, w_ple_gate, w_ple_proj, ln_final):
    depth = w_in.shape[0]
    assert depth == 1, "the fused tail applies the final norm, so exactly one layer is supported"
    tm = 512
    assert (x.shape[0] * x.shape[1]) % tm == 0 and x.shape[1] % RWKV_TILE == 0
    i = 0
    return _layer(x, p[i], ln_mix[i], w_in[i], gla_conv_w[i], gla_gk_w[i], gla_gk_b[i],
                  gla_norm_g[i], rwkv_mu[i], rwkv_w0[i], rwkv_w2[i], rwkv_a0[i], rwkv_a2[i],
                  rwkv_g2[i], rwkv_k_k[i], rwkv_k_a[i], rwkv_r_k[i].reshape(-1), rwkv_gn_w[i],
                  rwkv_gn_b[i], w_out[i], ln_mlp[i], w_ff1[i], w_ff2[i], ln_ple[i],
                  w_ple_gate[i], w_ple_proj[i], ln_final, tm)
```

```python
import functools

import jax
import jax.numpy as jnp
from jax import lax
from jax.experimental import pallas as pl
from jax.experimental.pallas import tpu as pltpu

F32 = jnp.float32
BF16 = jnp.bfloat16

NORM_EPS = 1e-6
GLA_GATE_NORMALIZER = 16.0
RWKV_GN_EPS = 64e-5
RWKV_HEAD = 64
CHUNK = 64
LANES = 128
HALO = 8
GLA_TILE = 256
RWKV_TILE = 256
RWKV_UNROLL = 4
VMEM_LIMIT_BYTES = 56 * 1024 * 1024


def _round_up(n, m):
    return (n + m - 1) // m * m


def _sigmoid(x):
    return 1.0 / (1.0 + jnp.exp(-x))


def _softplus(x):
    return jnp.maximum(x, 0.0) + jnp.log(1.0 + jnp.exp(-jnp.abs(x)))


def _dot(a, b):
    return jnp.dot(a, b, preferred_element_type=F32)


def _dot_nt(a, b):
    return lax.dot_general(a, b, (((1,), (1,)), ((), ())), preferred_element_type=F32)


def _dot_tn(a, b):
    return lax.dot_general(a, b, (((0,), (0,)), ((), ())), preferred_element_type=F32)


def _split2(x):
    hi = x.astype(BF16)
    lo = (x - hi.astype(F32)).astype(BF16)
    return hi, lo


def _split3(x):
    hi = x.astype(BF16)
    r1 = x - hi.astype(F32)
    mid = r1.astype(BF16)
    lo = (r1 - mid.astype(F32)).astype(BF16)
    return hi, mid, lo


def _dot_exact_lhs(a_bf16, x):
    hi, mid, lo = _split3(x)
    return _dot(a_bf16, hi) + _dot(a_bf16, mid) + _dot(a_bf16, lo)


def _dot_f32(a, b):
    a_hi, a_lo = _split2(a)
    b_hi, b_lo = _split2(b)
    return _dot(a_hi, b_hi) + _dot(a_hi, b_lo) + _dot(a_lo, b_hi)


def _rmsnorm(x, g):
    ms = jnp.mean(x * x, axis=-1, keepdims=True)
    return x * lax.rsqrt(ms + NORM_EPS) * g


def _in_proj_kernel(x_ref, g_ref, wg_ref, wr_ref, zg_ref, zr_ref):
    xn = _rmsnorm(x_ref[...], g_ref[...]).astype(BF16)
    zg_ref[...] = _dot(xn, wg_ref[...])
    zr_ref[...] = _dot(xn, wr_ref[...])


def _in_proj(x2, ln, wg, wr, tm):
    t, d = x2.shape
    ng, nr = wg.shape[1], wr.shape[1]
    const = lambda i: (0, 0)
    return pl.pallas_call(
        _in_proj_kernel,
        grid=(t // tm,),
        in_specs=[
            pl.BlockSpec((tm, d), lambda i: (i, 0)),
            pl.BlockSpec((1, d), const),
            pl.BlockSpec((d, ng), const, pipeline_mode=pl.Buffered(1)),
            pl.BlockSpec((d, nr), const, pipeline_mode=pl.Buffered(1)),
        ],
        out_specs=[
            pl.BlockSpec((tm, ng), lambda i: (i, 0)),
            pl.BlockSpec((tm, nr), lambda i: (i, 0)),
        ],
        out_shape=[
            jax.ShapeDtypeStruct((t, ng), F32),
            jax.ShapeDtypeStruct((t, nr), F32),
        ],
        compiler_params=pltpu.CompilerParams(
            dimension_semantics=("arbitrary",), vmem_limit_bytes=VMEM_LIMIT_BYTES),
        name="in_proj",
    )(x2, ln, wg, wr)


def _gla_kernel(z_ref, cw_ref, gkw_ref, gkb_ref, ng_ref, tri_ref, o_ref,
                xb_ref, st_ref, *, heads, dk, dv):
    kw = heads * dk
    vw = heads * dv
    nqkv = 2 * kw + vw
    taps = cw_ref.shape[0]
    pairs = kw // LANES

    @pl.when(pl.program_id(1) == 0)
    def _():
        xb_ref[0:HALO, :] = jnp.zeros((HALO, nqkv), F32)
        st_ref[...] = jnp.zeros_like(st_ref)

    tl = z_ref.shape[1]
    nchunks = tl // CHUNK
    z = z_ref[0]
    xb_ref[HALO:HALO + tl, :] = z[:, :nqkv]
    conv = jnp.zeros((tl, nqkv), F32)
    for t in range(taps):
        off = HALO - (taps - 1) + t
        conv = conv + cw_ref[t:t + 1, :] * xb_ref[off:off + tl, :]
    xb_ref[0:HALO, :] = xb_ref[tl:tl + HALO, :]
    qkv = conv * _sigmoid(conv)
    q = qkv[:, :kw] * (dk ** -0.5)
    k = qkv[:, kw:2 * kw]
    v = qkv[:, 2 * kw:nqkv].astype(BF16)
    g = z[:, nqkv:nqkv + vw]
    low = z[:, nqkv + vw:]

    gk_pre = _dot(low, gkw_ref[...]) + gkb_ref[...]
    gk = -_softplus(-gk_pre) * (1.0 / GLA_GATE_NORMALIZER)
    b = _dot_exact_lhs(tri_ref[...], gk)
    last_rows = [b[c * CHUNK + CHUNK - 1:(c + 1) * CHUNK, :] for c in range(nchunks)]
    b_last = jnp.concatenate([jnp.broadcast_to(row, (CHUNK, kw)) for row in last_rows], axis=0)
    b_ref = 0.5 * b_last
    qe = (q * jnp.exp(b - b_ref)).astype(BF16)
    ke = (k * jnp.exp(b_ref - b)).astype(BF16)
    qb = (q * jnp.exp(b)).astype(BF16)
    kd = (k * jnp.exp(b_last - b)).astype(BF16)
    e_last = [jnp.exp(row) for row in last_rows]

    lane = lax.broadcasted_iota(jnp.int32, (CHUNK, LANES), 1)
    trow = lax.broadcasted_iota(jnp.int32, (CHUNK, LANES), 0)
    causal = jnp.where(lane < dk, lane, lane - dk) <= trow
    m0 = (lane < dk).astype(BF16)
    m1 = (lane >= dk).astype(BF16)
    vlane = lax.broadcasted_iota(jnp.int32, (CHUNK, 2 * dv), 1)
    v0 = (vlane < dv).astype(BF16)
    v1 = (vlane >= dv).astype(BF16)
    srow = lax.broadcasted_iota(jnp.int32, (2 * dv, LANES), 0)
    scol = lax.broadcasted_iota(jnp.int32, (2 * dv, LANES), 1)
    st_mask = ((srow < dv) == (scol < dk)).astype(F32)

    items = [(c, p) for c in range(nchunks) for p in range(pairs)]
    rs = [slice(c * CHUNK, (c + 1) * CHUNK) for c in range(nchunks)]
    ls = [slice(p * LANES, (p + 1) * LANES) for p in range(pairs)]
    vs = [slice(2 * p * dv, 2 * (p + 1) * dv) for p in range(pairs)]
    s = [_dot_nt(qe[rs[c], ls[p]],
                 jnp.concatenate([ke[rs[c], ls[p]] * m0, ke[rs[c], ls[p]] * m1], axis=0))
         for c, p in items]
    s = [jnp.where(causal, s_, 0.0).astype(BF16) for s_ in s]
    o_intra = [_dot(s_, jnp.concatenate([v[rs[c], vs[p]] * v0, v[rs[c], vs[p]] * v1], axis=0))
               for s_, (c, p) in zip(s, items)]
    kv = [_dot_tn(v[rs[c], vs[p]], kd[rs[c], ls[p]]) for c, p in items]
    st = [st_ref[p] for p in range(pairs)]
    o_rows = []
    for c in range(nchunks):
        o_c = []
        for p in range(pairs):
            i = c * pairs + p
            o_c.append(o_intra[i] + _dot_nt(qb[rs[c], ls[p]], st[p].astype(BF16)))
            st[p] = st[p] * e_last[c][:, ls[p]] + st_mask * kv[i]
        o_rows.append(jnp.concatenate(o_c, axis=1))
    for p in range(pairs):
        st_ref[p] = st[p]
    o = jnp.concatenate(o_rows, axis=0)
    o_n = []
    for h in range(heads):
        o_h = o[:, h * dv:(h + 1) * dv]
        ms = jnp.mean(o_h * o_h, axis=-1, keepdims=True)
        o_n.append(o_h * lax.rsqrt(ms + NORM_EPS) * ng_ref[...])
    o_ref[0] = (jnp.concatenate(o_n, axis=1) * (g * _sigmoid(g))).astype(o_ref.dtype)


def _chunk_tri(tl):
    idx = jnp.arange(tl)
    same = idx[:, None] // CHUNK == idx[None, :] // CHUNK
    return ((idx[:, None] >= idx[None, :]) & same).astype(BF16)


def _gla_mixer(zg, conv_w, gkw_p, gk_b, norm_g, *, heads, dk, dv):
    bsz, s, cols = zg.shape
    kw, vw = heads * dk, heads * dv
    nqkv = 2 * kw + vw
    tl = GLA_TILE
    tri = _chunk_tri(tl)
    const2 = lambda b, j: (0, 0)
    kern = functools.partial(_gla_kernel, heads=heads, dk=dk, dv=dv)
    return pl.pallas_call(
        kern,
        grid=(bsz, s // tl),
        in_specs=[
            pl.BlockSpec((1, tl, cols), lambda b, j: (b, j, 0)),
            pl.BlockSpec(conv_w.shape, const2),
            pl.BlockSpec(gkw_p.shape, const2),
            pl.BlockSpec(gk_b.shape, const2),
            pl.BlockSpec(norm_g.shape, const2),
            pl.BlockSpec(tri.shape, const2),
        ],
        out_specs=pl.BlockSpec((1, tl, vw), lambda b, j: (b, j, 0)),
        out_shape=jax.ShapeDtypeStruct((bsz, s, vw), BF16),
        scratch_shapes=[
            pltpu.VMEM((tl + HALO, nqkv), F32),
            pltpu.VMEM((kw // LANES, 2 * dv, LANES), F32),
        ],
        compiler_params=pltpu.CompilerParams(
            dimension_semantics=("arbitrary", "arbitrary"), vmem_limit_bytes=VMEM_LIMIT_BYTES),
        name="gla_mixer",
    )(zg, conv_w, gkw_p, gk_b, norm_g, tri)


def _headsum(x, bd_pair):
    rows, width = x.shape
    groups = width // LANES
    hi, lo = _split2(x)
    stacked = jnp.concatenate(
        [t[:, i * LANES:(i + 1) * LANES] for t in (hi, lo) for i in range(groups)], axis=0)
    s = _dot(stacked, bd_pair)
    return jnp.concatenate(
        [s[i * rows:(i + 1) * rows] + s[(groups + i) * rows:(groups + i + 1) * rows]
         for i in range(groups)], axis=1)


def _rwkv_kernel(z_ref, mu_ref, w0_ref, w2_ref, a0_ref, a2_ref, g2_ref, kk_ref, ka_ref,
                 rk_ref, gnw_ref, gnb_ref, tri_ref, bdp_ref, o_ref,
                 xb_ref, st_ref, at_s, rt_s, bt_s, kt_s, bh_s, kh_s, v_s, rtf_s, gam_s, y_s,
                 *, width):
    n = RWKV_HEAD
    pairs = width // LANES
    tl = z_ref.shape[1]
    nchunks = tl // CHUNK
    levels = (CHUNK - 1).bit_length()

    @pl.when(pl.program_id(1) == 0)
    def _():
        xb_ref[0:HALO, :] = jnp.zeros((HALO, xb_ref.shape[1]), F32)
        st_ref[...] = jnp.zeros_like(st_ref)

    z = z_ref[0]
    xb_ref[HALO:HALO + tl, :] = z
    zp = xb_ref[HALO - 1:HALO - 1 + tl, :]
    xb_ref[0:HALO, :] = xb_ref[tl:tl + HALO, :]
    zs = z + mu_ref[...] * (zp - z)
    r = zs[:, 0:width]
    k = zs[:, width:2 * width]
    v = zs[:, 2 * width:3 * width]
    low = zs[:, 3 * width:]

    w_pre = w0_ref[...] + _dot(jnp.tanh(low), w2_ref[...])
    logw = -jnp.exp(-_softplus(-w_pre) - 0.5)
    a_sig = _sigmoid(a0_ref[...] + _dot(low, a2_ref[...]))
    g = _dot(_sigmoid(low), g2_ref[...])

    bdp = bdp_ref[...]
    kk = k * kk_ref[...]
    kk = kk / jnp.maximum(jnp.sqrt(_headsum(kk * kk, bdp)), 1e-12)
    kf = k * (1.0 + (a_sig - 1.0) * ka_ref[...])
    bv = kk * a_sig
    bonus = _headsum(r * kf * rk_ref[...], bdp)

    lg = _dot_exact_lhs(tri_ref[...], logw)
    last_rows = [lg[c * CHUNK + CHUNK - 1:(c + 1) * CHUNK, :] for c in range(nchunks)]
    lg_last = jnp.concatenate([jnp.broadcast_to(row, (CHUNK, width)) for row in last_rows], axis=0)
    for c, row in enumerate(last_rows):
        gam_s[c] = jnp.broadcast_to(jnp.exp(row), (HALO, width))
    e_inv = jnp.exp(-lg)
    e_rem = jnp.exp(lg_last - lg)
    rt = r * jnp.exp(lg)
    at_s[...] = (-kk * jnp.exp(lg - logw)).astype(BF16)
    rt_s[...] = rt.astype(BF16)
    rtf_s[...] = rt
    bt_s[...] = (bv * e_inv).astype(BF16)
    kt_s[...] = (kf * e_inv).astype(BF16)
    bh_s[...] = (bv * e_rem).astype(BF16)
    kh_s[...] = (kf * e_rem).astype(BF16)
    v_s[...] = v.astype(BF16)

    lane = lax.broadcasted_iota(jnp.int32, (CHUNK, LANES), 1)
    trow = lax.broadcasted_iota(jnp.int32, (CHUNK, LANES), 0)
    tcol = jnp.where(lane < n, lane, lane - n)
    incl, strict = tcol <= trow, tcol < trow
    m0 = (lane < n).astype(BF16)
    m1 = (lane >= n).astype(BF16)
    m0w = jnp.concatenate([m0, m0], axis=1)
    m1w = jnp.concatenate([m1, m1], axis=1)
    prow = lax.broadcasted_iota(jnp.int32, (LANES, LANES), 0)
    pcol = lax.broadcasted_iota(jnp.int32, (LANES, LANES), 1)
    blk = ((prow < n) == (pcol < n)).astype(F32)
    eye = (prow == pcol).astype(F32)
    lsl = [slice(p * LANES, (p + 1) * LANES) for p in range(pairs)]

    def per_head(xb, wide=False):
        return jnp.concatenate([xb * (m0w if wide else m0), xb * (m1w if wide else m1)], axis=0)

    def step_body(it, carry):
        items = [(u, p) for u in range(RWKV_UNROLL) for p in range(pairs)]
        rows = [pl.ds(pl.multiple_of((it * RWKV_UNROLL + u) * CHUNK, CHUNK), CHUNK)
                for u in range(RWKV_UNROLL)]
        at_b = [at_s[rows[u], lsl[p]] for u, p in items]
        v_b = [v_s[rows[u], lsl[p]] for u, p in items]
        lhs = [jnp.concatenate([a_, rt_s[rows[u], lsl[p]]], axis=0)
               for a_, (u, p) in zip(at_b, items)]
        sb = [_dot_nt(l_, per_head(bt_s[rows[u], lsl[p]])) for l_, (u, p) in zip(lhs, items)]
        sk = [_dot_nt(l_, per_head(kt_s[rows[u], lsl[p]])) for l_, (u, p) in zip(lhs, items)]
        a_ab = [jnp.where(strict, s_[:CHUNK], 0.0).astype(BF16) for s_ in sb]
        a_rb = [jnp.where(incl, s_[CHUNK:], 0.0).astype(BF16) for s_ in sb]
        a_kk = [jnp.concatenate([jnp.where(strict, s_[:CHUNK], 0.0),
                                 jnp.where(incl, s_[CHUNK:], 0.0)], axis=0).astype(BF16) for s_ in sk]
        akv = [_dot(a_, per_head(v_)) for a_, v_ in zip(a_kk, v_b)]
        x = [jnp.concatenate([a_.astype(F32), k_[:CHUNK]], axis=1) for a_, k_ in zip(at_b, akv)]
        pw = a_ab
        for lvl in range(levels):
            x = [x_ + _dot(p_, per_head(x_.astype(BF16), wide=True)) for x_, p_ in zip(x, pw)]
            if lvl + 1 < levels:
                pw = [_dot(p_, per_head(p_)).astype(BF16) for p_ in pw]
        xb = [x_.astype(BF16) for x_ in x]
        zz = [_dot(a_, per_head(x_, wide=True)) for a_, x_ in zip(a_rb, xb)]
        mn = []
        for i, (u, p) in enumerate(items):
            lb = jnp.concatenate([bh_s[rows[u], lsl[p]], kh_s[rows[u], lsl[p]]], axis=0)
            rb = jnp.concatenate(
                [xb[i], jnp.concatenate([jnp.zeros_like(v_b[i]), v_b[i]], axis=1)], axis=0)
            mn.append(_dot_tn(lb, rb))
        t_cur = [st_ref[p] for p in range(pairs)]
        for i, (u, p) in enumerate(items):
            ls = lsl[p]
            gam = gam_s[it * RWKV_UNROLL + u][0:1, ls]
            rhat = rtf_s[rows[u], ls] + zz[i][:, :LANES]
            yzero = zz[i][:, LANES:] + akv[i][CHUNK:]
            m_mat = blk * mn[i][:, :LANES] + eye * gam
            n_mat = blk * mn[i][:, LANES:]
            y_s[rows[u], ls] = _dot_f32(rhat, t_cur[p]) + yzero
            t_cur[p] = _dot_f32(m_mat, t_cur[p]) + n_mat
        for p in range(pairs):
            st_ref[p] = t_cur[p]
        return carry

    lax.fori_loop(0, nchunks // RWKV_UNROLL, step_body, 0)

    y = y_s[...]
    mean = _headsum(y, bdp) * (1.0 / n)
    d = y - mean
    var = _headsum(d * d, bdp) * (1.0 / n)
    yn = d * lax.rsqrt(var + RWKV_GN_EPS) * gnw_ref[...] + gnb_ref[...]
    o_ref[0] = ((yn + bonus * v) * g).astype(o_ref.dtype)


def _rwkv_mixer(zr, params, *, width):
    bsz, s, cols = zr.shape
    tl = RWKV_TILE
    nchunks = tl // CHUNK
    tri = _chunk_tri(tl)
    hid = jnp.arange(LANES) // RWKV_HEAD
    bdp = (hid[:, None] == hid[None, :]).astype(BF16)
    const2 = lambda b, j: (0, 0)
    kern = functools.partial(_rwkv_kernel, width=width)
    return pl.pallas_call(
        kern,
        grid=(bsz, s // tl),
        in_specs=[pl.BlockSpec((1, tl, cols), lambda b, j: (b, j, 0))]
        + [pl.BlockSpec(a.shape, const2) for a in params]
        + [pl.BlockSpec(tri.shape, const2), pl.BlockSpec(bdp.shape, const2)],
        out_specs=pl.BlockSpec((1, tl, width), lambda b, j: (b, j, 0)),
        out_shape=jax.ShapeDtypeStruct((bsz, s, width), BF16),
        scratch_shapes=[
            pltpu.VMEM((tl + HALO, cols), F32),
            pltpu.VMEM((width // LANES, LANES, LANES), F32),
        ] + [pltpu.VMEM((tl, width), BF16)] * 7 + [
            pltpu.VMEM((tl, width), F32),
            pltpu.VMEM((nchunks, HALO, width), F32),
            pltpu.VMEM((tl, width), F32),
        ],
        compiler_params=pltpu.CompilerParams(
            dimension_semantics=("arbitrary", "arbitrary"), vmem_limit_bytes=VMEM_LIMIT_BYTES),
        name="rwkv_mixer",
    )(zr, *params, tri, bdp)


def _tail_kernel(x_ref, og_ref, or_ref, p_ref, wo_ref, lnm_ref, w1_ref, w2_ref,
                 lnp_ref, wg_ref, wp_ref, lnf_ref, out_ref):
    gw = og_ref.shape[1]
    h = x_ref[...] + _dot(og_ref[...], wo_ref[0:gw, :]) + _dot(or_ref[...], wo_ref[gw:, :])
    n1 = _rmsnorm(h, lnm_ref[...]).astype(BF16)
    hid = jnp.maximum(_dot(n1, w1_ref[...]), 0.0)
    h = h + _dot((hid * hid).astype(BF16), w2_ref[...])
    n2 = _rmsnorm(h, lnp_ref[...]).astype(BF16)
    gate = _sigmoid(_dot(n2, wg_ref[...]))
    h = h + gate * _dot(p_ref[...].astype(BF16), wp_ref[...])
    out_ref[...] = _rmsnorm(h, lnf_ref[...])


def _tail(x2, og, orw, p2, wo, lnm, w1, w2, lnp, wg, wp, lnf, tm):
    t, d = x2.shape
    const = lambda i: (0, 0)
    rows = lambda a: pl.BlockSpec((tm, a.shape[1]), lambda i: (i, 0))
    whole = lambda a: pl.BlockSpec(a.shape, const, pipeline_mode=pl.Buffered(1))
    return pl.pallas_call(
        _tail_kernel,
        grid=(t // tm,),
        in_specs=[rows(x2), rows(og), rows(orw), rows(p2), whole(wo), whole(lnm), whole(w1),
                  whole(w2), whole(lnp), whole(wg), whole(wp), whole(lnf)],
        out_specs=pl.BlockSpec((tm, d), lambda i: (i, 0)),
        out_shape=jax.ShapeDtypeStruct((t, d), F32),
        compiler_params=pltpu.CompilerParams(
            dimension_semantics=("arbitrary",), vmem_limit_bytes=VMEM_LIMIT_BYTES),
        name="tail",
    )(x2, og, orw, p2, wo, lnm, w1, w2, lnp, wg, wp, lnf)


def _pad_cols(w, n):
    return jnp.pad(w, ((0, 0), (0, n - w.shape[1])))


def _place_rows(w, start, total):
    return jnp.pad(w, ((start, total - start - w.shape[0]), (0, 0)))


def _layer(h, p_i, ln_mix, w_in, conv_w, gk_w, gk_b, norm_g, mu, w0, w2, a0, a2, g2, k_k, k_a,
           r_k, gn_w, gn_b, w_out, ln_mlp, w_ff1, w_ff2, ln_ple, w_gate, w_proj, ln_next, tm):
    bsz, s, d = h.shape
    t = bsz * s
    row = lambda a: a.reshape(1, -1)

    gate_rank, kw = gk_w.shape
    dv = norm_g.shape[0]
    vw = (conv_w.shape[1] - 2 * kw)
    heads = vw // dv
    dk = kw // heads
    gla_cols = 2 * kw + 2 * vw + gate_rank
    gla_pad = _round_up(gla_cols, LANES)
    width = w0.shape[0]
    rwkv_cols = w_in.shape[1] - gla_cols
    rwkv_pad = _round_up(rwkv_cols, LANES)
    low_pad = rwkv_pad - 3 * width
    dr, ar = w2.shape[0], a2.shape[0]

    wg_in = _pad_cols(w_in[:, :gla_cols], gla_pad).astype(BF16)
    wr_in = _pad_cols(w_in[:, gla_cols:], rwkv_pad).astype(BF16)
    zg, zr = _in_proj(h.reshape(t, d), row(ln_mix), wg_in, wr_in, tm)

    gkw_p = _place_rows(gk_w, 0, gla_pad - (gla_cols - gate_rank))
    o_gla = _gla_mixer(zg.reshape(bsz, s, gla_pad), conv_w, gkw_p, row(gk_b),
                       row(norm_g), heads=heads, dk=dk, dv=dv)

    rwkv_params = (
        _pad_cols(row(mu), rwkv_pad), row(w0), _place_rows(w2, 0, low_pad), row(a0),
        _place_rows(a2, dr, low_pad), _place_rows(g2, dr + ar, low_pad), row(k_k), row(k_a),
        row(r_k), row(gn_w), row(gn_b))
    o_rwkv = _rwkv_mixer(zr.reshape(bsz, s, rwkv_pad), rwkv_params, width=width)

    out = _tail(h.reshape(t, d), o_gla.reshape(t, vw), o_rwkv.reshape(t, width),
                p_i.reshape(t, -1), w_out.astype(BF16), row(ln_mlp), w_ff1.astype(BF16),
                w_ff2.astype(BF16), row(ln_ple), w_gate.astype(BF16), w_proj.astype(BF16),
                row(ln_next), tm)
    return out.reshape(bsz, s, d)


def kernel(x, p, ln_mix, w_in, gla_conv_w, gla_gk_w, gla_gk_b, gla_norm_g, rwkv_mu, rwkv_w0,
           rwkv_w2, rwkv_a0, rwkv_a2, rwkv_g2, rwkv_k_k, rwkv_k_a, rwkv_r_k, rwkv_gn_w,
           rwkv_gn_b, w_out, ln_mlp, w_ff1, w_ff2, ln_ple, w_ple_gate, w_ple_proj, ln_final):
    depth = w_in.shape[0]
    assert depth == 1, "the fused tail applies the final norm, so exactly one layer is supported"
    tm = 512
    assert (x.shape[0] * x.shape[1]) % tm == 0 and x.shape[1] % RWKV_TILE == 0
    i = 0
    return _layer(x, p[i], ln_mix[i], w_in[i], gla_conv_w[i], gla_gk_w[i], gla_gk_b[i],
                  gla_norm_g[i], rwkv_mu[i], rwkv_w0[i], rwkv_w2[i], rwkv_a0[i], rwkv_a2[i],
                  rwkv_g2[i], rwkv_k_k[i], rwkv_k_a[i], rwkv_r_k[i].reshape(-1), rwkv_gn_w[i],
                  rwkv_gn_b[i], w_out[i], ln_mlp[i], w_ff1[i], w_ff2[i], ln_ple[i],
                  w_ple_gate[i], w_ple_proj[i], ln_final, tm)
```

```python
import functools

import jax
import jax.numpy as jnp
from jax import lax
from jax.experimental import pallas as pl
from jax.experimental.pallas import tpu as pltpu

F32 = jnp.float32
BF16 = jnp.bfloat16

NORM_EPS = 1e-6
GLA_GATE_NORMALIZER = 16.0
RWKV_GN_EPS = 64e-5
RWKV_HEAD = 64
CHUNK = 64
LANES = 128
HALO = 8
TILE = 256
RWKV_UNROLL = 4
VMEM_LIMIT_BYTES = 56 * 1024 * 1024


def _round_up(n, m):
    return (n + m - 1) // m * m


def _sigmoid(x):
    return 1.0 / (1.0 + jnp.exp(-x))


def _softplus(x):
    return jnp.maximum(x, 0.0) + jnp.log(1.0 + jnp.exp(-jnp.abs(x)))


def _dot(a, b):
    return jnp.dot(a, b, preferred_element_type=F32)


def _dot_nt(a, b):
    return lax.dot_general(a, b, (((1,), (1,)), ((), ())), preferred_element_type=F32)


def _dot_tn(a, b):
    return lax.dot_general(a, b, (((0,), (0,)), ((), ())), preferred_element_type=F32)


def _split2(x):
    hi = x.astype(BF16)
    lo = (x - hi.astype(F32)).astype(BF16)
    return hi, lo


def _split3(x):
    hi = x.astype(BF16)
    r1 = x - hi.astype(F32)
    mid = r1.astype(BF16)
    lo = (r1 - mid.astype(F32)).astype(BF16)
    return hi, mid, lo


def _dot_exact_lhs(a_bf16, x):
    hi, mid, lo = _split3(x)
    return _dot(a_bf16, hi) + _dot(a_bf16, mid) + _dot(a_bf16, lo)


def _dot_f32(a, b):
    a_hi, a_lo = _split2(a)
    b_hi, b_lo = _split2(b)
    return _dot(a_hi, b_hi) + _dot(a_hi, b_lo) + _dot(a_lo, b_hi)


def _rmsnorm(x, g):
    ms = jnp.mean(x * x, axis=-1, keepdims=True)
    return x * lax.rsqrt(ms + NORM_EPS) * g


def _chunk_tri(tl):
    idx = jnp.arange(tl)
    same = idx[:, None] // CHUNK == idx[None, :] // CHUNK
    return ((idx[:, None] >= idx[None, :]) & same).astype(BF16)


def _gla_body(z, cw_ref, gkw_ref, gkb_ref, ng_ref, tri_ref, xb_ref, st_ref, *, heads, dk, dv):
    kw = heads * dk
    vw = heads * dv
    nqkv = 2 * kw + vw
    taps = cw_ref.shape[0]
    pairs = kw // LANES
    tl = z.shape[0]
    nchunks = tl // CHUNK
    xb_ref[HALO:HALO + tl, :] = z[:, :nqkv]
    conv = jnp.zeros((tl, nqkv), F32)
    for t in range(taps):
        off = HALO - (taps - 1) + t
        conv = conv + cw_ref[t:t + 1, :] * xb_ref[off:off + tl, :]
    xb_ref[0:HALO, :] = xb_ref[tl:tl + HALO, :]
    qkv = conv * _sigmoid(conv)
    q = qkv[:, :kw] * (dk ** -0.5)
    k = qkv[:, kw:2 * kw]
    v = qkv[:, 2 * kw:nqkv].astype(BF16)
    g = z[:, nqkv:nqkv + vw]
    low = z[:, nqkv + vw:]

    gk_pre = _dot(low, gkw_ref[...]) + gkb_ref[...]
    yield
    gk = -_softplus(-gk_pre) * (1.0 / GLA_GATE_NORMALIZER)
    b = _dot_exact_lhs(tri_ref[...], gk)
    yield
    last_rows = [b[c * CHUNK + CHUNK - 1:(c + 1) * CHUNK, :] for c in range(nchunks)]
    b_last = jnp.concatenate([jnp.broadcast_to(row, (CHUNK, kw)) for row in last_rows], axis=0)
    b_ref = 0.5 * b_last
    qe = (q * jnp.exp(b - b_ref)).astype(BF16)
    ke = (k * jnp.exp(b_ref - b)).astype(BF16)
    qb = (q * jnp.exp(b)).astype(BF16)
    kd = (k * jnp.exp(b_last - b)).astype(BF16)
    e_last = [jnp.exp(row) for row in last_rows]

    lane = lax.broadcasted_iota(jnp.int32, (CHUNK, LANES), 1)
    trow = lax.broadcasted_iota(jnp.int32, (CHUNK, LANES), 0)
    causal = jnp.where(lane < dk, lane, lane - dk) <= trow
    m0 = (lane < dk).astype(BF16)
    m1 = (lane >= dk).astype(BF16)
    vlane = lax.broadcasted_iota(jnp.int32, (CHUNK, 2 * dv), 1)
    v0 = (vlane < dv).astype(BF16)
    v1 = (vlane >= dv).astype(BF16)
    srow = lax.broadcasted_iota(jnp.int32, (2 * dv, LANES), 0)
    scol = lax.broadcasted_iota(jnp.int32, (2 * dv, LANES), 1)
    st_mask = ((srow < dv) == (scol < dk)).astype(F32)

    items = [(c, p) for c in range(nchunks) for p in range(pairs)]
    rs = [slice(c * CHUNK, (c + 1) * CHUNK) for c in range(nchunks)]
    ls = [slice(p * LANES, (p + 1) * LANES) for p in range(pairs)]
    vs = [slice(2 * p * dv, 2 * (p + 1) * dv) for p in range(pairs)]
    s = [_dot_nt(qe[rs[c], ls[p]],
                 jnp.concatenate([ke[rs[c], ls[p]] * m0, ke[rs[c], ls[p]] * m1], axis=0))
         for c, p in items]
    s = [jnp.where(causal, s_, 0.0).astype(BF16) for s_ in s]
    o_intra = [_dot(s_, jnp.concatenate([v[rs[c], vs[p]] * v0, v[rs[c], vs[p]] * v1], axis=0))
               for s_, (c, p) in zip(s, items)]
    kv = [_dot_tn(v[rs[c], vs[p]], kd[rs[c], ls[p]]) for c, p in items]
    st = [st_ref[p] for p in range(pairs)]
    o_rows = []
    for c in range(nchunks):
        o_c = []
        for p in range(pairs):
            i = c * pairs + p
            o_c.append(o_intra[i] + _dot_nt(qb[rs[c], ls[p]], st[p].astype(BF16)))
            st[p] = st[p] * e_last[c][:, ls[p]] + st_mask * kv[i]
        o_rows.append(jnp.concatenate(o_c, axis=1))
    for p in range(pairs):
        st_ref[p] = st[p]
    o = jnp.concatenate(o_rows, axis=0)
    o_n = []
    for h in range(heads):
        o_h = o[:, h * dv:(h + 1) * dv]
        ms = jnp.mean(o_h * o_h, axis=-1, keepdims=True)
        o_n.append(o_h * lax.rsqrt(ms + NORM_EPS) * ng_ref[...])
    return (jnp.concatenate(o_n, axis=1) * (g * _sigmoid(g))).astype(BF16)


def _headsum(x, bd_pair):
    rows, width = x.shape
    groups = width // LANES
    hi, lo = _split2(x)
    stacked = jnp.concatenate(
        [t[:, i * LANES:(i + 1) * LANES] for t in (hi, lo) for i in range(groups)], axis=0)
    s = _dot(stacked, bd_pair)
    return jnp.concatenate(
        [s[i * rows:(i + 1) * rows] + s[(groups + i) * rows:(groups + i + 1) * rows]
         for i in range(groups)], axis=1)


def _rwkv_body(z, mu_ref, w0_ref, w2_ref, a0_ref, a2_ref, g2_ref, kk_ref, ka_ref,
               rk_ref, gnw_ref, gnb_ref, tri_ref, bdp_ref,
               xb_ref, st_ref, at_s, rt_s, bt_s, kt_s, bh_s, kh_s, v_s, rtf_s, gam_s, y_s):
    n = RWKV_HEAD
    width = w0_ref.shape[1]
    pairs = width // LANES
    tl = z.shape[0]
    nchunks = tl // CHUNK
    levels = (CHUNK - 1).bit_length()

    xb_ref[HALO:HALO + tl, :] = z
    zp = xb_ref[HALO - 1:HALO - 1 + tl, :]
    xb_ref[0:HALO, :] = xb_ref[tl:tl + HALO, :]
    zs = z + mu_ref[...] * (zp - z)
    r = zs[:, 0:width]
    k = zs[:, width:2 * width]
    v = zs[:, 2 * width:3 * width]
    low = zs[:, 3 * width:]

    w_pre = w0_ref[...] + _dot(jnp.tanh(low), w2_ref[...])
    logw = -jnp.exp(-_softplus(-w_pre) - 0.5)
    a_sig = _sigmoid(a0_ref[...] + _dot(low, a2_ref[...]))
    g = _dot(_sigmoid(low), g2_ref[...])
    yield

    bdp = bdp_ref[...]
    kk = k * kk_ref[...]
    kk = kk / jnp.maximum(jnp.sqrt(_headsum(kk * kk, bdp)), 1e-12)
    kf = k * (1.0 + (a_sig - 1.0) * ka_ref[...])
    bv = kk * a_sig
    bonus = _headsum(r * kf * rk_ref[...], bdp)

    lg = _dot_exact_lhs(tri_ref[...], logw)
    yield
    last_rows = [lg[c * CHUNK + CHUNK - 1:(c + 1) * CHUNK, :] for c in range(nchunks)]
    lg_last = jnp.concatenate([jnp.broadcast_to(row, (CHUNK, width)) for row in last_rows], axis=0)
    for c, row in enumerate(last_rows):
        gam_s[c] = jnp.broadcast_to(jnp.exp(row), (HALO, width))
    e_inv = jnp.exp(-lg)
    e_rem = jnp.exp(lg_last - lg)
    rt = r * jnp.exp(lg)
    at_s[...] = (-kk * jnp.exp(lg - logw)).astype(BF16)
    rt_s[...] = rt.astype(BF16)
    rtf_s[...] = rt
    bt_s[...] = (bv * e_inv).astype(BF16)
    kt_s[...] = (kf * e_inv).astype(BF16)
    bh_s[...] = (bv * e_rem).astype(BF16)
    kh_s[...] = (kf * e_rem).astype(BF16)
    v_s[...] = v.astype(BF16)
    yield

    lane = lax.broadcasted_iota(jnp.int32, (CHUNK, LANES), 1)
    trow = lax.broadcasted_iota(jnp.int32, (CHUNK, LANES), 0)
    tcol = jnp.where(lane < n, lane, lane - n)
    incl, strict = tcol <= trow, tcol < trow
    m0 = (lane < n).astype(BF16)
    m1 = (lane >= n).astype(BF16)
    m0w = jnp.concatenate([m0, m0], axis=1)
    m1w = jnp.concatenate([m1, m1], axis=1)
    prow = lax.broadcasted_iota(jnp.int32, (LANES, LANES), 0)
    pcol = lax.broadcasted_iota(jnp.int32, (LANES, LANES), 1)
    blk = ((prow < n) == (pcol < n)).astype(F32)
    eye = (prow == pcol).astype(F32)
    lsl = [slice(p * LANES, (p + 1) * LANES) for p in range(pairs)]

    def per_head(xb, wide=False):
        return jnp.concatenate([xb * (m0w if wide else m0), xb * (m1w if wide else m1)], axis=0)

    def step_body(it, carry):
        items = [(u, p) for u in range(RWKV_UNROLL) for p in range(pairs)]
        rows = [pl.ds(pl.multiple_of((it * RWKV_UNROLL + u) * CHUNK, CHUNK), CHUNK)
                for u in range(RWKV_UNROLL)]
        at_b = [at_s[rows[u], lsl[p]] for u, p in items]
        v_b = [v_s[rows[u], lsl[p]] for u, p in items]
        lhs = [jnp.concatenate([a_, rt_s[rows[u], lsl[p]]], axis=0)
               for a_, (u, p) in zip(at_b, items)]
        sb = [_dot_nt(l_, per_head(bt_s[rows[u], lsl[p]])) for l_, (u, p) in zip(lhs, items)]
        sk = [_dot_nt(l_, per_head(kt_s[rows[u], lsl[p]])) for l_, (u, p) in zip(lhs, items)]
        a_ab = [jnp.where(strict, s_[:CHUNK], 0.0).astype(BF16) for s_ in sb]
        a_rb = [jnp.where(incl, s_[CHUNK:], 0.0).astype(BF16) for s_ in sb]
        a_kk = [jnp.concatenate([jnp.where(strict, s_[:CHUNK], 0.0),
                                 jnp.where(incl, s_[CHUNK:], 0.0)], axis=0).astype(BF16) for s_ in sk]
        akv = [_dot(a_, per_head(v_)) for a_, v_ in zip(a_kk, v_b)]
        x = [jnp.concatenate([a_.astype(F32), k_[:CHUNK]], axis=1) for a_, k_ in zip(at_b, akv)]
        pw = a_ab
        for lvl in range(levels):
            x = [x_ + _dot(p_, per_head(x_.astype(BF16), wide=True)) for x_, p_ in zip(x, pw)]
            if lvl + 1 < levels:
                pw = [_dot(p_, per_head(p_)).astype(BF16) for p_ in pw]
        xb = [x_.astype(BF16) for x_ in x]
        zz = [_dot(a_, per_head(x_, wide=True)) for a_, x_ in zip(a_rb, xb)]
        mn = []
        for i, (u, p) in enumerate(items):
            lb = jnp.concatenate([bh_s[rows[u], lsl[p]], kh_s[rows[u], lsl[p]]], axis=0)
            rb = jnp.concatenate(
                [xb[i], jnp.concatenate([jnp.zeros_like(v_b[i]), v_b[i]], axis=1)], axis=0)
            mn.append(_dot_tn(lb, rb))
        t_cur = [st_ref[p] for p in range(pairs)]
        for i, (u, p) in enumerate(items):
            ls = lsl[p]
            gam = gam_s[it * RWKV_UNROLL + u][0:1, ls]
            rhat = rtf_s[rows[u], ls] + zz[i][:, :LANES]
            yzero = zz[i][:, LANES:] + akv[i][CHUNK:]
            m_mat = blk * mn[i][:, :LANES] + eye * gam
            n_mat = blk * mn[i][:, LANES:]
            y_s[rows[u], ls] = _dot_f32(rhat, t_cur[p]) + yzero
            t_cur[p] = _dot_f32(m_mat, t_cur[p]) + n_mat
        for p in range(pairs):
            st_ref[p] = t_cur[p]
        return carry

    lax.fori_loop(0, nchunks // RWKV_UNROLL, step_body, 0)

    y = y_s[...]
    mean = _headsum(y, bdp) * (1.0 / n)
    d = y - mean
    var = _headsum(d * d, bdp) * (1.0 / n)
    yn = d * lax.rsqrt(var + RWKV_GN_EPS) * gnw_ref[...] + gnb_ref[...]
    return ((yn + bonus * v) * g).astype(BF16)


def _tail_body(x, o, p, wo_ref, lnm_ref, w1_ref, w2_ref, lnp_ref, wg_ref, wp_ref, lnf_ref):
    h = x + _dot(o, wo_ref[...])
    n1 = _rmsnorm(h, lnm_ref[...]).astype(BF16)
    yield
    hid = jnp.maximum(_dot(n1, w1_ref[...]), 0.0)
    yield
    h = h + _dot((hid * hid).astype(BF16), w2_ref[...])
    n2 = _rmsnorm(h, lnp_ref[...]).astype(BF16)
    yield
    gate = _sigmoid(_dot(n2, wg_ref[...]))
    h = h + gate * _dot(p.astype(BF16), wp_ref[...])
    return _rmsnorm(h, lnf_ref[...])


N_GLA, N_RWKV, N_TAIL = 5, 13, 8


def _interleave(*gens):
    results = [None] * len(gens)
    live = list(range(len(gens)))
    while live:
        for i in list(live):
            try:
                next(gens[i])
            except StopIteration as done:
                results[i] = done.value
                live.remove(i)
    return results


def _block_kernel(*refs, heads, dk, dv, steps_per_seq):
    xc_ref, xp_ref, pp_ref, ln_ref, wg_ref, wr_ref = refs[:6]
    i = 6
    gla_p, rwkv_p, tail_p = refs[i:i + N_GLA], refs[i + N_GLA:i + N_GLA + N_RWKV], \
        refs[i + N_GLA + N_RWKV:i + N_GLA + N_RWKV + N_TAIL]
    i += N_GLA + N_RWKV + N_TAIL
    out_ref = refs[i]
    o_s, gxb_ref, gst_ref, rxb_ref, rst_ref = refs[i + 1:i + 6]
    rwkv_s = refs[i + 6:]
    t = pl.program_id(0)

    @pl.when(t % steps_per_seq == 0)
    def _():
        gxb_ref[0:HALO, :] = jnp.zeros((HALO, gxb_ref.shape[1]), F32)
        gst_ref[...] = jnp.zeros_like(gst_ref)
        rxb_ref[0:HALO, :] = jnp.zeros((HALO, rxb_ref.shape[1]), F32)
        rst_ref[...] = jnp.zeros_like(rst_ref)

    @pl.when(t == 0)
    def _():
        o_s[...] = jnp.zeros_like(o_s)

    xn = _rmsnorm(xc_ref[0], ln_ref[...]).astype(BF16)
    zg = _dot(xn, wg_ref[...])
    zr = _dot(xn, wr_ref[...])
    out, o_gla, o_rwkv = _interleave(
        _tail_body(xp_ref[0], o_s[...], pp_ref[0], *tail_p),
        _gla_body(zg, *gla_p, gxb_ref, gst_ref, heads=heads, dk=dk, dv=dv),
        _rwkv_body(zr, *rwkv_p, rxb_ref, rst_ref, *rwkv_s))
    out_ref[0] = out
    o_s[...] = jnp.concatenate([o_gla, o_rwkv], axis=1)


def _block(x, p, ln, wg_in, wr_in, gla_params, rwkv_params, tail_params, *, heads, dk, dv):
    bsz, s, d = x.shape
    tl = TILE
    steps_per_seq = s // tl
    nsteps = bsz * steps_per_seq
    kw, vw = heads * dk, heads * dv
    nqkv = 2 * kw + vw
    width = rwkv_params[1].shape[1]
    nchunks = tl // CHUNK
    tri = _chunk_tri(tl)
    hid = jnp.arange(LANES) // RWKV_HEAD
    bdp = (hid[:, None] == hid[None, :]).astype(BF16)
    gla_params = (*gla_params, tri)
    rwkv_params = (*rwkv_params, tri, bdp)
    assert (len(gla_params), len(rwkv_params), len(tail_params)) == (N_GLA, N_RWKV, N_TAIL)

    def cur(t):
        tc = jnp.minimum(t, nsteps - 1)
        return (tc // steps_per_seq, tc % steps_per_seq, 0)

    def prev(t):
        tp = jnp.maximum(t - 1, 0)
        return (tp // steps_per_seq, tp % steps_per_seq, 0)

    whole = lambda a: pl.BlockSpec(a.shape, lambda t: (0,) * a.ndim, pipeline_mode=pl.Buffered(1))
    consts = (ln, wg_in, wr_in, *gla_params, *rwkv_params, *tail_params)
    kern = functools.partial(_block_kernel, heads=heads, dk=dk, dv=dv, steps_per_seq=steps_per_seq)
    return pl.pallas_call(
        kern,
        grid=(nsteps + 1,),
        in_specs=[pl.BlockSpec((1, tl, d), cur), pl.BlockSpec((1, tl, d), prev),
                  pl.BlockSpec((1, tl, p.shape[2]), prev)] + [whole(a) for a in consts],
        out_specs=pl.BlockSpec((1, tl, d), prev),
        out_shape=jax.ShapeDtypeStruct((bsz, s, d), F32),
        scratch_shapes=[
            pltpu.VMEM((tl, vw + width), BF16),
            pltpu.VMEM((tl + HALO, nqkv), F32),
            pltpu.VMEM((kw // LANES, 2 * dv, LANES), F32),
            pltpu.VMEM((tl + HALO, wr_in.shape[1]), F32),
            pltpu.VMEM((width // LANES, LANES, LANES), F32),
        ] + [pltpu.VMEM((tl, width), BF16)] * 7 + [
            pltpu.VMEM((tl, width), F32),
            pltpu.VMEM((nchunks, HALO, width), F32),
            pltpu.VMEM((tl, width), F32),
        ],
        compiler_params=pltpu.CompilerParams(
            dimension_semantics=("arbitrary",), vmem_limit_bytes=VMEM_LIMIT_BYTES),
        name="hybrid_block",
    )(x, x, p, *consts)


def _pad_cols(w, n):
    return jnp.pad(w, ((0, 0), (0, n - w.shape[1])))


def _place_rows(w, start, total):
    return jnp.pad(w, ((start, total - start - w.shape[0]), (0, 0)))


def _layer(h, p_i, ln_mix, w_in, conv_w, gk_w, gk_b, norm_g, mu, w0, w2, a0, a2, g2, k_k, k_a,
           r_k, gn_w, gn_b, w_out, ln_mlp, w_ff1, w_ff2, ln_ple, w_gate, w_proj, ln_next):
    row = lambda a: a.reshape(1, -1)

    gate_rank, kw = gk_w.shape
    dv = norm_g.shape[0]
    vw = (conv_w.shape[1] - 2 * kw)
    heads = vw // dv
    dk = kw // heads
    gla_cols = 2 * kw + 2 * vw + gate_rank
    gla_pad = _round_up(gla_cols, LANES)
    width = w0.shape[0]
    rwkv_cols = w_in.shape[1] - gla_cols
    rwkv_pad = _round_up(rwkv_cols, LANES)
    low_pad = rwkv_pad - 3 * width
    dr, ar = w2.shape[0], a2.shape[0]

    wg_in = _pad_cols(w_in[:, :gla_cols], gla_pad).astype(BF16)
    wr_in = _pad_cols(w_in[:, gla_cols:], rwkv_pad).astype(BF16)
    gla_params = (conv_w, _place_rows(gk_w, 0, gla_pad - (gla_cols - gate_rank)), row(gk_b), row(norm_g))
    rwkv_params = (
        _pad_cols(row(mu), rwkv_pad), row(w0), _place_rows(w2, 0, low_pad), row(a0),
        _place_rows(a2, dr, low_pad), _place_rows(g2, dr + ar, low_pad), row(k_k), row(k_a),
        row(r_k), row(gn_w), row(gn_b))
    tail_params = (w_out.astype(BF16), row(ln_mlp), w_ff1.astype(BF16), w_ff2.astype(BF16),
                   row(ln_ple), w_gate.astype(BF16), w_proj.astype(BF16), row(ln_next))
    return _block(h, p_i, row(ln_mix), wg_in, wr_in, gla_params, rwkv_params, tail_params,
                  heads=heads, dk=dk, dv=dv)


def kernel(x, p, ln_mix, w_in, gla_conv_w, gla_gk_w, gla_gk_b, gla_norm_g, rwkv_mu, rwkv_w0,
           rwkv_w2, rwkv_a0, rwkv_a2, rwkv_g2, rwkv_k_k, rwkv_k_a, rwkv_r_k, rwkv_gn_w,
           rwkv_gn_b, w_out, ln_mlp, w_ff1, w_ff2, ln_ple, w_ple_gate, w_ple_proj, ln_final):
    depth = w_in.shape[0]
    assert depth == 1, "the fused tail applies the final norm, so exactly one layer is supported"
    assert x.shape[1] % TILE == 0 and TILE % (CHUNK * RWKV_UNROLL) == 0
    i = 0
    return _layer(x, p[i], ln_mix[i], w_in[i], gla_conv_w[i], gla_gk_w[i], gla_gk_b[i],
                  gla_norm_g[i], rwkv_mu[i], rwkv_w0[i], rwkv_w2[i], rwkv_a0[i], rwkv_a2[i],
                  rwkv_g2[i], rwkv_k_k[i], rwkv_k_a[i], rwkv_r_k[i].reshape(-1), rwkv_gn_w[i],
                  rwkv_gn_b[i], w_out[i], ln_mlp[i], w_ff1[i], w_ff2[i], ln_ple[i],
                  w_ple_gate[i], w_ple_proj[i], ln_final)
```

```python
import functools

import jax
import jax.numpy as jnp
from jax import lax
from jax.experimental import pallas as pl
from jax.experimental.pallas import tpu as pltpu

F32 = jnp.float32
BF16 = jnp.bfloat16

NORM_EPS = 1e-6
GLA_GATE_NORMALIZER = 16.0
RWKV_GN_EPS = 64e-5
RWKV_HEAD = 64
CHUNK = 64
LANES = 128
HALO = 8
TILE = 256
RWKV_UNROLL = 4
MXU_WIDTH = 256
FF_UP_PIECES = 8
FF_DOWN_PIECES = 4
IN_PIECES = 2
VMEM_LIMIT_BYTES = 60 * 1024 * 1024


def _round_up(n, m):
    return (n + m - 1) // m * m


def _sigmoid(x):
    return 1.0 / (1.0 + jnp.exp(-x))


def _softplus(x):
    return jnp.maximum(x, 0.0) + jnp.log(1.0 + jnp.exp(-jnp.abs(x)))


def _dot(a, b):
    return jnp.dot(a, b, preferred_element_type=F32)


def _dot_nt(a, b):
    return lax.dot_general(a, b, (((1,), (1,)), ((), ())), preferred_element_type=F32)


def _dot_tn(a, b):
    return lax.dot_general(a, b, (((0,), (0,)), ((), ())), preferred_element_type=F32)


def _split2(x):
    hi = x.astype(BF16)
    lo = (x - hi.astype(F32)).astype(BF16)
    return hi, lo


def _dot_exact_lhs(a_bf16, x):
    hi, lo = _split2(x)
    return _dot(a_bf16, hi) + _dot(a_bf16, lo)


def _dot_f32(a, b_hi, b_lo):
    a_hi, a_lo = _split2(a)
    return _dot(a_hi, b_hi) + _dot(a_hi, b_lo) + _dot(a_lo, b_hi)


def _rmsnorm(x, g):
    ms = jnp.mean(x * x, axis=-1, keepdims=True)
    return x * lax.rsqrt(ms + NORM_EPS) * g


def _chunk_tri(tl):
    idx = jnp.arange(tl)
    same = idx[:, None] // CHUNK == idx[None, :] // CHUNK
    return ((idx[:, None] >= idx[None, :]) & same).astype(BF16)


def _gla_body(z_ref, cw_ref, gkw_ref, gkb_ref, ng_ref, tri_ref, xb_ref, st_ref, *, heads, dk, dv):
    kw = heads * dk
    vw = heads * dv
    nqkv = 2 * kw + vw
    taps = cw_ref.shape[0]
    pairs = kw // LANES
    tl = z_ref.shape[0]
    nchunks = tl // CHUNK
    xb_ref[HALO:HALO + tl, :] = z_ref[:, :nqkv]
    conv = jnp.zeros((tl, nqkv), F32)
    for t in range(taps):
        off = HALO - (taps - 1) + t
        conv = conv + cw_ref[t:t + 1, :] * xb_ref[off:off + tl, :]
    xb_ref[0:HALO, :] = xb_ref[tl:tl + HALO, :]
    qkv = conv * _sigmoid(conv)
    q = qkv[:, :kw] * (dk ** -0.5)
    k = qkv[:, kw:2 * kw]
    v = qkv[:, 2 * kw:nqkv].astype(BF16)
    low = z_ref[:, nqkv + vw:]

    gk_pre = _dot(low, gkw_ref[...]) + gkb_ref[...]
    yield
    gk = -_softplus(-gk_pre) * (1.0 / GLA_GATE_NORMALIZER)
    b = _dot_exact_lhs(tri_ref[...], gk)
    yield
    last_rows = [b[c * CHUNK + CHUNK - 1:(c + 1) * CHUNK, :] for c in range(nchunks)]
    b_last = jnp.concatenate([jnp.broadcast_to(row, (CHUNK, kw)) for row in last_rows], axis=0)
    b_ref = 0.5 * b_last
    qe = (q * jnp.exp(b - b_ref)).astype(BF16)
    head0 = lax.rem(lax.broadcasted_iota(jnp.int32, (tl, kw), 1), LANES) < dk
    ke = k * jnp.exp(b_ref - b)
    ke0 = jnp.where(head0, ke, 0.0).astype(BF16)
    ke1 = jnp.where(head0, 0.0, ke).astype(BF16)
    qb = (q * jnp.exp(b)).astype(BF16)
    kd = (k * jnp.exp(b_last - b)).astype(BF16)
    e_last = [jnp.exp(row) for row in last_rows]

    lane = lax.broadcasted_iota(jnp.int32, (CHUNK, LANES), 1)
    trow = lax.broadcasted_iota(jnp.int32, (CHUNK, LANES), 0)
    causal = jnp.where(lane < dk, lane, lane - dk) <= trow
    vzero = jnp.zeros((CHUNK, dv), BF16)
    srow = lax.broadcasted_iota(jnp.int32, (2 * dv, LANES), 0)
    scol = lax.broadcasted_iota(jnp.int32, (2 * dv, LANES), 1)
    st_mask = ((srow < dv) == (scol < dk)).astype(F32)

    items = [(c, p) for c in range(nchunks) for p in range(pairs)]
    rs = [slice(c * CHUNK, (c + 1) * CHUNK) for c in range(nchunks)]
    ls = [slice(p * LANES, (p + 1) * LANES) for p in range(pairs)]
    vs = [slice(2 * p * dv, 2 * (p + 1) * dv) for p in range(pairs)]
    s = [_dot_nt(qe[rs[c], ls[p]], jnp.concatenate([ke0[rs[c], ls[p]], ke1[rs[c], ls[p]]], axis=0))
         for c, p in items]
    s = [jnp.where(causal, s_, 0.0).astype(BF16) for s_ in s]
    v_bd = [jnp.concatenate(
        [jnp.concatenate([v[rs[c], 2 * p * dv:(2 * p + 1) * dv], vzero], axis=1),
         jnp.concatenate([vzero, v[rs[c], (2 * p + 1) * dv:(2 * p + 2) * dv]], axis=1)], axis=0)
        for c, p in items]
    o_intra = [_dot(s_, v_) for s_, v_ in zip(s, v_bd)]
    kv = [_dot_tn(v[rs[c], vs[p]], kd[rs[c], ls[p]]) for c, p in items]
    st = [st_ref[p] for p in range(pairs)]
    o_rows = []
    for c in range(nchunks):
        o_c = []
        for p in range(pairs):
            i = c * pairs + p
            o_c.append(o_intra[i] + _dot_nt(qb[rs[c], ls[p]], st[p].astype(BF16)))
            st[p] = st[p] * e_last[c][:, ls[p]] + st_mask * kv[i]
        o_rows.append(jnp.concatenate(o_c, axis=1))
    for p in range(pairs):
        st_ref[p] = st[p]
    o = jnp.concatenate(o_rows, axis=0)
    o_n = []
    for h in range(heads):
        o_h = o[:, h * dv:(h + 1) * dv]
        ms = jnp.mean(o_h * o_h, axis=-1, keepdims=True)
        o_n.append(o_h * lax.rsqrt(ms + NORM_EPS) * ng_ref[...])
    g = z_ref[:, nqkv:nqkv + vw]
    return (jnp.concatenate(o_n, axis=1) * (g * _sigmoid(g))).astype(BF16)


def _headsum(x, bd_pair):
    rows, width = x.shape
    groups = width // LANES
    xb = x.astype(BF16)
    stacked = jnp.concatenate([xb[:, i * LANES:(i + 1) * LANES] for i in range(groups)], axis=0)
    s = _dot(stacked, bd_pair)
    return jnp.concatenate([s[i * rows:(i + 1) * rows] for i in range(groups)], axis=1)


def _rwkv_body(z_ref, mu_ref, w0_ref, w2_ref, a0_ref, a2_ref, g2_ref, kk_ref, ka_ref,
               rk_ref, gnw_ref, gnb_ref, tri_ref, bdp_ref,
               xb_ref, st_ref, at_s, rt_s, bt_s, kt_s, bh_s, kh_s, v_s, rtf_s, gam_s, y_s):
    n = RWKV_HEAD
    width = w0_ref.shape[1]
    pairs = width // LANES
    tl = z_ref.shape[0]
    nchunks = tl // CHUNK
    levels = (CHUNK - 1).bit_length()

    z = z_ref[...]
    xb_ref[HALO:HALO + tl, :] = z
    zp = xb_ref[HALO - 1:HALO - 1 + tl, :]
    xb_ref[0:HALO, :] = xb_ref[tl:tl + HALO, :]
    zs = z + mu_ref[...] * (zp - z)
    r = zs[:, 0:width]
    k = zs[:, width:2 * width]
    v = zs[:, 2 * width:3 * width]
    low = zs[:, 3 * width:]

    w_pre = w0_ref[...] + _dot(jnp.tanh(low), w2_ref[...])
    logw = -jnp.exp(-_softplus(-w_pre) - 0.5)
    a_sig = _sigmoid(a0_ref[...] + _dot(low, a2_ref[...]))
    g = _dot(_sigmoid(low), g2_ref[...])
    yield

    bdp = bdp_ref[...]
    kk = k * kk_ref[...]
    kk_sq = _headsum(kk * kk, bdp)
    yield
    kk = kk / jnp.maximum(jnp.sqrt(kk_sq), 1e-12)
    kf = k * (1.0 + (a_sig - 1.0) * ka_ref[...])
    bv = kk * a_sig
    bonus = _headsum(r * kf * rk_ref[...], bdp)
    yield

    lg = _dot_exact_lhs(tri_ref[...], logw)
    yield
    last_rows = [lg[c * CHUNK + CHUNK - 1:(c + 1) * CHUNK, :] for c in range(nchunks)]
    lg_last = jnp.concatenate([jnp.broadcast_to(row, (CHUNK, width)) for row in last_rows], axis=0)
    for c, row in enumerate(last_rows):
        gam_s[c] = jnp.broadcast_to(jnp.exp(row), (HALO, width))
    e_inv = jnp.exp(-lg)
    e_rem = jnp.exp(lg_last - lg)
    rt = r * jnp.exp(lg)
    at_s[...] = (-kk * jnp.exp(lg - logw)).astype(BF16)
    rt_s[...] = rt.astype(BF16)
    rtf_s[...] = rt
    bt_s[...] = (bv * e_inv).astype(BF16)
    kt_s[...] = (kf * e_inv).astype(BF16)
    bh_s[...] = (bv * e_rem).astype(BF16)
    kh_s[...] = (kf * e_rem).astype(BF16)
    v_s[...] = v.astype(BF16)
    yield

    lane = lax.broadcasted_iota(jnp.int32, (CHUNK, LANES), 1)
    trow = lax.broadcasted_iota(jnp.int32, (CHUNK, LANES), 0)
    tcol = jnp.where(lane < n, lane, lane - n)
    incl, strict = tcol <= trow, tcol < trow
    eye2 = (tcol == trow).astype(F32)
    m0 = (lane < n).astype(BF16)
    m1 = (lane >= n).astype(BF16)
    m0w = jnp.concatenate([m0, m0], axis=1)
    m1w = jnp.concatenate([m1, m1], axis=1)
    prow = lax.broadcasted_iota(jnp.int32, (LANES, LANES), 0)
    pcol = lax.broadcasted_iota(jnp.int32, (LANES, LANES), 1)
    blk = ((prow < n) == (pcol < n)).astype(F32)
    eye = (prow == pcol).astype(F32)
    lsl = [slice(p * LANES, (p + 1) * LANES) for p in range(pairs)]

    def per_head(xb, wide=False):
        return jnp.concatenate([xb * (m0w if wide else m0), xb * (m1w if wide else m1)], axis=0)

    for it in range(nchunks // RWKV_UNROLL):
        items = [(u, p) for u in range(RWKV_UNROLL) for p in range(pairs)]
        rows = [pl.ds((it * RWKV_UNROLL + u) * CHUNK, CHUNK) for u in range(RWKV_UNROLL)]
        at_b = [at_s[rows[u], lsl[p]] for u, p in items]
        v_b = [v_s[rows[u], lsl[p]] for u, p in items]
        lhs = [jnp.concatenate([a_, rt_s[rows[u], lsl[p]]], axis=0)
               for a_, (u, p) in zip(at_b, items)]
        sb = [_dot_nt(l_, per_head(bt_s[rows[u], lsl[p]])) for l_, (u, p) in zip(lhs, items)]
        sk = [_dot_nt(l_, per_head(kt_s[rows[u], lsl[p]])) for l_, (u, p) in zip(lhs, items)]
        yield
        a_ab = [jnp.where(strict, s_[:CHUNK], 0.0).astype(BF16) for s_ in sb]
        a_rb = [jnp.where(incl, s_[CHUNK:], 0.0).astype(BF16) for s_ in sb]
        a_kk = [jnp.concatenate([jnp.where(strict, s_[:CHUNK], 0.0),
                                 jnp.where(incl, s_[CHUNK:], 0.0)], axis=0).astype(BF16) for s_ in sk]
        akv = [_dot(a_, per_head(v_)) for a_, v_ in zip(a_kk, v_b)]
        yield
        pw = a_ab
        inv = [eye2 + p_.astype(F32) for p_ in pw]
        for lvl in range(1, levels):
            pw = [_dot(p_, per_head(p_)).astype(BF16) for p_ in pw]
            inv = [q_ + _dot(p_, per_head(q_.astype(BF16))) for q_, p_ in zip(inv, pw)]
            yield
        x0 = [jnp.concatenate([a_, k_[:CHUNK].astype(BF16)], axis=1) for a_, k_ in zip(at_b, akv)]
        xb = [_dot(q_.astype(BF16), per_head(x_, wide=True)).astype(BF16)
              for q_, x_ in zip(inv, x0)]
        zz = [_dot(a_, per_head(x_, wide=True)) for a_, x_ in zip(a_rb, xb)]
        mn = []
        for i, (u, p) in enumerate(items):
            lb = jnp.concatenate([bh_s[rows[u], lsl[p]], kh_s[rows[u], lsl[p]]], axis=0)
            rb = jnp.concatenate(
                [xb[i], jnp.concatenate([jnp.zeros_like(v_b[i]), v_b[i]], axis=1)], axis=0)
            mn.append(_dot_tn(lb, rb))
        yield
        t_cur = [st_ref[p] for p in range(pairs)]
        for i, (u, p) in enumerate(items):
            ls = lsl[p]
            gam = gam_s[it * RWKV_UNROLL + u][0:1, ls]
            rhat = rtf_s[rows[u], ls] + zz[i][:, :LANES]
            yzero = zz[i][:, LANES:] + akv[i][CHUNK:]
            m_mat = blk * mn[i][:, :LANES] + eye * gam
            n_mat = blk * mn[i][:, LANES:]
            t_hi, t_lo = _split2(t_cur[p])
            y_s[rows[u], ls] = _dot_f32(rhat, t_hi, t_lo) + yzero
            t_cur[p] = _dot_f32(m_mat, t_hi, t_lo) + n_mat
        for p in range(pairs):
            st_ref[p] = t_cur[p]
        yield

    y = y_s[...]
    mean = _headsum(y, bdp) * (1.0 / n)
    d = y - mean
    var = _headsum(d * d, bdp) * (1.0 / n)
    yn = d * lax.rsqrt(var + RWKV_GN_EPS) * gnw_ref[...] + gnb_ref[...]
    return ((yn + bonus * v) * g).astype(BF16)


def _tail_body(x, o, p, wo_ref, lnm_ref, w1_ref, w2_ref, lnp_ref, wg_ref, wp_ref, lnf_ref):
    h = x + _dot(o, wo_ref[...])
    n1 = _rmsnorm(h, lnm_ref[...]).astype(BF16)
    yield
    d_ff, d = w1_ref.shape[1], w2_ref.shape[1]
    hid = []
    for cs in _lane_blocks(d_ff, FF_UP_PIECES):
        a = jnp.maximum(_dot(n1, w1_ref[:, cs]), 0.0)
        hid.append((a * a).astype(BF16))
        yield
    hid = jnp.concatenate(hid, axis=1)
    down = []
    for cs in _lane_blocks(d, FF_DOWN_PIECES):
        down.append(_dot(hid, w2_ref[:, cs]))
        yield
    h = h + jnp.concatenate(down, axis=1)
    n2 = _rmsnorm(h, lnp_ref[...]).astype(BF16)
    gate = _sigmoid(_dot(n2, wg_ref[...]))
    h = h + gate * _dot(p.astype(BF16), wp_ref[...])
    return _rmsnorm(h, lnf_ref[...])


N_GLA, N_RWKV, N_TAIL = 5, 13, 8


def _lane_blocks(n, k):
    groups = n // MXU_WIDTH
    cuts = [(groups * i // k) * MXU_WIDTH for i in range(k)] + [n]
    return [slice(a, b) for a, b in zip(cuts[:-1], cuts[1:])]


def _in_proj_body(x, ln_ref, wg_ref, wr_ref, zg_ref, zr_ref):
    xn = _rmsnorm(x, ln_ref[...]).astype(BF16)
    for w_ref, z_ref in ((wg_ref, zg_ref), (wr_ref, zr_ref)):
        for cs in _lane_blocks(w_ref.shape[1], IN_PIECES):
            yield
            z_ref[:, cs] = _dot(xn, w_ref[:, cs])


def _interleave(*gens):
    results = [None] * len(gens)
    live = list(range(len(gens)))
    while live:
        for i in list(live):
            try:
                next(gens[i])
            except StopIteration as done:
                results[i] = done.value
                live.remove(i)
    return results


def _block_kernel(*refs, heads, dk, dv, steps_per_seq):
    xn_ref, xp_ref, pp_ref, ln_ref, wg_ref, wr_ref = refs[:6]
    i = 6
    gla_p, rwkv_p, tail_p = refs[i:i + N_GLA], refs[i + N_GLA:i + N_GLA + N_RWKV], \
        refs[i + N_GLA + N_RWKV:i + N_GLA + N_RWKV + N_TAIL]
    i += N_GLA + N_RWKV + N_TAIL
    out_ref = refs[i]
    o_s, zg_s, zr_s, gxb_ref, gst_ref, rxb_ref, rst_ref = refs[i + 1:i + 8]
    rwkv_s = refs[i + 8:]
    t = pl.program_id(0)
    cur = lax.rem(t, 2)

    @pl.when(t % steps_per_seq == 0)
    def _():
        gxb_ref[0:HALO, :] = jnp.zeros((HALO, gxb_ref.shape[1]), F32)
        gst_ref[...] = jnp.zeros_like(gst_ref)
        rxb_ref[0:HALO, :] = jnp.zeros((HALO, rxb_ref.shape[1]), F32)
        rst_ref[...] = jnp.zeros_like(rst_ref)

    @pl.when(t == 0)
    def _():
        o_s[...] = jnp.zeros_like(o_s)
        for _ in _in_proj_body(xp_ref[0], ln_ref, wg_ref, wr_ref, zg_s.at[0], zr_s.at[0]):
            pass

    out, _, o_rwkv, o_gla = _interleave(
        _tail_body(xp_ref[0], o_s[...], pp_ref[0], *tail_p),
        _in_proj_body(xn_ref[0], ln_ref, wg_ref, wr_ref, zg_s.at[1 - cur], zr_s.at[1 - cur]),
        _rwkv_body(zr_s.at[cur], *rwkv_p, rxb_ref, rst_ref, *rwkv_s),
        _gla_body(zg_s.at[cur], *gla_p, gxb_ref, gst_ref, heads=heads, dk=dk, dv=dv))
    out_ref[0] = out
    o_s[...] = jnp.concatenate([o_gla, o_rwkv], axis=1)


def _block(x, p, ln, wg_in, wr_in, gla_params, rwkv_params, tail_params, *, heads, dk, dv):
    bsz, s, d = x.shape
    tl = TILE
    steps_per_seq = s // tl
    nsteps = bsz * steps_per_seq
    kw, vw = heads * dk, heads * dv
    nqkv = 2 * kw + vw
    width = rwkv_params[1].shape[1]
    nchunks = tl // CHUNK
    tri = _chunk_tri(tl)
    hid = jnp.arange(LANES) // RWKV_HEAD
    bdp = (hid[:, None] == hid[None, :]).astype(BF16)
    gla_params = (*gla_params, tri)
    rwkv_params = (*rwkv_params, tri, bdp)
    assert (len(gla_params), len(rwkv_params), len(tail_params)) == (N_GLA, N_RWKV, N_TAIL)

    def nxt(t):
        tn = jnp.minimum(t + 1, nsteps - 1)
        return (tn // steps_per_seq, tn % steps_per_seq, 0)

    def prev(t):
        tp = jnp.maximum(t - 1, 0)
        return (tp // steps_per_seq, tp % steps_per_seq, 0)

    whole = lambda a: pl.BlockSpec(a.shape, lambda t: (0,) * a.ndim, pipeline_mode=pl.Buffered(1))
    consts = (ln, wg_in, wr_in, *gla_params, *rwkv_params, *tail_params)
    kern = functools.partial(_block_kernel, heads=heads, dk=dk, dv=dv, steps_per_seq=steps_per_seq)
    return pl.pallas_call(
        kern,
        grid=(nsteps + 1,),
        in_specs=[pl.BlockSpec((1, tl, d), nxt), pl.BlockSpec((1, tl, d), prev),
                  pl.BlockSpec((1, tl, p.shape[2]), prev)] + [whole(a) for a in consts],
        out_specs=pl.BlockSpec((1, tl, d), prev),
        out_shape=jax.ShapeDtypeStruct((bsz, s, d), F32),
        scratch_shapes=[
            pltpu.VMEM((tl, vw + width), BF16),
            pltpu.VMEM((2, tl, wg_in.shape[1]), F32),
            pltpu.VMEM((2, tl, wr_in.shape[1]), F32),
            pltpu.VMEM((tl + HALO, nqkv), F32),
            pltpu.VMEM((kw // LANES, 2 * dv, LANES), F32),
            pltpu.VMEM((tl + HALO, wr_in.shape[1]), F32),
            pltpu.VMEM((width // LANES, LANES, LANES), F32),
        ] + [pltpu.VMEM((tl, width), BF16)] * 7 + [
            pltpu.VMEM((tl, width), F32),
            pltpu.VMEM((nchunks, HALO, width), F32),
            pltpu.VMEM((tl, width), F32),
        ],
        compiler_params=pltpu.CompilerParams(
            dimension_semantics=("arbitrary",), vmem_limit_bytes=VMEM_LIMIT_BYTES),
        name="hybrid_block",
    )(x, x, p, *consts)


def _pad_cols(w, n):
    return jnp.pad(w, ((0, 0), (0, n - w.shape[1])))


def _place_rows(w, start, total):
    return jnp.pad(w, ((start, total - start - w.shape[0]), (0, 0)))


def _layer(h, p_i, ln_mix, w_in, conv_w, gk_w, gk_b, norm_g, mu, w0, w2, a0, a2, g2, k_k, k_a,
           r_k, gn_w, gn_b, w_out, ln_mlp, w_ff1, w_ff2, ln_ple, w_gate, w_proj, ln_next):
    row = lambda a: a.reshape(1, -1)

    gate_rank, kw = gk_w.shape
    dv = norm_g.shape[0]
    vw = (conv_w.shape[1] - 2 * kw)
    heads = vw // dv
    dk = kw // heads
    gla_cols = 2 * kw + 2 * vw + gate_rank
    gla_pad = _round_up(gla_cols, LANES)
    width = w0.shape[0]
    rwkv_cols = w_in.shape[1] - gla_cols
    rwkv_pad = _round_up(rwkv_cols, LANES)
    low_pad = rwkv_pad - 3 * width
    dr, ar = w2.shape[0], a2.shape[0]

    wg_in = _pad_cols(w_in[:, :gla_cols], gla_pad).astype(BF16)
    wr_in = _pad_cols(w_in[:, gla_cols:], rwkv_pad).astype(BF16)
    gla_params = (conv_w, _place_rows(gk_w, 0, gla_pad - (gla_cols - gate_rank)), row(gk_b), row(norm_g))
    rwkv_params = (
        _pad_cols(row(mu), rwkv_pad), row(w0), _place_rows(w2, 0, low_pad), row(a0),
        _place_rows(a2, dr, low_pad), _place_rows(g2, dr + ar, low_pad), row(k_k), row(k_a),
        row(r_k), row(gn_w), row(gn_b))
    tail_params = (w_out.astype(BF16), row(ln_mlp), w_ff1.astype(BF16), w_ff2.astype(BF16),
                   row(ln_ple), w_gate.astype(BF16), w_proj.astype(BF16), row(ln_next))
    return _block(h, p_i, row(ln_mix), wg_in, wr_in, gla_params, rwkv_params, tail_params,
                  heads=heads, dk=dk, dv=dv)


def kernel(x, p, ln_mix, w_in, gla_conv_w, gla_gk_w, gla_gk_b, gla_norm_g, rwkv_mu, rwkv_w0,
           rwkv_w2, rwkv_a0, rwkv_a2, rwkv_g2, rwkv_k_k, rwkv_k_a, rwkv_r_k, rwkv_gn_w,
           rwkv_gn_b, w_out, ln_mlp, w_ff1, w_ff2, ln_ple, w_ple_gate, w_ple_proj, ln_final):
    depth = w_in.shape[0]
    assert depth == 1, "the fused tail applies the final norm, so exactly one layer is supported"
    assert x.shape[1] % TILE == 0 and TILE % (CHUNK * RWKV_UNROLL) == 0
    i = 0
    return _layer(x, p[i], ln_mix[i], w_in[i], gla_conv_w[i], gla_gk_w[i], gla_gk_b[i],
                  gla_norm_g[i], rwkv_mu[i], rwkv_w0[i], rwkv_w2[i], rwkv_a0[i], rwkv_a2[i],
                  rwkv_g2[i], rwkv_k_k[i], rwkv_k_a[i], rwkv_r_k[i].reshape(-1), rwkv_gn_w[i],
                  rwkv_gn_b[i], w_out[i], ln_mlp[i], w_ff1[i], w_ff2[i], ln_ple[i],
                  w_ple_gate[i], w_ple_proj[i], ln_final)
```

```python
import functools

import jax
import jax.numpy as jnp
from jax import lax
from jax.experimental import pallas as pl
from jax.experimental.pallas import tpu as pltpu

F32 = jnp.float32
BF16 = jnp.bfloat16

NORM_EPS = 1e-6
GLA_GATE_NORMALIZER = 16.0
RWKV_GN_EPS = 64e-5
RWKV_HEAD = 64
CHUNK = 64
LANES = 128
HALO = 8
TILE = 256
RWKV_UNROLL = 4
MXU_WIDTH = 256
FF_UP_PIECES = 8
FF_DOWN_PIECES = 4
IN_PIECES = 2
STAGE_ORDER = "TITTIRG" "TTIR" "TTRG" "TTIR" "TIRG" "TRTRTRTR"
VMEM_LIMIT_BYTES = 60 * 1024 * 1024


def _round_up(n, m):
    return (n + m - 1) // m * m


def _sigmoid(x):
    return 1.0 / (1.0 + jnp.exp(-x))


def _softplus(x):
    return jnp.maximum(x, 0.0) + jnp.log(1.0 + jnp.exp(-jnp.abs(x)))


def _dot(a, b):
    return jnp.dot(a, b, preferred_element_type=F32)


def _dot_nt(a, b):
    return lax.dot_general(a, b, (((1,), (1,)), ((), ())), preferred_element_type=F32)


def _dot_tn(a, b):
    return lax.dot_general(a, b, (((0,), (0,)), ((), ())), preferred_element_type=F32)


def _split2(x):
    hi = x.astype(BF16)
    lo = (x - hi.astype(F32)).astype(BF16)
    return hi, lo


def _dot_exact_lhs(a_bf16, x):
    hi, lo = _split2(x)
    return _dot(a_bf16, hi) + _dot(a_bf16, lo)


def _dot_f32(a, b_hi, b_lo):
    a_hi, a_lo = _split2(a)
    return _dot(a_hi, b_hi) + _dot(a_hi, b_lo) + _dot(a_lo, b_hi)


def _rmsnorm(x, g):
    ms = jnp.mean(x * x, axis=-1, keepdims=True)
    return x * lax.rsqrt(ms + NORM_EPS) * g


def _chunk_tri(tl):
    idx = jnp.arange(tl)
    same = idx[:, None] // CHUNK == idx[None, :] // CHUNK
    return ((idx[:, None] >= idx[None, :]) & same).astype(BF16)


def _gla_body(z_ref, cw_ref, gkw_ref, gkb_ref, ng_ref, tri_ref, xb_ref, st_ref, *, heads, dk, dv):
    kw = heads * dk
    vw = heads * dv
    nqkv = 2 * kw + vw
    taps = cw_ref.shape[0]
    pairs = kw // LANES
    tl = z_ref.shape[0]
    nchunks = tl // CHUNK
    xb_ref[HALO:HALO + tl, :] = z_ref[:, :nqkv]
    conv = jnp.zeros((tl, nqkv), F32)
    for t in range(taps):
        off = HALO - (taps - 1) + t
        conv = conv + cw_ref[t:t + 1, :] * xb_ref[off:off + tl, :]
    xb_ref[0:HALO, :] = xb_ref[tl:tl + HALO, :]
    qkv = conv * _sigmoid(conv)
    q = qkv[:, :kw] * (dk ** -0.5)
    k = qkv[:, kw:2 * kw]
    v = qkv[:, 2 * kw:nqkv].astype(BF16)
    low = z_ref[:, nqkv + vw:]

    gk_pre = _dot(low.astype(BF16), gkw_ref[...]) + gkb_ref[...]
    yield
    gk = -_softplus(-gk_pre) * (1.0 / GLA_GATE_NORMALIZER)
    b = _dot_exact_lhs(tri_ref[...], gk)
    yield
    last_rows = [b[c * CHUNK + CHUNK - 1:(c + 1) * CHUNK, :] for c in range(nchunks)]
    b_last = jnp.concatenate([jnp.broadcast_to(row, (CHUNK, kw)) for row in last_rows], axis=0)
    b_ref = 0.5 * b_last
    qe = (q * jnp.exp(b - b_ref)).astype(BF16)
    head0 = lax.rem(lax.broadcasted_iota(jnp.int32, (tl, kw), 1), LANES) < dk
    ke = k * jnp.exp(b_ref - b)
    ke0 = jnp.where(head0, ke, 0.0).astype(BF16)
    ke1 = jnp.where(head0, 0.0, ke).astype(BF16)
    qb = (q * jnp.exp(b)).astype(BF16)
    kd = (k * jnp.exp(b_last - b)).astype(BF16)
    e_last = [jnp.exp(row) for row in last_rows]

    lane = lax.broadcasted_iota(jnp.int32, (CHUNK, LANES), 1)
    trow = lax.broadcasted_iota(jnp.int32, (CHUNK, LANES), 0)
    causal = jnp.where(lane < dk, lane, lane - dk) <= trow
    vzero = jnp.zeros((CHUNK, dv), BF16)
    srow = lax.broadcasted_iota(jnp.int32, (2 * dv, LANES), 0)
    scol = lax.broadcasted_iota(jnp.int32, (2 * dv, LANES), 1)
    st_mask = ((srow < dv) == (scol < dk)).astype(F32)

    items = [(c, p) for c in range(nchunks) for p in range(pairs)]
    rs = [slice(c * CHUNK, (c + 1) * CHUNK) for c in range(nchunks)]
    ls = [slice(p * LANES, (p + 1) * LANES) for p in range(pairs)]
    vs = [slice(2 * p * dv, 2 * (p + 1) * dv) for p in range(pairs)]
    s = [_dot_nt(qe[rs[c], ls[p]], jnp.concatenate([ke0[rs[c], ls[p]], ke1[rs[c], ls[p]]], axis=0))
         for c, p in items]
    s = [jnp.where(causal, s_, 0.0).astype(BF16) for s_ in s]
    v_bd = [jnp.concatenate(
        [jnp.concatenate([v[rs[c], 2 * p * dv:(2 * p + 1) * dv], vzero], axis=1),
         jnp.concatenate([vzero, v[rs[c], (2 * p + 1) * dv:(2 * p + 2) * dv]], axis=1)], axis=0)
        for c, p in items]
    o_intra = [_dot(s_, v_) for s_, v_ in zip(s, v_bd)]
    kv = [_dot_tn(v[rs[c], vs[p]], kd[rs[c], ls[p]]) for c, p in items]
    st = [st_ref[p] for p in range(pairs)]
    o_rows = []
    for c in range(nchunks):
        o_c = []
        for p in range(pairs):
            i = c * pairs + p
            o_c.append(o_intra[i] + _dot_nt(qb[rs[c], ls[p]], st[p].astype(BF16)))
            st[p] = st[p] * e_last[c][:, ls[p]] + st_mask * kv[i]
        o_rows.append(jnp.concatenate(o_c, axis=1))
    for p in range(pairs):
        st_ref[p] = st[p]
    o = jnp.concatenate(o_rows, axis=0)
    o_n = []
    for h in range(heads):
        o_h = o[:, h * dv:(h + 1) * dv]
        ms = jnp.mean(o_h * o_h, axis=-1, keepdims=True)
        o_n.append(o_h * lax.rsqrt(ms + NORM_EPS) * ng_ref[...])
    g = z_ref[:, nqkv:nqkv + vw]
    return (jnp.concatenate(o_n, axis=1) * (g * _sigmoid(g))).astype(BF16)


def _headsum(x, bd_pair):
    rows, width = x.shape
    groups = width // LANES
    xb = x.astype(BF16)
    stacked = jnp.concatenate([xb[:, i * LANES:(i + 1) * LANES] for i in range(groups)], axis=0)
    s = _dot(stacked, bd_pair)
    return jnp.concatenate([s[i * rows:(i + 1) * rows] for i in range(groups)], axis=1)


def _rwkv_body(z_ref, mu_ref, w0_ref, w2_ref, a0_ref, a2_ref, g2_ref, kk_ref, ka_ref,
               rk_ref, gnw_ref, gnb_ref, tri_ref, bdp_ref,
               xb_ref, st_ref, at_s, rt_s, bt_s, kt_s, bh_s, kh_s, v_s, rtf_s, gam_s, y_s):
    n = RWKV_HEAD
    width = w0_ref.shape[1]
    pairs = width // LANES
    tl = z_ref.shape[0]
    nchunks = tl // CHUNK
    levels = (CHUNK - 1).bit_length()

    z = z_ref[...]
    xb_ref[HALO:HALO + tl, :] = z
    zp = xb_ref[HALO - 1:HALO - 1 + tl, :]
    xb_ref[0:HALO, :] = xb_ref[tl:tl + HALO, :]
    zs = z + mu_ref[...] * (zp - z)
    r = zs[:, 0:width]
    k = zs[:, width:2 * width]
    v = zs[:, 2 * width:3 * width]
    low = zs[:, 3 * width:]

    w_pre = w0_ref[...] + _dot(jnp.tanh(low).astype(BF16), w2_ref[...])
    logw = -jnp.exp(-_softplus(-w_pre) - 0.5)
    a_sig = _sigmoid(a0_ref[...] + _dot(low.astype(BF16), a2_ref[...]))
    g = _dot(_sigmoid(low).astype(BF16), g2_ref[...])
    yield

    bdp = bdp_ref[...]
    kk = k * kk_ref[...]
    kk_sq = _headsum(kk * kk, bdp)
    yield
    kk = kk / jnp.maximum(jnp.sqrt(kk_sq), 1e-12)
    kf = k * (1.0 + (a_sig - 1.0) * ka_ref[...])
    bv = kk * a_sig
    bonus = _headsum(r * kf * rk_ref[...], bdp)
    yield

    lg = _dot_exact_lhs(tri_ref[...], logw)
    yield
    last_rows = [lg[c * CHUNK + CHUNK - 1:(c + 1) * CHUNK, :] for c in range(nchunks)]
    lg_last = jnp.concatenate([jnp.broadcast_to(row, (CHUNK, width)) for row in last_rows], axis=0)
    for c, row in enumerate(last_rows):
        gam_s[c] = jnp.broadcast_to(jnp.exp(row), (HALO, width))
    e_inv = jnp.exp(-lg)
    e_rem = jnp.exp(lg_last - lg)
    rt = r * jnp.exp(lg)
    at_s[...] = (-kk * jnp.exp(lg - logw)).astype(BF16)
    rt_s[...] = rt.astype(BF16)
    rtf_s[...] = rt
    bt_s[...] = (bv * e_inv).astype(BF16)
    kt_s[...] = (kf * e_inv).astype(BF16)
    bh_s[...] = (bv * e_rem).astype(BF16)
    kh_s[...] = (kf * e_rem).astype(BF16)
    v_s[...] = v.astype(BF16)
    yield

    lane = lax.broadcasted_iota(jnp.int32, (CHUNK, LANES), 1)
    trow = lax.broadcasted_iota(jnp.int32, (CHUNK, LANES), 0)
    tcol = jnp.where(lane < n, lane, lane - n)
    incl, strict = tcol <= trow, tcol < trow
    eye2 = (tcol == trow).astype(F32)
    m0 = (lane < n).astype(BF16)
    m1 = (lane >= n).astype(BF16)
    m0w = jnp.concatenate([m0, m0], axis=1)
    m1w = jnp.concatenate([m1, m1], axis=1)
    prow = lax.broadcasted_iota(jnp.int32, (LANES, LANES), 0)
    pcol = lax.broadcasted_iota(jnp.int32, (LANES, LANES), 1)
    blk = ((prow < n) == (pcol < n)).astype(F32)
    eye = (prow == pcol).astype(F32)
    lsl = [slice(p * LANES, (p + 1) * LANES) for p in range(pairs)]

    def per_head(xb, wide=False):
        return jnp.concatenate([xb * (m0w if wide else m0), xb * (m1w if wide else m1)], axis=0)

    for it in range(nchunks // RWKV_UNROLL):
        items = [(u, p) for u in range(RWKV_UNROLL) for p in range(pairs)]
        rows = [pl.ds((it * RWKV_UNROLL + u) * CHUNK, CHUNK) for u in range(RWKV_UNROLL)]
        at_b = [at_s[rows[u], lsl[p]] for u, p in items]
        v_b = [v_s[rows[u], lsl[p]] for u, p in items]
        lhs = [jnp.concatenate([a_, rt_s[rows[u], lsl[p]]], axis=0)
               for a_, (u, p) in zip(at_b, items)]
        sb = [_dot_nt(l_, per_head(bt_s[rows[u], lsl[p]])) for l_, (u, p) in zip(lhs, items)]
        sk = [_dot_nt(l_, per_head(kt_s[rows[u], lsl[p]])) for l_, (u, p) in zip(lhs, items)]
        yield
        a_ab = [jnp.where(strict, s_[:CHUNK], 0.0).astype(BF16) for s_ in sb]
        a_rb = [jnp.where(incl, s_[CHUNK:], 0.0).astype(BF16) for s_ in sb]
        a_kk = [jnp.concatenate([jnp.where(strict, s_[:CHUNK], 0.0),
                                 jnp.where(incl, s_[CHUNK:], 0.0)], axis=0).astype(BF16) for s_ in sk]
        akv = [_dot(a_, per_head(v_)) for a_, v_ in zip(a_kk, v_b)]
        yield
        pw = a_ab
        inv = [eye2 + p_.astype(F32) for p_ in pw]
        for lvl in range(1, levels):
            pw = [_dot(p_, per_head(p_)).astype(BF16) for p_ in pw]
            inv = [q_ + _dot(p_, per_head(q_.astype(BF16))) for q_, p_ in zip(inv, pw)]
            yield
        x0 = [jnp.concatenate([a_, k_[:CHUNK].astype(BF16)], axis=1) for a_, k_ in zip(at_b, akv)]
        xb = [_dot(q_.astype(BF16), per_head(x_, wide=True)).astype(BF16)
              for q_, x_ in zip(inv, x0)]
        zz = [_dot(a_, per_head(x_, wide=True)) for a_, x_ in zip(a_rb, xb)]
        mn = []
        for i, (u, p) in enumerate(items):
            lb = jnp.concatenate([bh_s[rows[u], lsl[p]], kh_s[rows[u], lsl[p]]], axis=0)
            rb = jnp.concatenate(
                [xb[i], jnp.concatenate([jnp.zeros_like(v_b[i]), v_b[i]], axis=1)], axis=0)
            mn.append(_dot_tn(lb, rb))
        yield
        t_cur = [st_ref[p] for p in range(pairs)]
        for i, (u, p) in enumerate(items):
            ls = lsl[p]
            gam = gam_s[it * RWKV_UNROLL + u][0:1, ls]
            rhat = rtf_s[rows[u], ls] + zz[i][:, :LANES]
            yzero = zz[i][:, LANES:] + akv[i][CHUNK:]
            m_mat = blk * mn[i][:, :LANES] + eye * gam
            n_mat = blk * mn[i][:, LANES:]
            t_hi, t_lo = _split2(t_cur[p])
            y_s[rows[u], ls] = _dot_f32(rhat, t_hi, t_lo) + yzero
            t_cur[p] = _dot_f32(m_mat, t_hi, t_lo) + n_mat
        for p in range(pairs):
            st_ref[p] = t_cur[p]
        yield

    y = y_s[...]
    mean = _headsum(y, bdp) * (1.0 / n)
    d = y - mean
    var = _headsum(d * d, bdp) * (1.0 / n)
    yn = d * lax.rsqrt(var + RWKV_GN_EPS) * gnw_ref[...] + gnb_ref[...]
    return ((yn + bonus * v) * g).astype(BF16)


def _tail_body(x, o, p, wo_ref, lnm_ref, w1_ref, w2_ref, lnp_ref, wg_ref, wp_ref, lnf_ref):
    h = x + _dot(o, wo_ref[...])
    n1 = _rmsnorm(h, lnm_ref[...]).astype(BF16)
    yield
    d_ff, d = w1_ref.shape[1], w2_ref.shape[1]
    hid = []
    for cs in _lane_blocks(d_ff, FF_UP_PIECES):
        a = jnp.maximum(_dot(n1, w1_ref[:, cs]).astype(BF16), 0.0)
        hid.append(a * a)
        yield
    hid = jnp.concatenate(hid, axis=1)
    down = []
    for cs in _lane_blocks(d, FF_DOWN_PIECES):
        down.append(_dot(hid, w2_ref[:, cs]))
        yield
    h = h + jnp.concatenate(down, axis=1)
    n2 = _rmsnorm(h, lnp_ref[...]).astype(BF16)
    gate = _sigmoid(_dot(n2, wg_ref[...]))
    h = h + gate * _dot(p.astype(BF16), wp_ref[...])
    return _rmsnorm(h, lnf_ref[...])


N_GLA, N_RWKV, N_TAIL = 5, 13, 8


def _lane_blocks(n, k):
    groups = n // MXU_WIDTH
    cuts = [(groups * i // k) * MXU_WIDTH for i in range(k)] + [n]
    return [slice(a, b) for a, b in zip(cuts[:-1], cuts[1:])]


def _in_proj_body(x, ln_ref, wg_ref, wr_ref, zg_ref, zr_ref):
    xn = _rmsnorm(x, ln_ref[...]).astype(BF16)
    for w_ref, z_ref in ((wg_ref, zg_ref), (wr_ref, zr_ref)):
        for cs in _lane_blocks(w_ref.shape[1], IN_PIECES):
            yield
            z_ref[:, cs] = _dot(xn, w_ref[:, cs])


def _interleave(gens, order):
    results = {}
    live = dict(gens)

    def advance(name):
        if name in live:
            try:
                next(live[name])
            except StopIteration as done:
                results[name] = done.value
                del live[name]

    for name in order:
        advance(name)
    while live:
        for name in list(live):
            advance(name)
    return results


def _block_kernel(*refs, heads, dk, dv, steps_per_seq):
    xn_ref, xp_ref, pp_ref, ln_ref, wg_ref, wr_ref = refs[:6]
    i = 6
    gla_p, rwkv_p, tail_p = refs[i:i + N_GLA], refs[i + N_GLA:i + N_GLA + N_RWKV], \
        refs[i + N_GLA + N_RWKV:i + N_GLA + N_RWKV + N_TAIL]
    i += N_GLA + N_RWKV + N_TAIL
    out_ref = refs[i]
    o_s, zg_s, zr_s, gxb_ref, gst_ref, rxb_ref, rst_ref = refs[i + 1:i + 8]
    rwkv_s = refs[i + 8:]
    t = pl.program_id(0)
    cur = lax.rem(t, 2)

    @pl.when(t % steps_per_seq == 0)
    def _():
        gxb_ref[0:HALO, :] = jnp.zeros((HALO, gxb_ref.shape[1]), F32)
        gst_ref[...] = jnp.zeros_like(gst_ref)
        rxb_ref[0:HALO, :] = jnp.zeros((HALO, rxb_ref.shape[1]), F32)
        rst_ref[...] = jnp.zeros_like(rst_ref)

    @pl.when(t == 0)
    def _():
        o_s[...] = jnp.zeros_like(o_s)
        for _ in _in_proj_body(xp_ref[0], ln_ref, wg_ref, wr_ref, zg_s.at[0], zr_s.at[0]):
            pass

    res = _interleave({
        "T": _tail_body(xp_ref[0], o_s[...], pp_ref[0], *tail_p),
        "I": _in_proj_body(xn_ref[0], ln_ref, wg_ref, wr_ref, zg_s.at[1 - cur], zr_s.at[1 - cur]),
        "R": _rwkv_body(zr_s.at[cur], *rwkv_p, rxb_ref, rst_ref, *rwkv_s),
        "G": _gla_body(zg_s.at[cur], *gla_p, gxb_ref, gst_ref, heads=heads, dk=dk, dv=dv),
    }, STAGE_ORDER)
    out_ref[0] = res["T"]
    o_s[...] = jnp.concatenate([res["G"], res["R"]], axis=1)


def _block(x, p, ln, wg_in, wr_in, gla_params, rwkv_params, tail_params, *, heads, dk, dv):
    bsz, s, d = x.shape
    tl = TILE
    steps_per_seq = s // tl
    nsteps = bsz * steps_per_seq
    kw, vw = heads * dk, heads * dv
    nqkv = 2 * kw + vw
    width = rwkv_params[1].shape[1]
    nchunks = tl // CHUNK
    tri = _chunk_tri(tl)
    hid = jnp.arange(LANES) // RWKV_HEAD
    bdp = (hid[:, None] == hid[None, :]).astype(BF16)
    gla_params = (*gla_params, tri)
    rwkv_params = (*rwkv_params, tri, bdp)
    assert (len(gla_params), len(rwkv_params), len(tail_params)) == (N_GLA, N_RWKV, N_TAIL)

    def nxt(t):
        tn = jnp.minimum(t + 1, nsteps - 1)
        return (tn // steps_per_seq, tn % steps_per_seq, 0)

    def prev(t):
        tp = jnp.maximum(t - 1, 0)
        return (tp // steps_per_seq, tp % steps_per_seq, 0)

    whole = lambda a: pl.BlockSpec(a.shape, lambda t: (0,) * a.ndim, pipeline_mode=pl.Buffered(1))
    consts = (ln, wg_in, wr_in, *gla_params, *rwkv_params, *tail_params)
    kern = functools.partial(_block_kernel, heads=heads, dk=dk, dv=dv, steps_per_seq=steps_per_seq)
    return pl.pallas_call(
        kern,
        grid=(nsteps + 1,),
        in_specs=[pl.BlockSpec((1, tl, d), nxt), pl.BlockSpec((1, tl, d), prev),
                  pl.BlockSpec((1, tl, p.shape[2]), prev)] + [whole(a) for a in consts],
        out_specs=pl.BlockSpec((1, tl, d), prev),
        out_shape=jax.ShapeDtypeStruct((bsz, s, d), F32),
        scratch_shapes=[
            pltpu.VMEM((tl, vw + width), BF16),
            pltpu.VMEM((2, tl, wg_in.shape[1]), F32),
            pltpu.VMEM((2, tl, wr_in.shape[1]), F32),
            pltpu.VMEM((tl + HALO, nqkv), F32),
            pltpu.VMEM((kw // LANES, 2 * dv, LANES), F32),
            pltpu.VMEM((tl + HALO, wr_in.shape[1]), F32),
            pltpu.VMEM((width // LANES, LANES, LANES), F32),
        ] + [pltpu.VMEM((tl, width), BF16)] * 7 + [
            pltpu.VMEM((tl, width), F32),
            pltpu.VMEM((nchunks, HALO, width), F32),
            pltpu.VMEM((tl, width), F32),
        ],
        compiler_params=pltpu.CompilerParams(
            dimension_semantics=("arbitrary",), vmem_limit_bytes=VMEM_LIMIT_BYTES),
        name="hybrid_block",
    )(x, x, p, *consts)


def _pad_cols(w, n):
    return jnp.pad(w, ((0, 0), (0, n - w.shape[1])))


def _place_rows(w, start, total):
    return jnp.pad(w, ((start, total - start - w.shape[0]), (0, 0)))


def _layer(h, p_i, ln_mix, w_in, conv_w, gk_w, gk_b, norm_g, mu, w0, w2, a0, a2, g2, k_k, k_a,
           r_k, gn_w, gn_b, w_out, ln_mlp, w_ff1, w_ff2, ln_ple, w_gate, w_proj, ln_next):
    row = lambda a: a.reshape(1, -1)

    gate_rank, kw = gk_w.shape
    dv = norm_g.shape[0]
    vw = (conv_w.shape[1] - 2 * kw)
    heads = vw // dv
    dk = kw // heads
    gla_cols = 2 * kw + 2 * vw + gate_rank
    gla_pad = _round_up(gla_cols, LANES)
    width = w0.shape[0]
    rwkv_cols = w_in.shape[1] - gla_cols
    rwkv_pad = _round_up(rwkv_cols, LANES)
    low_pad = rwkv_pad - 3 * width
    dr, ar = w2.shape[0], a2.shape[0]

    wg_in = _pad_cols(w_in[:, :gla_cols], gla_pad).astype(BF16)
    wr_in = _pad_cols(w_in[:, gla_cols:], rwkv_pad).astype(BF16)
    gla_params = (conv_w, _place_rows(gk_w, 0, gla_pad - (gla_cols - gate_rank)).astype(BF16), row(gk_b),
                  row(norm_g))
    rwkv_params = (
        _pad_cols(row(mu), rwkv_pad), row(w0), _place_rows(w2, 0, low_pad).astype(BF16), row(a0),
        _place_rows(a2, dr, low_pad).astype(BF16), _place_rows(g2, dr + ar, low_pad).astype(BF16),
        row(k_k), row(k_a),
        row(r_k), row(gn_w), row(gn_b))
    tail_params = (w_out.astype(BF16), row(ln_mlp), w_ff1.astype(BF16), w_ff2.astype(BF16),
                   row(ln_ple), w_gate.astype(BF16), w_proj.astype(BF16), row(ln_next))
    return _block(h, p_i, row(ln_mix), wg_in, wr_in, gla_params, rwkv_params, tail_params,
                  heads=heads, dk=dk, dv=dv)


def kernel(x, p, ln_mix, w_in, gla_conv_w, gla_gk_w, gla_gk_b, gla_norm_g, rwkv_mu, rwkv_w0,
           rwkv_w2, rwkv_a0, rwkv_a2, rwkv_g2, rwkv_k_k, rwkv_k_a, rwkv_r_k, rwkv_gn_w,
           rwkv_gn_b, w_out, ln_mlp, w_ff1, w_ff2, ln_ple, w_ple_gate, w_ple_proj, ln_final):
    depth = w_in.shape[0]
    assert depth == 1, "the fused tail applies the final norm, so exactly one layer is supported"
    assert x.shape[1] % TILE == 0 and TILE % (CHUNK * RWKV_UNROLL) == 0
    i = 0
    return _layer(x, p.reshape(p.shape[1:]), ln_mix[i], w_in[i], gla_conv_w[i], gla_gk_w[i], gla_gk_b[i],
                  gla_norm_g[i], rwkv_mu[i], rwkv_w0[i], rwkv_w2[i], rwkv_a0[i], rwkv_a2[i],
                  rwkv_g2[i], rwkv_k_k[i], rwkv_k_a[i], rwkv_r_k[i].reshape(-1), rwkv_gn_w[i],
                  rwkv_gn_b[i], w_out[i], ln_mlp[i], w_ff1[i], w_ff2[i], ln_ple[i],
                  w_ple_gate[i], w_ple_proj[i], ln_final)
```

```python
import functools

import jax
import jax.numpy as jnp
from jax import lax
from jax.experimental import pallas as pl
from jax.experimental.pallas import tpu as pltpu

F32 = jnp.float32
BF16 = jnp.bfloat16

NORM_EPS = 1e-6
GLA_GATE_NORMALIZER = 16.0
RWKV_GN_EPS = 64e-5
RWKV_HEAD = 64
CHUNK = 64
LANES = 128
HALO = 8
TILE = 256
RWKV_UNROLL = 4
MXU_WIDTH = 256
FF_UP_PIECES = 8
FF_DOWN_PIECES = 4
IN_PIECES = 2
STAGE_ORDER = "TITTIRG" "TTIR" "TTRG" "TTIR" "TIRG" "TRTRTRTR"
VMEM_LIMIT_BYTES = 60 * 1024 * 1024


def _sigmoid(x):
    return 1.0 / (1.0 + jnp.exp(-x))


def _softplus(x):
    return jnp.maximum(x, 0.0) + jnp.log(1.0 + jnp.exp(-jnp.abs(x)))


def _dot(a, b):
    return jnp.dot(a, b, preferred_element_type=F32)


def _dot_nt(a, b):
    return lax.dot_general(a, b, (((1,), (1,)), ((), ())), preferred_element_type=F32)


def _dot_tn(a, b):
    return lax.dot_general(a, b, (((0,), (0,)), ((), ())), preferred_element_type=F32)


def _split2(x):
    hi = x.astype(BF16)
    lo = (x - hi.astype(F32)).astype(BF16)
    return hi, lo


def _dot_exact_lhs(a_bf16, x):
    hi, lo = _split2(x)
    return _dot(a_bf16, hi) + _dot(a_bf16, lo)


def _dot_wide_rhs(a, b_hi, b_lo):
    a = a.astype(BF16)
    return _dot(a, b_hi) + _dot(a, b_lo)


def _rmsnorm(x, g):
    ms = jnp.mean(x * x, axis=-1, keepdims=True)
    return x * lax.rsqrt(ms + NORM_EPS) * g


def _chunk_tri(tl):
    idx = jnp.arange(tl)
    same = idx[:, None] // CHUNK == idx[None, :] // CHUNK
    return ((idx[:, None] >= idx[None, :]) & same).astype(BF16)


def _gla_body(z_ref, low_ref, cw_ref, gkw_ref, gkb_ref, ng_ref, tri_ref, xb_ref, st_ref, *, heads, dk, dv):
    kw = heads * dk
    vw = heads * dv
    nqkv = 2 * kw + vw
    taps = cw_ref.shape[0]
    pairs = kw // LANES
    tl = z_ref.shape[0]
    nchunks = tl // CHUNK
    xb_ref[HALO:HALO + tl, :] = z_ref[:, :nqkv]
    conv = jnp.zeros((tl, nqkv), F32)
    for t in range(taps):
        off = HALO - (taps - 1) + t
        conv = conv + cw_ref[t:t + 1, :] * xb_ref[off:off + tl, :]
    xb_ref[0:HALO, :] = xb_ref[tl:tl + HALO, :]
    qkv = conv * _sigmoid(conv)
    q = qkv[:, :kw] * (dk ** -0.5)
    k = qkv[:, kw:2 * kw]
    v = qkv[:, 2 * kw:nqkv].astype(BF16)
    low = low_ref[...]

    gk_pre = _dot(low.astype(BF16), gkw_ref[...]) + gkb_ref[...]
    yield
    gk = -_softplus(-gk_pre) * (1.0 / GLA_GATE_NORMALIZER)
    b = _dot_exact_lhs(tri_ref[...], gk)
    yield
    last_rows = [b[c * CHUNK + CHUNK - 1:(c + 1) * CHUNK, :] for c in range(nchunks)]
    b_last = jnp.concatenate([jnp.broadcast_to(row, (CHUNK, kw)) for row in last_rows], axis=0)
    b_ref = 0.5 * b_last
    qe = (q * jnp.exp(b - b_ref)).astype(BF16)
    head0 = lax.rem(lax.broadcasted_iota(jnp.int32, (tl, kw), 1), LANES) < dk
    ke = k * jnp.exp(b_ref - b)
    ke0 = jnp.where(head0, ke, 0.0).astype(BF16)
    ke1 = jnp.where(head0, 0.0, ke).astype(BF16)
    qb = (q * jnp.exp(b)).astype(BF16)
    kd = (k * jnp.exp(b_last - b)).astype(BF16)
    e_last = [jnp.exp(row) for row in last_rows]

    lane = lax.broadcasted_iota(jnp.int32, (CHUNK, LANES), 1)
    trow = lax.broadcasted_iota(jnp.int32, (CHUNK, LANES), 0)
    causal = jnp.where(lane < dk, lane, lane - dk) <= trow
    vzero = jnp.zeros((CHUNK, dv), BF16)
    srow = lax.broadcasted_iota(jnp.int32, (2 * dv, LANES), 0)
    scol = lax.broadcasted_iota(jnp.int32, (2 * dv, LANES), 1)
    st_mask = ((srow < dv) == (scol < dk)).astype(F32)

    items = [(c, p) for c in range(nchunks) for p in range(pairs)]
    rs = [slice(c * CHUNK, (c + 1) * CHUNK) for c in range(nchunks)]
    ls = [slice(p * LANES, (p + 1) * LANES) for p in range(pairs)]
    vs = [slice(2 * p * dv, 2 * (p + 1) * dv) for p in range(pairs)]
    s = [_dot_nt(qe[rs[c], ls[p]], jnp.concatenate([ke0[rs[c], ls[p]], ke1[rs[c], ls[p]]], axis=0))
         for c, p in items]
    s = [jnp.where(causal, s_, 0.0).astype(BF16) for s_ in s]
    v_bd = [jnp.concatenate(
        [jnp.concatenate([v[rs[c], 2 * p * dv:(2 * p + 1) * dv], vzero], axis=1),
         jnp.concatenate([vzero, v[rs[c], (2 * p + 1) * dv:(2 * p + 2) * dv]], axis=1)], axis=0)
        for c, p in items]
    o_intra = [_dot(s_, v_) for s_, v_ in zip(s, v_bd)]
    kv = [_dot_tn(v[rs[c], vs[p]], kd[rs[c], ls[p]]) for c, p in items]
    st = [st_ref[p] for p in range(pairs)]
    o_rows = []
    for c in range(nchunks):
        o_c = []
        for p in range(pairs):
            i = c * pairs + p
            o_c.append(o_intra[i] + _dot_nt(qb[rs[c], ls[p]], st[p].astype(BF16)))
            st[p] = st[p] * e_last[c][:, ls[p]] + st_mask * kv[i]
        o_rows.append(jnp.concatenate(o_c, axis=1))
    for p in range(pairs):
        st_ref[p] = st[p]
    o = jnp.concatenate(o_rows, axis=0)
    o_n = []
    for h in range(heads):
        o_h = o[:, h * dv:(h + 1) * dv]
        ms = jnp.mean(o_h * o_h, axis=-1, keepdims=True)
        o_n.append(o_h * lax.rsqrt(ms + NORM_EPS) * ng_ref[...])
    g = z_ref[:, nqkv:nqkv + vw]
    return (jnp.concatenate(o_n, axis=1) * (g * _sigmoid(g))).astype(BF16)


def _headsum(x, bd_pair):
    rows, width = x.shape
    groups = width // LANES
    xb = x.astype(BF16)
    stacked = jnp.concatenate([xb[:, i * LANES:(i + 1) * LANES] for i in range(groups)], axis=0)
    s = _dot(stacked, bd_pair)
    return jnp.concatenate([s[i * rows:(i + 1) * rows] for i in range(groups)], axis=1)


def _rwkv_body(z_ref, mu_ref, w0_ref, w2_ref, a0_ref, a2_ref, g2_ref, kk_ref, ka_ref,
               rk_ref, gnw_ref, gnb_ref, tri_ref, bdp_ref,
               xb_ref, st_ref, at_s, rt_s, bt_s, kt_s, bh_s, kh_s, v_s, rtf_s, gam_s, y_s):
    n = RWKV_HEAD
    width = w0_ref.shape[1]
    pairs = width // LANES
    tl = z_ref.shape[0]
    nchunks = tl // CHUNK
    levels = (CHUNK - 1).bit_length()

    z = z_ref[...]
    xb_ref[HALO:HALO + tl, :] = z
    zp = xb_ref[HALO - 1:HALO - 1 + tl, :]
    xb_ref[0:HALO, :] = xb_ref[tl:tl + HALO, :]
    zs = z + mu_ref[...] * (zp - z)
    r = zs[:, 0:width]
    k = zs[:, width:2 * width]
    v = zs[:, 2 * width:3 * width]
    low = zs[:, 3 * width:]

    w_pre = w0_ref[...] + _dot(jnp.tanh(low).astype(BF16), w2_ref[...])
    logw = -jnp.exp(-_softplus(-w_pre) - 0.5)
    a_sig = _sigmoid(a0_ref[...] + _dot(low.astype(BF16), a2_ref[...]))
    g = _dot(_sigmoid(low).astype(BF16), g2_ref[...])
    yield

    bdp = bdp_ref[...]
    kk = k * kk_ref[...]
    kk_sq = _headsum(kk * kk, bdp)
    yield
    kk = kk / jnp.maximum(jnp.sqrt(kk_sq), 1e-12)
    kf = k * (1.0 + (a_sig - 1.0) * ka_ref[...])
    bv = kk * a_sig
    bonus = _headsum(r * kf * rk_ref[...], bdp)
    yield

    lg = _dot_exact_lhs(tri_ref[...], logw)
    yield
    last_rows = [lg[c * CHUNK + CHUNK - 1:(c + 1) * CHUNK, :] for c in range(nchunks)]
    lg_last = jnp.concatenate([jnp.broadcast_to(row, (CHUNK, width)) for row in last_rows], axis=0)
    for c, row in enumerate(last_rows):
        gam_s[c] = jnp.broadcast_to(jnp.exp(row), (HALO, width))
    e_inv = jnp.exp(-lg)
    e_rem = jnp.exp(lg_last - lg)
    rt = r * jnp.exp(lg)
    at_s[...] = (-kk * jnp.exp(lg - logw)).astype(BF16)
    rt_s[...] = rt.astype(BF16)
    rtf_s[...] = rt
    bt_s[...] = (bv * e_inv).astype(BF16)
    kt_s[...] = (kf * e_inv).astype(BF16)
    bh_s[...] = (bv * e_rem).astype(BF16)
    kh_s[...] = (kf * e_rem).astype(BF16)
    v_s[...] = v.astype(BF16)
    yield

    lane = lax.broadcasted_iota(jnp.int32, (CHUNK, LANES), 1)
    trow = lax.broadcasted_iota(jnp.int32, (CHUNK, LANES), 0)
    tcol = jnp.where(lane < n, lane, lane - n)
    incl, strict = tcol <= trow, tcol < trow
    eye2 = (tcol == trow).astype(F32)
    m0 = (lane < n).astype(BF16)
    m1 = (lane >= n).astype(BF16)
    m0w = jnp.concatenate([m0, m0], axis=1)
    m1w = jnp.concatenate([m1, m1], axis=1)
    prow = lax.broadcasted_iota(jnp.int32, (LANES, LANES), 0)
    pcol = lax.broadcasted_iota(jnp.int32, (LANES, LANES), 1)
    blk = ((prow < n) == (pcol < n)).astype(F32)
    eye = (prow == pcol).astype(F32)
    lsl = [slice(p * LANES, (p + 1) * LANES) for p in range(pairs)]

    def per_head(xb, wide=False):
        return jnp.concatenate([xb * (m0w if wide else m0), xb * (m1w if wide else m1)], axis=0)

    for it in range(nchunks // RWKV_UNROLL):
        items = [(u, p) for u in range(RWKV_UNROLL) for p in range(pairs)]
        rows = [pl.ds((it * RWKV_UNROLL + u) * CHUNK, CHUNK) for u in range(RWKV_UNROLL)]
        at_b = [at_s[rows[u], lsl[p]] for u, p in items]
        v_b = [v_s[rows[u], lsl[p]] for u, p in items]
        lhs = [jnp.concatenate([a_, rt_s[rows[u], lsl[p]]], axis=0)
               for a_, (u, p) in zip(at_b, items)]
        sb = [_dot_nt(l_, per_head(bt_s[rows[u], lsl[p]])) for l_, (u, p) in zip(lhs, items)]
        sk = [_dot_nt(l_, per_head(kt_s[rows[u], lsl[p]])) for l_, (u, p) in zip(lhs, items)]
        yield
        a_ab = [jnp.where(strict, s_[:CHUNK], 0.0).astype(BF16) for s_ in sb]
        a_rb = [jnp.where(incl, s_[CHUNK:], 0.0).astype(BF16) for s_ in sb]
        a_kk = [jnp.concatenate([jnp.where(strict, s_[:CHUNK], 0.0),
                                 jnp.where(incl, s_[CHUNK:], 0.0)], axis=0).astype(BF16) for s_ in sk]
        akv = [_dot(a_, per_head(v_)) for a_, v_ in zip(a_kk, v_b)]
        yield
        pw = a_ab
        inv = [eye2 + p_.astype(F32) for p_ in pw]
        for lvl in range(1, levels):
            pw = [_dot(p_, per_head(p_)).astype(BF16) for p_ in pw]
            inv = [q_ + _dot(p_, per_head(q_.astype(BF16))) for q_, p_ in zip(inv, pw)]
            yield
        x0 = [jnp.concatenate([a_, k_[:CHUNK].astype(BF16)], axis=1) for a_, k_ in zip(at_b, akv)]
        xb = [_dot(q_.astype(BF16), per_head(x_, wide=True)).astype(BF16)
              for q_, x_ in zip(inv, x0)]
        zz = [_dot(a_, per_head(x_, wide=True)) for a_, x_ in zip(a_rb, xb)]
        mn = []
        for i, (u, p) in enumerate(items):
            lb = jnp.concatenate([bh_s[rows[u], lsl[p]], kh_s[rows[u], lsl[p]]], axis=0)
            rb = jnp.concatenate(
                [xb[i], jnp.concatenate([jnp.zeros_like(v_b[i]), v_b[i]], axis=1)], axis=0)
            mn.append(_dot_tn(lb, rb))
        yield
        t_cur = [st_ref[p] for p in range(pairs)]
        for i, (u, p) in enumerate(items):
            ls = lsl[p]
            gam = gam_s[it * RWKV_UNROLL + u][0:1, ls]
            rhat = rtf_s[rows[u], ls] + zz[i][:, :LANES]
            yzero = zz[i][:, LANES:] + akv[i][CHUNK:]
            m_mat = blk * mn[i][:, :LANES] + eye * gam
            n_mat = blk * mn[i][:, LANES:]
            t_hi, t_lo = _split2(t_cur[p])
            y_s[rows[u], ls] = _dot(rhat.astype(BF16), t_hi) + yzero
            t_cur[p] = _dot_wide_rhs(m_mat, t_hi, t_lo) + n_mat
        for p in range(pairs):
            st_ref[p] = t_cur[p]
        yield

    y = y_s[...]
    mean = _headsum(y, bdp) * (1.0 / n)
    d = y - mean
    var = _headsum(d * d, bdp) * (1.0 / n)
    yn = d * lax.rsqrt(var + RWKV_GN_EPS) * gnw_ref[...] + gnb_ref[...]
    return ((yn + bonus * v) * g).astype(BF16)


def _tail_body(x, o, p, wo_ref, lnm_ref, w1_ref, w2_ref, lnp_ref, wg_ref, wp_ref, lnf_ref):
    h = x + _dot(o, wo_ref[...])
    n1 = _rmsnorm(h, lnm_ref[...]).astype(BF16)
    yield
    d_ff, d = w1_ref.shape[1], w2_ref.shape[1]
    hid = []
    for cs in _lane_blocks(d_ff, FF_UP_PIECES):
        a = jnp.maximum(_dot(n1, w1_ref[:, cs]).astype(BF16), 0.0)
        hid.append(a * a)
        yield
    hid = jnp.concatenate(hid, axis=1)
    down = []
    for cs in _lane_blocks(d, FF_DOWN_PIECES):
        down.append(_dot(hid, w2_ref[:, cs]))
        yield
    h = h + jnp.concatenate(down, axis=1)
    n2 = _rmsnorm(h, lnp_ref[...]).astype(BF16)
    gate = _sigmoid(_dot(n2, wg_ref[...]))
    h = h + gate * _dot(p.astype(BF16), wp_ref[...])
    return _rmsnorm(h, lnf_ref[...])


N_GLA, N_RWKV, N_TAIL = 5, 13, 8


def _lane_blocks(n, k):
    groups = n // MXU_WIDTH
    cuts = [(groups * i // k) * MXU_WIDTH for i in range(k)] + [n]
    return [slice(a, b) for a, b in zip(cuts[:-1], cuts[1:])]


def _in_proj_body(x, ln_ref, wg_ref, wr_ref, zg_ref, zr_ref):
    xn = _rmsnorm(x, ln_ref[...]).astype(BF16)
    for w_ref, z_ref in ((wg_ref, zg_ref), (wr_ref, zr_ref)):
        for cs in _lane_blocks(w_ref.shape[1], IN_PIECES):
            yield
            z_ref[:, cs] = _dot(xn, w_ref[:, cs])


def _interleave(gens, order):
    results = {}
    live = dict(gens)

    def advance(name):
        if name in live:
            try:
                next(live[name])
            except StopIteration as done:
                results[name] = done.value
                del live[name]

    for name in order:
        advance(name)
    while live:
        for name in list(live):
            advance(name)
    return results


def _block_kernel(*refs, heads, dk, dv, steps_per_seq):
    xn_ref, xp_ref, pp_ref, ln_ref, wg_ref, wr_ref = refs[:6]
    i = 6
    gla_p, rwkv_p, tail_p = refs[i:i + N_GLA], refs[i + N_GLA:i + N_GLA + N_RWKV], \
        refs[i + N_GLA + N_RWKV:i + N_GLA + N_RWKV + N_TAIL]
    i += N_GLA + N_RWKV + N_TAIL
    out_ref = refs[i]
    o_s, zg_s, zr_s, gxb_ref, gst_ref, rxb_ref, rst_ref = refs[i + 1:i + 8]
    rwkv_s = refs[i + 8:]
    t = pl.program_id(0)
    cur = lax.rem(t, 2)

    @pl.when(t % steps_per_seq == 0)
    def _():
        gxb_ref[0:HALO, :] = jnp.zeros((HALO, gxb_ref.shape[1]), F32)
        gst_ref[...] = jnp.zeros_like(gst_ref)
        rxb_ref[0:HALO, :] = jnp.zeros((HALO, rxb_ref.shape[1]), F32)
        rst_ref[...] = jnp.zeros_like(rst_ref)

    @pl.when(t == 0)
    def _():
        o_s[...] = jnp.zeros_like(o_s)
        for _ in _in_proj_body(xp_ref[0], ln_ref, wg_ref, wr_ref, zg_s.at[0], zr_s.at[0]):
            pass

    res = _interleave({
        "T": _tail_body(xp_ref[0], o_s[...], pp_ref[0], *tail_p),
        "I": _in_proj_body(xn_ref[0], ln_ref, wg_ref, wr_ref, zg_s.at[1 - cur], zr_s.at[1 - cur]),
        "R": _rwkv_body(zr_s.at[cur], *rwkv_p, rxb_ref, rst_ref, *rwkv_s),
        "G": _gla_body(zg_s.at[cur], zr_s.at[cur, :, pl.ds(zr_s.shape[2] - MXU_WIDTH, MXU_WIDTH)], *gla_p,
                       gxb_ref, gst_ref, heads=heads, dk=dk, dv=dv),
    }, STAGE_ORDER)
    out_ref[0] = res["T"]
    o_s[...] = jnp.concatenate([res["G"], res["R"]], axis=1)


def _block(x, p, ln, wg_in, wr_in, gla_params, rwkv_params, tail_params, *, heads, dk, dv):
    bsz, s, d = x.shape
    tl = TILE
    steps_per_seq = s // tl
    nsteps = bsz * steps_per_seq
    kw, vw = heads * dk, heads * dv
    nqkv = 2 * kw + vw
    width = rwkv_params[1].shape[1]
    nchunks = tl // CHUNK
    tri = _chunk_tri(tl)
    hid = jnp.arange(LANES) // RWKV_HEAD
    bdp = (hid[:, None] == hid[None, :]).astype(BF16)
    gla_params = (*gla_params, tri)
    rwkv_params = (*rwkv_params, tri, bdp)
    assert (len(gla_params), len(rwkv_params), len(tail_params)) == (N_GLA, N_RWKV, N_TAIL)

    def nxt(t):
        tn = jnp.minimum(t + 1, nsteps - 1)
        return (tn // steps_per_seq, tn % steps_per_seq, 0)

    def prev(t):
        tp = jnp.maximum(t - 1, 0)
        return (tp // steps_per_seq, tp % steps_per_seq, 0)

    whole = lambda a: pl.BlockSpec(a.shape, lambda t: (0,) * a.ndim, pipeline_mode=pl.Buffered(1))
    consts = (ln, wg_in, wr_in, *gla_params, *rwkv_params, *tail_params)
    kern = functools.partial(_block_kernel, heads=heads, dk=dk, dv=dv, steps_per_seq=steps_per_seq)
    return pl.pallas_call(
        kern,
        grid=(nsteps + 1,),
        in_specs=[pl.BlockSpec((1, tl, d), nxt), pl.BlockSpec((1, tl, d), prev),
                  pl.BlockSpec((1, tl, p.shape[2]), prev)] + [whole(a) for a in consts],
        out_specs=pl.BlockSpec((1, tl, d), prev),
        out_shape=jax.ShapeDtypeStruct((bsz, s, d), F32),
        scratch_shapes=[
            pltpu.VMEM((tl, vw + width), BF16),
            pltpu.VMEM((2, tl, wg_in.shape[1]), F32),
            pltpu.VMEM((2, tl, wr_in.shape[1]), F32),
            pltpu.VMEM((tl + HALO, nqkv), F32),
            pltpu.VMEM((kw // LANES, 2 * dv, LANES), F32),
            pltpu.VMEM((tl + HALO, wr_in.shape[1]), F32),
            pltpu.VMEM((width // LANES, LANES, LANES), F32),
        ] + [pltpu.VMEM((tl, width), BF16)] * 7 + [
            pltpu.VMEM((tl, width), F32),
            pltpu.VMEM((nchunks, HALO, width), F32),
            pltpu.VMEM((tl, width), F32),
        ],
        compiler_params=pltpu.CompilerParams(
            dimension_semantics=("arbitrary",), vmem_limit_bytes=VMEM_LIMIT_BYTES),
        name="hybrid_block",
    )(x, x, p, *consts)


def _pad_cols(w, n):
    return jnp.pad(w, ((0, 0), (0, n - w.shape[1])))


def _place_rows(w, start, total):
    return jnp.pad(w, ((start, total - start - w.shape[0]), (0, 0)))


def _layer(h, p_i, ln_mix, w_in, conv_w, gk_w, gk_b, norm_g, mu, w0, w2, a0, a2, g2, k_k, k_a,
           r_k, gn_w, gn_b, w_out, ln_mlp, w_ff1, w_ff2, ln_ple, w_gate, w_proj, ln_next):
    row = lambda a: a.reshape(1, -1)

    gate_rank, kw = gk_w.shape
    dv = norm_g.shape[0]
    vw = (conv_w.shape[1] - 2 * kw)
    heads = vw // dv
    dk = kw // heads
    gla_main = 2 * kw + 2 * vw
    gla_cols = gla_main + gate_rank
    width = w0.shape[0]
    low_pad = MXU_WIDTH
    rwkv_pad = 3 * width + low_pad
    dr, ar, gr = w2.shape[0], a2.shape[0], g2.shape[0]
    assert gla_main % MXU_WIDTH == 0 and dr + ar + gr + gate_rank <= low_pad

    wg_in = w_in[:, :gla_main].astype(BF16)
    wr_in = _pad_cols(jnp.concatenate([w_in[:, gla_cols:], w_in[:, gla_main:gla_cols]], axis=1),
                      rwkv_pad).astype(BF16)
    gla_params = (conv_w, _place_rows(gk_w, dr + ar + gr, low_pad).astype(BF16), row(gk_b), row(norm_g))
    rwkv_params = (
        _pad_cols(row(mu), rwkv_pad), row(w0), _place_rows(w2, 0, low_pad).astype(BF16), row(a0),
        _place_rows(a2, dr, low_pad).astype(BF16), _place_rows(g2, dr + ar, low_pad).astype(BF16),
        row(k_k), row(k_a),
        row(r_k), row(gn_w), row(gn_b))
    tail_params = (w_out.astype(BF16), row(ln_mlp), w_ff1.astype(BF16), w_ff2.astype(BF16),
                   row(ln_ple), w_gate.astype(BF16), w_proj.astype(BF16), row(ln_next))
    return _block(h, p_i, row(ln_mix), wg_in, wr_in, gla_params, rwkv_params, tail_params,
                  heads=heads, dk=dk, dv=dv)


def kernel(x, p, ln_mix, w_in, gla_conv_w, gla_gk_w, gla_gk_b, gla_norm_g, rwkv_mu, rwkv_w0,
           rwkv_w2, rwkv_a0, rwkv_a2, rwkv_g2, rwkv_k_k, rwkv_k_a, rwkv_r_k, rwkv_gn_w,
           rwkv_gn_b, w_out, ln_mlp, w_ff1, w_ff2, ln_ple, w_ple_gate, w_ple_proj, ln_final):
    depth = w_in.shape[0]
    assert depth == 1, "the fused tail applies the final norm, so exactly one layer is supported"
    assert x.shape[1] % TILE == 0 and TILE % (CHUNK * RWKV_UNROLL) == 0
    i = 0
    return _layer(x, p.reshape(p.shape[1:]), ln_mix[i], w_in[i], gla_conv_w[i], gla_gk_w[i], gla_gk_b[i],
                  gla_norm_g[i], rwkv_mu[i], rwkv_w0[i], rwkv_w2[i], rwkv_a0[i], rwkv_a2[i],
                  rwkv_g2[i], rwkv_k_k[i], rwkv_k_a[i], rwkv_r_k[i].reshape(-1), rwkv_gn_w[i],
                  rwkv_gn_b[i], w_out[i], ln_mlp[i], w_ff1[i], w_ff2[i], ln_ple[i],
                  w_ple_gate[i], w_ple_proj[i], ln_final)
```

```python
import functools

import jax
import jax.numpy as jnp
from jax import lax
from jax.experimental import pallas as pl
from jax.experimental.pallas import tpu as pltpu

F32 = jnp.float32
BF16 = jnp.bfloat16

NORM_EPS = 1e-6
GLA_GATE_NORMALIZER = 16.0
RWKV_GN_EPS = 64e-5
RWKV_HEAD = 64
CHUNK = 64
LANES = 128
HALO = 8
TILE = 256
RWKV_UNROLL = 4
MXU_WIDTH = 256
FF_UP_PIECES = 8
FF_DOWN_PIECES = 4
IN_PIECES = 2
STAGE_ORDER = "RG" "TIT" "R" "IT" "RG" "TTI" "RG" "T" "R" "TT" "R" "TI" "RG" "I" "RTRTRTRTRT"
VMEM_LIMIT_BYTES = 60 * 1024 * 1024


def _sigmoid(x):
    return 1.0 / (1.0 + jnp.exp(-x))


def _softplus(x):
    return jnp.maximum(x, 0.0) + jnp.log(1.0 + jnp.exp(-jnp.abs(x)))


def _dot(a, b):
    return jnp.dot(a, b, preferred_element_type=F32)


def _dot_nt(a, b):
    return lax.dot_general(a, b, (((1,), (1,)), ((), ())), preferred_element_type=F32)


def _dot_tn(a, b):
    return lax.dot_general(a, b, (((0,), (0,)), ((), ())), preferred_element_type=F32)


def _split2(x):
    hi = x.astype(BF16)
    lo = (x - hi.astype(F32)).astype(BF16)
    return hi, lo


def _dot_wide_rhs(a, b_hi, b_lo):
    a = a.astype(BF16)
    return _dot(a, b_hi) + _dot(a, b_lo)


def _rmsnorm(x, g):
    ms = jnp.mean(x * x, axis=-1, keepdims=True)
    return x * lax.rsqrt(ms + NORM_EPS) * g


def _chunk_tri(tl):
    idx = jnp.arange(tl)
    same = idx[:, None] // CHUNK == idx[None, :] // CHUNK
    return ((idx[:, None] >= idx[None, :]) & same).astype(BF16)


def _gla_body(z_ref, low_ref, cw_ref, gkw_ref, gkb_ref, ng_ref, tri_ref, xb_ref, st_ref, *, heads, dk, dv):
    kw = heads * dk
    vw = heads * dv
    nqkv = 2 * kw + vw
    taps = cw_ref.shape[0]
    pairs = kw // LANES
    tl = z_ref.shape[0]
    nchunks = tl // CHUNK
    xb_ref[HALO:HALO + tl, :] = z_ref[:, :nqkv]
    conv = jnp.zeros((tl, nqkv), F32)
    for t in range(taps):
        off = HALO - (taps - 1) + t
        conv = conv + cw_ref[t:t + 1, :] * xb_ref[off:off + tl, :]
    xb_ref[0:HALO, :] = xb_ref[tl:tl + HALO, :]
    qkv = conv * _sigmoid(conv)
    q = qkv[:, :kw] * (dk ** -0.5)
    k = qkv[:, kw:2 * kw]
    v = qkv[:, 2 * kw:nqkv].astype(BF16)
    low = low_ref[...]

    gk_pre = _dot(low.astype(BF16), gkw_ref[...]) + gkb_ref[...]
    yield
    gk = -_softplus(-gk_pre) * (1.0 / GLA_GATE_NORMALIZER)
    gk_hi, gk_lo = _split2(gk)
    yield
    b = _dot(tri_ref[...], gk_hi) + _dot(tri_ref[...], gk_lo)
    yield
    last_rows = [b[c * CHUNK + CHUNK - 1:(c + 1) * CHUNK, :] for c in range(nchunks)]
    b_last = jnp.concatenate([jnp.broadcast_to(row, (CHUNK, kw)) for row in last_rows], axis=0)
    b_ref = 0.5 * b_last
    qe = (q * jnp.exp(b - b_ref)).astype(BF16)
    head0 = lax.rem(lax.broadcasted_iota(jnp.int32, (tl, kw), 1), LANES) < dk
    ke = k * jnp.exp(b_ref - b)
    ke0 = jnp.where(head0, ke, 0.0).astype(BF16)
    ke1 = jnp.where(head0, 0.0, ke).astype(BF16)
    qb = (q * jnp.exp(b)).astype(BF16)
    kd = (k * jnp.exp(b_last - b)).astype(BF16)
    e_last = [jnp.exp(row) for row in last_rows]

    lane = lax.broadcasted_iota(jnp.int32, (CHUNK, LANES), 1)
    trow = lax.broadcasted_iota(jnp.int32, (CHUNK, LANES), 0)
    causal = jnp.where(lane < dk, lane, lane - dk) <= trow
    vzero = jnp.zeros((CHUNK, dv), BF16)
    srow = lax.broadcasted_iota(jnp.int32, (2 * dv, LANES), 0)
    scol = lax.broadcasted_iota(jnp.int32, (2 * dv, LANES), 1)
    st_mask = ((srow < dv) == (scol < dk)).astype(F32)

    items = [(c, p) for c in range(nchunks) for p in range(pairs)]
    rs = [slice(c * CHUNK, (c + 1) * CHUNK) for c in range(nchunks)]
    ls = [slice(p * LANES, (p + 1) * LANES) for p in range(pairs)]
    vs = [slice(2 * p * dv, 2 * (p + 1) * dv) for p in range(pairs)]
    s = [_dot_nt(qe[rs[c], ls[p]], jnp.concatenate([ke0[rs[c], ls[p]], ke1[rs[c], ls[p]]], axis=0))
         for c, p in items]
    s = [jnp.where(causal, s_, 0.0).astype(BF16) for s_ in s]
    v_bd = [jnp.concatenate(
        [jnp.concatenate([v[rs[c], 2 * p * dv:(2 * p + 1) * dv], vzero], axis=1),
         jnp.concatenate([vzero, v[rs[c], (2 * p + 1) * dv:(2 * p + 2) * dv]], axis=1)], axis=0)
        for c, p in items]
    o_intra = [_dot(s_, v_) for s_, v_ in zip(s, v_bd)]
    kv = [_dot_tn(v[rs[c], vs[p]], kd[rs[c], ls[p]]) for c, p in items]
    st = [st_ref[p] for p in range(pairs)]
    o_rows = []
    for c in range(nchunks):
        o_c = []
        for p in range(pairs):
            i = c * pairs + p
            o_c.append(o_intra[i] + _dot_nt(qb[rs[c], ls[p]], st[p].astype(BF16)))
            st[p] = st[p] * e_last[c][:, ls[p]] + st_mask * kv[i]
        o_rows.append(jnp.concatenate(o_c, axis=1))
    for p in range(pairs):
        st_ref[p] = st[p]
    o = jnp.concatenate(o_rows, axis=0)
    o_n = []
    for h in range(heads):
        o_h = o[:, h * dv:(h + 1) * dv]
        ms = jnp.mean(o_h * o_h, axis=-1, keepdims=True)
        o_n.append(o_h * lax.rsqrt(ms + NORM_EPS) * ng_ref[...])
    g = z_ref[:, nqkv:nqkv + vw]
    return (jnp.concatenate(o_n, axis=1) * (g * _sigmoid(g))).astype(BF16)


def _headsum(x, bd_pair):
    rows, width = x.shape
    groups = width // LANES
    xb = x.astype(BF16)
    stacked = jnp.concatenate([xb[:, i * LANES:(i + 1) * LANES] for i in range(groups)], axis=0)
    s = _dot(stacked, bd_pair)
    return jnp.concatenate([s[i * rows:(i + 1) * rows] for i in range(groups)], axis=1)


def _rwkv_body(z_ref, mu_ref, w0_ref, w2_ref, a0_ref, a2_ref, g2_ref, kk_ref, ka_ref,
               rk_ref, gnw_ref, gnb_ref, tri_ref, bdp_ref,
               xb_ref, st_ref, at_s, rt_s, bt_s, kt_s, bh_s, kh_s, v_s, rtf_s, gam_s, y_s):
    n = RWKV_HEAD
    width = w0_ref.shape[1]
    pairs = width // LANES
    tl = z_ref.shape[0]
    nchunks = tl // CHUNK
    levels = (CHUNK - 1).bit_length()

    z = z_ref[...]
    xb_ref[HALO:HALO + tl, :] = z
    zp = xb_ref[HALO - 1:HALO - 1 + tl, :]
    xb_ref[0:HALO, :] = xb_ref[tl:tl + HALO, :]
    zs = z + mu_ref[...] * (zp - z)
    r = zs[:, 0:width]
    k = zs[:, width:2 * width]
    v = zs[:, 2 * width:3 * width]
    low = zs[:, 3 * width:]

    low_t, low_b, low_s = jnp.tanh(low).astype(BF16), low.astype(BF16), _sigmoid(low).astype(BF16)
    yield
    w_pre = w0_ref[...] + _dot(low_t, w2_ref[...])
    a_pre = a0_ref[...] + _dot(low_b, a2_ref[...])
    g = _dot(low_s, g2_ref[...])
    yield
    logw = -jnp.exp(-_softplus(-w_pre) - 0.5)
    a_sig = _sigmoid(a_pre)

    bdp = bdp_ref[...]
    kk = k * kk_ref[...]
    kk2 = kk * kk
    kf = k * (1.0 + (a_sig - 1.0) * ka_ref[...])
    rkr = r * kf * rk_ref[...]
    yield
    kk_sq = _headsum(kk2, bdp)
    bonus = _headsum(rkr, bdp)
    yield
    kk = kk / jnp.maximum(jnp.sqrt(kk_sq), 1e-12)
    bv = kk * a_sig
    logw_hi, logw_lo = _split2(logw)
    yield

    lg = _dot(tri_ref[...], logw_hi) + _dot(tri_ref[...], logw_lo)
    yield
    last_rows = [lg[c * CHUNK + CHUNK - 1:(c + 1) * CHUNK, :] for c in range(nchunks)]
    lg_last = jnp.concatenate([jnp.broadcast_to(row, (CHUNK, width)) for row in last_rows], axis=0)
    for c, row in enumerate(last_rows):
        gam_s[c] = jnp.broadcast_to(jnp.exp(row), (HALO, width))
    e_inv = jnp.exp(-lg)
    e_rem = jnp.exp(lg_last - lg)
    rt = r * jnp.exp(lg)
    at_s[...] = (-kk * jnp.exp(lg - logw)).astype(BF16)
    rt_s[...] = rt.astype(BF16)
    rtf_s[...] = rt
    bt_s[...] = (bv * e_inv).astype(BF16)
    kt_s[...] = (kf * e_inv).astype(BF16)
    bh_s[...] = (bv * e_rem).astype(BF16)
    kh_s[...] = (kf * e_rem).astype(BF16)
    v_s[...] = v.astype(BF16)
    yield

    lane = lax.broadcasted_iota(jnp.int32, (CHUNK, LANES), 1)
    trow = lax.broadcasted_iota(jnp.int32, (CHUNK, LANES), 0)
    tcol = jnp.where(lane < n, lane, lane - n)
    incl, strict = tcol <= trow, tcol < trow
    eye2 = (tcol == trow).astype(F32)
    m0 = (lane < n).astype(BF16)
    m1 = (lane >= n).astype(BF16)
    m0w = jnp.concatenate([m0, m0], axis=1)
    m1w = jnp.concatenate([m1, m1], axis=1)
    prow = lax.broadcasted_iota(jnp.int32, (LANES, LANES), 0)
    pcol = lax.broadcasted_iota(jnp.int32, (LANES, LANES), 1)
    blk = ((prow < n) == (pcol < n)).astype(F32)
    eye = (prow == pcol).astype(F32)
    lsl = [slice(p * LANES, (p + 1) * LANES) for p in range(pairs)]

    def per_head(xb, wide=False):
        return jnp.concatenate([xb * (m0w if wide else m0), xb * (m1w if wide else m1)], axis=0)

    for it in range(nchunks // RWKV_UNROLL):
        items = [(u, p) for u in range(RWKV_UNROLL) for p in range(pairs)]
        rows = [pl.ds((it * RWKV_UNROLL + u) * CHUNK, CHUNK) for u in range(RWKV_UNROLL)]
        at_b = [at_s[rows[u], lsl[p]] for u, p in items]
        v_b = [v_s[rows[u], lsl[p]] for u, p in items]
        lhs = [jnp.concatenate([a_, rt_s[rows[u], lsl[p]]], axis=0)
               for a_, (u, p) in zip(at_b, items)]
        sb = [_dot_nt(l_, per_head(bt_s[rows[u], lsl[p]])) for l_, (u, p) in zip(lhs, items)]
        sk = [_dot_nt(l_, per_head(kt_s[rows[u], lsl[p]])) for l_, (u, p) in zip(lhs, items)]
        yield
        a_ab = [jnp.where(strict, s_[:CHUNK], 0.0).astype(BF16) for s_ in sb]
        a_rb = [jnp.where(incl, s_[CHUNK:], 0.0).astype(BF16) for s_ in sb]
        a_kk = [jnp.concatenate([jnp.where(strict, s_[:CHUNK], 0.0),
                                 jnp.where(incl, s_[CHUNK:], 0.0)], axis=0).astype(BF16) for s_ in sk]
        akv = [_dot(a_, per_head(v_)) for a_, v_ in zip(a_kk, v_b)]
        yield
        pw = a_ab
        inv = [eye2 + p_.astype(F32) for p_ in pw]
        for lvl in range(1, levels):
            pw = [_dot(p_, per_head(p_)).astype(BF16) for p_ in pw]
            inv = [q_ + _dot(p_, per_head(q_.astype(BF16))) for q_, p_ in zip(inv, pw)]
            yield
        x0 = [jnp.concatenate([a_, k_[:CHUNK].astype(BF16)], axis=1) for a_, k_ in zip(at_b, akv)]
        xb = [_dot(q_.astype(BF16), per_head(x_, wide=True)).astype(BF16)
              for q_, x_ in zip(inv, x0)]
        zz = [_dot(a_, per_head(x_, wide=True)) for a_, x_ in zip(a_rb, xb)]
        mn = []
        for i, (u, p) in enumerate(items):
            lb = jnp.concatenate([bh_s[rows[u], lsl[p]], kh_s[rows[u], lsl[p]]], axis=0)
            rb = jnp.concatenate(
                [xb[i], jnp.concatenate([jnp.zeros_like(v_b[i]), v_b[i]], axis=1)], axis=0)
            mn.append(_dot_tn(lb, rb))
        yield
        t_cur = [st_ref[p] for p in range(pairs)]
        for i, (u, p) in enumerate(items):
            ls = lsl[p]
            gam = gam_s[it * RWKV_UNROLL + u][0:1, ls]
            rhat = rtf_s[rows[u], ls] + zz[i][:, :LANES]
            yzero = zz[i][:, LANES:] + akv[i][CHUNK:]
            m_mat = blk * mn[i][:, :LANES] + eye * gam
            n_mat = blk * mn[i][:, LANES:]
            t_hi, t_lo = _split2(t_cur[p])
            y_s[rows[u], ls] = _dot(rhat.astype(BF16), t_hi) + yzero
            t_cur[p] = _dot_wide_rhs(m_mat, t_hi, t_lo) + n_mat
        for p in range(pairs):
            st_ref[p] = t_cur[p]
        yield

    y = y_s[...]
    mean = _headsum(y, bdp) * (1.0 / n)
    d = y - mean
    var = _headsum(d * d, bdp) * (1.0 / n)
    yn = d * lax.rsqrt(var + RWKV_GN_EPS) * gnw_ref[...] + gnb_ref[...]
    return ((yn + bonus * v) * g).astype(BF16)


def _tail_body(x, o, p, wo_ref, lnm_ref, w1_ref, w2_ref, lnp_ref, wg_ref, wp_ref, lnf_ref):
    h = x + _dot(o, wo_ref[...])
    n1 = _rmsnorm(h, lnm_ref[...]).astype(BF16)
    yield
    d_ff, d = w1_ref.shape[1], w2_ref.shape[1]
    hid = []
    for cs in _lane_blocks(d_ff, FF_UP_PIECES):
        a = jnp.maximum(_dot(n1, w1_ref[:, cs]).astype(BF16), 0.0)
        hid.append(a * a)
        yield
    hid = jnp.concatenate(hid, axis=1)
    down = []
    for cs in _lane_blocks(d, FF_DOWN_PIECES):
        down.append(_dot(hid, w2_ref[:, cs]))
        yield
    h = h + jnp.concatenate(down, axis=1)
    n2 = _rmsnorm(h, lnp_ref[...]).astype(BF16)
    gate = _sigmoid(_dot(n2, wg_ref[...]))
    h = h + gate * _dot(p.astype(BF16), wp_ref[...])
    return _rmsnorm(h, lnf_ref[...])


N_GLA, N_RWKV, N_TAIL = 5, 13, 8


def _lane_blocks(n, k):
    groups = n // MXU_WIDTH
    cuts = [(groups * i // k) * MXU_WIDTH for i in range(k)] + [n]
    return [slice(a, b) for a, b in zip(cuts[:-1], cuts[1:])]


def _in_proj_body(x, ln_ref, wg_ref, wr_ref, zg_ref, zr_ref):
    xn = _rmsnorm(x, ln_ref[...]).astype(BF16)
    for w_ref, z_ref in ((wg_ref, zg_ref), (wr_ref, zr_ref)):
        for cs in _lane_blocks(w_ref.shape[1], IN_PIECES):
            yield
            z_ref[:, cs] = _dot(xn, w_ref[:, cs])


def _interleave(gens, order):
    results = {}
    live = dict(gens)

    def advance(name):
        if name in live:
            try:
                next(live[name])
            except StopIteration as done:
                results[name] = done.value
                del live[name]

    for name in order:
        advance(name)
    while live:
        for name in list(live):
            advance(name)
    return results


def _block_kernel(*refs, heads, dk, dv, steps_per_seq):
    xn_ref, xp_ref, pp_ref, ln_ref, wg_ref, wr_ref = refs[:6]
    i = 6
    gla_p, rwkv_p, tail_p = refs[i:i + N_GLA], refs[i + N_GLA:i + N_GLA + N_RWKV], \
        refs[i + N_GLA + N_RWKV:i + N_GLA + N_RWKV + N_TAIL]
    i += N_GLA + N_RWKV + N_TAIL
    out_ref = refs[i]
    o_s, zg_s, zr_s, gxb_ref, gst_ref, rxb_ref, rst_ref = refs[i + 1:i + 8]
    rwkv_s = refs[i + 8:]
    t = pl.program_id(0)
    cur = lax.rem(t, 2)

    @pl.when(t % steps_per_seq == 0)
    def _():
        gxb_ref[0:HALO, :] = jnp.zeros((HALO, gxb_ref.shape[1]), F32)
        gst_ref[...] = jnp.zeros_like(gst_ref)
        rxb_ref[0:HALO, :] = jnp.zeros((HALO, rxb_ref.shape[1]), F32)
        rst_ref[...] = jnp.zeros_like(rst_ref)

    @pl.when(t == 0)
    def _():
        o_s[...] = jnp.zeros_like(o_s)
        for _ in _in_proj_body(xp_ref[0], ln_ref, wg_ref, wr_ref, zg_s.at[0], zr_s.at[0]):
            pass

    res = _interleave({
        "T": _tail_body(xp_ref[0], o_s[...], pp_ref[0], *tail_p),
        "I": _in_proj_body(xn_ref[0], ln_ref, wg_ref, wr_ref, zg_s.at[1 - cur], zr_s.at[1 - cur]),
        "R": _rwkv_body(zr_s.at[cur], *rwkv_p, rxb_ref, rst_ref, *rwkv_s),
        "G": _gla_body(zg_s.at[cur], zr_s.at[cur, :, pl.ds(zr_s.shape[2] - MXU_WIDTH, MXU_WIDTH)], *gla_p,
                       gxb_ref, gst_ref, heads=heads, dk=dk, dv=dv),
    }, STAGE_ORDER)
    out_ref[0] = res["T"]
    o_s[...] = jnp.concatenate([res["G"], res["R"]], axis=1)


def _block(x, p, ln, wg_in, wr_in, gla_params, rwkv_params, tail_params, *, heads, dk, dv):
    bsz, s, d = x.shape
    tl = TILE
    steps_per_seq = s // tl
    nsteps = bsz * steps_per_seq
    kw, vw = heads * dk, heads * dv
    nqkv = 2 * kw + vw
    width = rwkv_params[1].shape[1]
    nchunks = tl // CHUNK
    tri = _chunk_tri(tl)
    hid = jnp.arange(LANES) // RWKV_HEAD
    bdp = (hid[:, None] == hid[None, :]).astype(BF16)
    gla_params = (*gla_params, tri)
    rwkv_params = (*rwkv_params, tri, bdp)
    assert (len(gla_params), len(rwkv_params), len(tail_params)) == (N_GLA, N_RWKV, N_TAIL)

    def nxt(t):
        tn = jnp.minimum(t + 1, nsteps - 1)
        return (tn // steps_per_seq, tn % steps_per_seq, 0)

    def prev(t):
        tp = jnp.maximum(t - 1, 0)
        return (tp // steps_per_seq, tp % steps_per_seq, 0)

    whole = lambda a: pl.BlockSpec(a.shape, lambda t: (0,) * a.ndim, pipeline_mode=pl.Buffered(1))
    consts = (ln, wg_in, wr_in, *gla_params, *rwkv_params, *tail_params)
    kern = functools.partial(_block_kernel, heads=heads, dk=dk, dv=dv, steps_per_seq=steps_per_seq)
    return pl.pallas_call(
        kern,
        grid=(nsteps + 1,),
        in_specs=[pl.BlockSpec((1, tl, d), nxt), pl.BlockSpec((1, tl, d), prev),
                  pl.BlockSpec((1, tl, p.shape[2]), prev)] + [whole(a) for a in consts],
        out_specs=pl.BlockSpec((1, tl, d), prev),
        out_shape=jax.ShapeDtypeStruct((bsz, s, d), F32),
        scratch_shapes=[
            pltpu.VMEM((tl, vw + width), BF16),
            pltpu.VMEM((2, tl, wg_in.shape[1]), F32),
            pltpu.VMEM((2, tl, wr_in.shape[1]), F32),
            pltpu.VMEM((tl + HALO, nqkv), F32),
            pltpu.VMEM((kw // LANES, 2 * dv, LANES), F32),
            pltpu.VMEM((tl + HALO, wr_in.shape[1]), F32),
            pltpu.VMEM((width // LANES, LANES, LANES), F32),
        ] + [pltpu.VMEM((tl, width), BF16)] * 7 + [
            pltpu.VMEM((tl, width), F32),
            pltpu.VMEM((nchunks, HALO, width), F32),
            pltpu.VMEM((tl, width), F32),
        ],
        compiler_params=pltpu.CompilerParams(
            dimension_semantics=("arbitrary",), vmem_limit_bytes=VMEM_LIMIT_BYTES),
        name="hybrid_block",
    )(x, x, p, *consts)


def _pad_cols(w, n):
    return jnp.pad(w, ((0, 0), (0, n - w.shape[1])))


def _place_rows(w, start, total):
    return jnp.pad(w, ((start, total - start - w.shape[0]), (0, 0)))


def _layer(h, p_i, ln_mix, w_in, conv_w, gk_w, gk_b, norm_g, mu, w0, w2, a0, a2, g2, k_k, k_a,
           r_k, gn_w, gn_b, w_out, ln_mlp, w_ff1, w_ff2, ln_ple, w_gate, w_proj, ln_next):
    row = lambda a: a.reshape(1, -1)

    gate_rank, kw = gk_w.shape
    dv = norm_g.shape[0]
    vw = (conv_w.shape[1] - 2 * kw)
    heads = vw // dv
    dk = kw // heads
    gla_main = 2 * kw + 2 * vw
    gla_cols = gla_main + gate_rank
    width = w0.shape[0]
    low_pad = MXU_WIDTH
    rwkv_pad = 3 * width + low_pad
    dr, ar, gr = w2.shape[0], a2.shape[0], g2.shape[0]
    assert gla_main % MXU_WIDTH == 0 and dr + ar + gr + gate_rank <= low_pad

    wg_in = w_in[:, :gla_main].astype(BF16)
    wr_in = _pad_cols(jnp.concatenate([w_in[:, gla_cols:], w_in[:, gla_main:gla_cols]], axis=1),
                      rwkv_pad).astype(BF16)
    gla_params = (conv_w, _place_rows(gk_w, dr + ar + gr, low_pad).astype(BF16), row(gk_b), row(norm_g))
    rwkv_params = (
        _pad_cols(row(mu), rwkv_pad), row(w0), _place_rows(w2, 0, low_pad).astype(BF16), row(a0),
        _place_rows(a2, dr, low_pad).astype(BF16), _place_rows(g2, dr + ar, low_pad).astype(BF16),
        row(k_k), row(k_a),
        row(r_k), row(gn_w), row(gn_b))
    tail_params = (w_out.astype(BF16), row(ln_mlp), w_ff1.astype(BF16), w_ff2.astype(BF16),
                   row(ln_ple), w_gate.astype(BF16), w_proj.astype(BF16), row(ln_next))
    return _block(h, p_i, row(ln_mix), wg_in, wr_in, gla_params, rwkv_params, tail_params,
                  heads=heads, dk=dk, dv=dv)


def kernel(x, p, ln_mix, w_in, gla_conv_w, gla_gk_w, gla_gk_b, gla_norm_g, rwkv_mu, rwkv_w0,
           rwkv_w2, rwkv_a0, rwkv_a2, rwkv_g2, rwkv_k_k, rwkv_k_a, rwkv_r_k, rwkv_gn_w,
           rwkv_gn_b, w_out, ln_mlp, w_ff1, w_ff2, ln_ple, w_ple_gate, w_ple_proj, ln_final):
    depth = w_in.shape[0]
    assert depth == 1, "the fused tail applies the final norm, so exactly one layer is supported"
    assert x.shape[1] % TILE == 0 and TILE % (CHUNK * RWKV_UNROLL) == 0
    i = 0
    return _layer(x, p.reshape(p.shape[1:]), ln_mix[i], w_in[i], gla_conv_w[i], gla_gk_w[i], gla_gk_b[i],
                  gla_norm_g[i], rwkv_mu[i], rwkv_w0[i], rwkv_w2[i], rwkv_a0[i], rwkv_a2[i],
                  rwkv_g2[i], rwkv_k_k[i], rwkv_k_a[i], rwkv_r_k[i].reshape(-1), rwkv_gn_w[i],
                  rwkv_gn_b[i], w_out[i], ln_mlp[i], w_ff1[i], w_ff2[i], ln_ple[i],
                  w_ple_gate[i], w_ple_proj[i], ln_final)
```

```python
import functools

import jax
import jax.numpy as jnp
from jax import lax
from jax.experimental import pallas as pl
from jax.experimental.pallas import tpu as pltpu

F32 = jnp.float32
BF16 = jnp.bfloat16

NORM_EPS = 1e-6
GLA_GATE_NORMALIZER = 16.0
RWKV_GN_EPS = 64e-5
RWKV_HEAD = 64
CHUNK = 64
LANES = 128
HALO = 8
TILE = 256
RWKV_UNROLL = 4
MXU_WIDTH = 256
FF_UP_PIECES = 8
FF_DOWN_PIECES = 4
IN_PIECES = 2
STAGE_ORDER = "TITTIRG" "TTIR" "TTRG" "TTIR" "TIRG" "TRTRTRTR"
VMEM_LIMIT_BYTES = 60 * 1024 * 1024


def _sigmoid(x):
    return 1.0 / (1.0 + jnp.exp(-x))


def _softplus(x):
    return jnp.maximum(x, 0.0) + jnp.log(1.0 + jnp.exp(-jnp.abs(x)))


def _dot(a, b):
    return jnp.dot(a, b, preferred_element_type=F32)


def _dot_nt(a, b):
    return lax.dot_general(a, b, (((1,), (1,)), ((), ())), preferred_element_type=F32)


def _dot_tn(a, b):
    return lax.dot_general(a, b, (((0,), (0,)), ((), ())), preferred_element_type=F32)


def _split2(x):
    hi = x.astype(BF16)
    lo = (x - hi.astype(F32)).astype(BF16)
    return hi, lo


def _dot_exact_lhs(a_bf16, x):
    hi, lo = _split2(x)
    return _dot(a_bf16, hi) + _dot(a_bf16, lo)


def _dot_wide_rhs(a, b_hi, b_lo):
    a = a.astype(BF16)
    return _dot(a, b_hi) + _dot(a, b_lo)


def _rmsnorm(x, g):
    ms = jnp.mean(x * x, axis=-1, keepdims=True)
    return x * lax.rsqrt(ms + NORM_EPS) * g


def _chunk_tri(tl):
    idx = jnp.arange(tl)
    same = idx[:, None] // CHUNK == idx[None, :] // CHUNK
    return ((idx[:, None] >= idx[None, :]) & same).astype(BF16)


def _gla_body(z_ref, low_ref, cw_ref, gkw_ref, gkb_ref, ng_ref, tri_ref, xb_ref, st_ref, *, heads, dk, dv):
    kw = heads * dk
    vw = heads * dv
    nqkv = 2 * kw + vw
    taps = cw_ref.shape[0]
    pairs = kw // LANES
    tl = z_ref.shape[0]
    nchunks = tl // CHUNK
    xb_ref[HALO:HALO + tl, :] = z_ref[:, :nqkv]
    conv = jnp.zeros((tl, nqkv), F32)
    for t in range(taps):
        off = HALO - (taps - 1) + t
        conv = conv + cw_ref[t:t + 1, :] * xb_ref[off:off + tl, :]
    xb_ref[0:HALO, :] = xb_ref[tl:tl + HALO, :]
    qkv = conv * _sigmoid(conv)
    q = qkv[:, :kw] * (dk ** -0.5)
    k = qkv[:, kw:2 * kw]
    v = qkv[:, 2 * kw:nqkv].astype(BF16)
    low = low_ref[...]

    gk_pre = _dot(low.astype(BF16), gkw_ref[...]) + gkb_ref[...]
    yield
    gk = -_softplus(-gk_pre) * (1.0 / GLA_GATE_NORMALIZER)
    b = _dot_exact_lhs(tri_ref[...], gk)
    yield
    last_rows = [b[c * CHUNK + CHUNK - 1:(c + 1) * CHUNK, :] for c in range(nchunks)]
    b_last = jnp.concatenate([jnp.broadcast_to(row, (CHUNK, kw)) for row in last_rows], axis=0)
    b_ref = 0.5 * b_last
    qe = (q * jnp.exp(b - b_ref)).astype(BF16)
    head0 = lax.rem(lax.broadcasted_iota(jnp.int32, (tl, kw), 1), LANES) < dk
    ke = k * jnp.exp(b_ref - b)
    ke0 = jnp.where(head0, ke, 0.0).astype(BF16)
    ke1 = jnp.where(head0, 0.0, ke).astype(BF16)
    qb = (q * jnp.exp(b)).astype(BF16)
    kd = (k * jnp.exp(b_last - b)).astype(BF16)
    e_last = [jnp.exp(row) for row in last_rows]

    lane = lax.broadcasted_iota(jnp.int32, (CHUNK, LANES), 1)
    trow = lax.broadcasted_iota(jnp.int32, (CHUNK, LANES), 0)
    causal = jnp.where(lane < dk, lane, lane - dk) <= trow
    vzero = jnp.zeros((CHUNK, dv), BF16)
    srow = lax.broadcasted_iota(jnp.int32, (2 * dv, LANES), 0)
    scol = lax.broadcasted_iota(jnp.int32, (2 * dv, LANES), 1)
    st_mask = ((srow < dv) == (scol < dk)).astype(F32)

    items = [(c, p) for c in range(nchunks) for p in range(pairs)]
    rs = [slice(c * CHUNK, (c + 1) * CHUNK) for c in range(nchunks)]
    ls = [slice(p * LANES, (p + 1) * LANES) for p in range(pairs)]
    vs = [slice(2 * p * dv, 2 * (p + 1) * dv) for p in range(pairs)]
    s = [_dot_nt(qe[rs[c], ls[p]], jnp.concatenate([ke0[rs[c], ls[p]], ke1[rs[c], ls[p]]], axis=0))
         for c, p in items]
    s = [jnp.where(causal, s_, 0.0).astype(BF16) for s_ in s]
    v_bd = [jnp.concatenate(
        [jnp.concatenate([v[rs[c], 2 * p * dv:(2 * p + 1) * dv], vzero], axis=1),
         jnp.concatenate([vzero, v[rs[c], (2 * p + 1) * dv:(2 * p + 2) * dv]], axis=1)], axis=0)
        for c, p in items]
    o_intra = [_dot(s_, v_) for s_, v_ in zip(s, v_bd)]
    kv = [_dot_tn(v[rs[c], vs[p]], kd[rs[c], ls[p]]) for c, p in items]
    st = [st_ref[p] for p in range(pairs)]
    o_rows = []
    for c in range(nchunks):
        o_c = []
        for p in range(pairs):
            i = c * pairs + p
            o_c.append(o_intra[i] + _dot_nt(qb[rs[c], ls[p]], st[p].astype(BF16)))
            st[p] = st[p] * e_last[c][:, ls[p]] + st_mask * kv[i]
        o_rows.append(jnp.concatenate(o_c, axis=1))
    for p in range(pairs):
        st_ref[p] = st[p]
    o = jnp.concatenate(o_rows, axis=0)
    o_n = []
    for h in range(heads):
        o_h = o[:, h * dv:(h + 1) * dv]
        ms = jnp.mean(o_h * o_h, axis=-1, keepdims=True)
        o_n.append(o_h * lax.rsqrt(ms + NORM_EPS) * ng_ref[...])
    g = z_ref[:, nqkv:nqkv + vw]
    return (jnp.concatenate(o_n, axis=1) * (g * _sigmoid(g))).astype(BF16)


def _headsum(x, bd_pair):
    rows, width = x.shape
    groups = width // LANES
    xb = x.astype(BF16)
    stacked = jnp.concatenate([xb[:, i * LANES:(i + 1) * LANES] for i in range(groups)], axis=0)
    s = _dot(stacked, bd_pair)
    return jnp.concatenate([s[i * rows:(i + 1) * rows] for i in range(groups)], axis=1)


def _rwkv_body(z_ref, mu_ref, w0_ref, w2_ref, a0_ref, a2_ref, g2_ref, kk_ref, ka_ref,
               rk_ref, gnw_ref, gnb_ref, tri_ref, bdp_ref,
               xb_ref, st_ref, at_s, rt_s, bt_s, kt_s, bh_s, kh_s, v_s, rtf_s, gam_s, y_s):
    n = RWKV_HEAD
    width = w0_ref.shape[1]
    pairs = width // LANES
    tl = z_ref.shape[0]
    nchunks = tl // CHUNK
    levels = (CHUNK - 1).bit_length()

    z = z_ref[...]
    xb_ref[HALO:HALO + tl, :] = z
    zp = xb_ref[HALO - 1:HALO - 1 + tl, :]
    xb_ref[0:HALO, :] = xb_ref[tl:tl + HALO, :]
    zs = z + mu_ref[...] * (zp - z)
    r = zs[:, 0:width]
    k = zs[:, width:2 * width]
    v = zs[:, 2 * width:3 * width]
    low = zs[:, 3 * width:]

    w_pre = w0_ref[...] + _dot(jnp.tanh(low).astype(BF16), w2_ref[...])
    logw = -jnp.exp(-_softplus(-w_pre) - 0.5)
    a_sig = _sigmoid(a0_ref[...] + _dot(low.astype(BF16), a2_ref[...]))
    g = _dot(_sigmoid(low).astype(BF16), g2_ref[...])
    yield

    bdp = bdp_ref[...]
    kk = k * kk_ref[...]
    kk_sq = _headsum(kk * kk, bdp)
    yield
    kk = kk / jnp.maximum(jnp.sqrt(kk_sq), 1e-12)
    kf = k * (1.0 + (a_sig - 1.0) * ka_ref[...])
    bv = kk * a_sig
    bonus = _headsum(r * kf * rk_ref[...], bdp)
    yield

    lg = _dot_exact_lhs(tri_ref[...], logw)
    yield
    last_rows = [lg[c * CHUNK + CHUNK - 1:(c + 1) * CHUNK, :] for c in range(nchunks)]
    lg_last = jnp.concatenate([jnp.broadcast_to(row, (CHUNK, width)) for row in last_rows], axis=0)
    for c, row in enumerate(last_rows):
        gam_s[c] = jnp.broadcast_to(jnp.exp(row), (HALO, width))
    e_inv = jnp.exp(-lg)
    e_rem = jnp.exp(lg_last - lg)
    rt = r * jnp.exp(lg)
    at_s[...] = (-kk * jnp.exp(lg - logw)).astype(BF16)
    rt_s[...] = rt.astype(BF16)
    rtf_s[...] = rt
    bt_s[...] = (bv * e_inv).astype(BF16)
    kt_s[...] = (kf * e_inv).astype(BF16)
    bh_s[...] = (bv * e_rem).astype(BF16)
    kh_s[...] = (kf * e_rem).astype(BF16)
    v_s[...] = v.astype(BF16)
    yield

    lane = lax.broadcasted_iota(jnp.int32, (CHUNK, LANES), 1)
    trow = lax.broadcasted_iota(jnp.int32, (CHUNK, LANES), 0)
    tcol = jnp.where(lane < n, lane, lane - n)
    incl, strict = tcol <= trow, tcol < trow
    eye2 = (tcol == trow).astype(F32)
    m0 = (lane < n).astype(BF16)
    m1 = (lane >= n).astype(BF16)
    m0w = jnp.concatenate([m0, m0], axis=1)
    m1w = jnp.concatenate([m1, m1], axis=1)
    prow = lax.broadcasted_iota(jnp.int32, (LANES, LANES), 0)
    pcol = lax.broadcasted_iota(jnp.int32, (LANES, LANES), 1)
    blk = ((prow < n) == (pcol < n)).astype(F32)
    eye = (prow == pcol).astype(F32)
    lsl = [slice(p * LANES, (p + 1) * LANES) for p in range(pairs)]

    def per_head(xb, wide=False):
        return jnp.concatenate([xb * (m0w if wide else m0), xb * (m1w if wide else m1)], axis=0)

    for it in range(nchunks // RWKV_UNROLL):
        items = [(u, p) for u in range(RWKV_UNROLL) for p in range(pairs)]
        rows = [pl.ds((it * RWKV_UNROLL + u) * CHUNK, CHUNK) for u in range(RWKV_UNROLL)]
        at_b = [at_s[rows[u], lsl[p]] for u, p in items]
        v_b = [v_s[rows[u], lsl[p]] for u, p in items]
        lhs = [jnp.concatenate([a_, rt_s[rows[u], lsl[p]]], axis=0)
               for a_, (u, p) in zip(at_b, items)]
        sb = [_dot_nt(l_, per_head(bt_s[rows[u], lsl[p]])) for l_, (u, p) in zip(lhs, items)]
        sk = [_dot_nt(l_, per_head(kt_s[rows[u], lsl[p]])) for l_, (u, p) in zip(lhs, items)]
        yield
        a_ab = [jnp.where(strict, s_[:CHUNK], 0.0).astype(BF16) for s_ in sb]
        a_rb = [jnp.where(incl, s_[CHUNK:], 0.0).astype(BF16) for s_ in sb]
        a_kk = [jnp.concatenate([jnp.where(strict, s_[:CHUNK], 0.0),
                                 jnp.where(incl, s_[CHUNK:], 0.0)], axis=0).astype(BF16) for s_ in sk]
        akv = [_dot(a_, per_head(v_)) for a_, v_ in zip(a_kk, v_b)]
        yield
        pw = a_ab
        inv = [eye2 + p_.astype(F32) for p_ in pw]
        for lvl in range(1, levels):
            pw = [_dot(p_, per_head(p_)).astype(BF16) for p_ in pw]
            inv = [q_ + _dot(p_, per_head(q_.astype(BF16))) for q_, p_ in zip(inv, pw)]
            yield
        x0 = [jnp.concatenate([a_, k_[:CHUNK].astype(BF16)], axis=1) for a_, k_ in zip(at_b, akv)]
        xb = [_dot(q_.astype(BF16), per_head(x_, wide=True)).astype(BF16)
              for q_, x_ in zip(inv, x0)]
        zz = [_dot(a_, per_head(x_, wide=True)) for a_, x_ in zip(a_rb, xb)]
        mn = []
        for i, (u, p) in enumerate(items):
            lb = jnp.concatenate([bh_s[rows[u], lsl[p]], kh_s[rows[u], lsl[p]]], axis=0)
            rb = jnp.concatenate(
                [xb[i], jnp.concatenate([jnp.zeros_like(v_b[i]), v_b[i]], axis=1)], axis=0)
            mn.append(_dot_tn(lb, rb))
        yield
        t_cur = [st_ref[p] for p in range(pairs)]
        for i, (u, p) in enumerate(items):
            ls = lsl[p]
            gam = gam_s[it * RWKV_UNROLL + u][0:1, ls]
            rhat = rtf_s[rows[u], ls] + zz[i][:, :LANES]
            yzero = zz[i][:, LANES:] + akv[i][CHUNK:]
            m_mat = blk * mn[i][:, :LANES] + eye * gam
            n_mat = blk * mn[i][:, LANES:]
            t_hi, t_lo = _split2(t_cur[p])
            y_s[rows[u], ls] = _dot(rhat.astype(BF16), t_hi) + yzero
            t_cur[p] = _dot_wide_rhs(m_mat, t_hi, t_lo) + n_mat
        for p in range(pairs):
            st_ref[p] = t_cur[p]
        yield

    y = y_s[...]
    mean = _headsum(y, bdp) * (1.0 / n)
    d = y - mean
    var = _headsum(d * d, bdp) * (1.0 / n)
    yn = d * lax.rsqrt(var + RWKV_GN_EPS) * gnw_ref[...] + gnb_ref[...]
    return ((yn + bonus * v) * g).astype(BF16)


def _tail_body(x, o, p, wo_ref, lnm_ref, w1_ref, w2_ref, lnp_ref, wg_ref, wp_ref, lnf_ref):
    h = x + _dot(o, wo_ref[...])
    n1 = _rmsnorm(h, lnm_ref[...]).astype(BF16)
    yield
    d_ff, d = w1_ref.shape[1], w2_ref.shape[1]
    hid = []
    for cs in _lane_blocks(d_ff, FF_UP_PIECES):
        a = jnp.maximum(_dot(n1, w1_ref[:, cs]).astype(BF16), 0.0)
        hid.append(a * a)
        yield
    hid = jnp.concatenate(hid, axis=1)
    down = []
    for cs in _lane_blocks(d, FF_DOWN_PIECES):
        down.append(_dot(hid, w2_ref[:, cs]))
        yield
    h = h + jnp.concatenate(down, axis=1)
    n2 = _rmsnorm(h, lnp_ref[...]).astype(BF16)
    gate = _sigmoid(_dot(n2, wg_ref[...]))
    h = h + gate * _dot(p.astype(BF16), wp_ref[...])
    return _rmsnorm(h, lnf_ref[...])


N_GLA, N_RWKV, N_TAIL = 5, 13, 8


def _lane_blocks(n, k):
    groups = n // MXU_WIDTH
    cuts = [(groups * i // k) * MXU_WIDTH for i in range(k)] + [n]
    return [slice(a, b) for a, b in zip(cuts[:-1], cuts[1:])]


def _in_proj_body(x, ln_ref, wg_ref, wr_ref, zg_ref, zr_ref):
    xn = _rmsnorm(x, ln_ref[...]).astype(BF16)
    for w_ref, z_ref in ((wg_ref, zg_ref), (wr_ref, zr_ref)):
        for cs in _lane_blocks(w_ref.shape[1], IN_PIECES):
            yield
            z_ref[:, cs] = _dot(xn, w_ref[:, cs])


def _interleave(gens, order):
    results = {}
    live = dict(gens)

    def advance(name):
        if name in live:
            try:
                next(live[name])
            except StopIteration as done:
                results[name] = done.value
                del live[name]

    for name in order:
        advance(name)
    while live:
        for name in list(live):
            advance(name)
    return results


def _block_kernel(*refs, heads, dk, dv, steps_per_seq):
    xn_ref, xp_ref, pp_ref, ln_ref, wg_ref, wr_ref = refs[:6]
    i = 6
    gla_p, rwkv_p, tail_p = refs[i:i + N_GLA], refs[i + N_GLA:i + N_GLA + N_RWKV], \
        refs[i + N_GLA + N_RWKV:i + N_GLA + N_RWKV + N_TAIL]
    i += N_GLA + N_RWKV + N_TAIL
    out_ref = refs[i]
    o_s, zg_s, zr_s, gxb_ref, gst_ref, rxb_ref, rst_ref = refs[i + 1:i + 8]
    rwkv_s = refs[i + 8:]
    t = pl.program_id(0)
    cur = lax.rem(t, 2)

    @pl.when(t % steps_per_seq == 0)
    def _():
        gxb_ref[0:HALO, :] = jnp.zeros((HALO, gxb_ref.shape[1]), F32)
        gst_ref[...] = jnp.zeros_like(gst_ref)
        rxb_ref[0:HALO, :] = jnp.zeros((HALO, rxb_ref.shape[1]), F32)
        rst_ref[...] = jnp.zeros_like(rst_ref)

    @pl.when(t == 0)
    def _():
        o_s[...] = jnp.zeros_like(o_s)
        for _ in _in_proj_body(xp_ref[0], ln_ref, wg_ref, wr_ref, zg_s.at[0], zr_s.at[0]):
            pass

    res = _interleave({
        "T": _tail_body(xp_ref[0], o_s[...], pp_ref[0], *tail_p),
        "I": _in_proj_body(xn_ref[0], ln_ref, wg_ref, wr_ref, zg_s.at[1 - cur], zr_s.at[1 - cur]),
        "R": _rwkv_body(zr_s.at[cur], *rwkv_p, rxb_ref, rst_ref, *rwkv_s),
        "G": _gla_body(zg_s.at[cur], zr_s.at[cur, :, pl.ds(zr_s.shape[2] - MXU_WIDTH, MXU_WIDTH)], *gla_p,
                       gxb_ref, gst_ref, heads=heads, dk=dk, dv=dv),
    }, STAGE_ORDER)
    out_ref[0] = res["T"]
    o_s[...] = jnp.concatenate([res["G"], res["R"]], axis=1)


def _block(x, p, ln, wg_in, wr_in, gla_params, rwkv_params, tail_params, *, heads, dk, dv):
    bsz, s, d = x.shape
    tl = TILE
    steps_per_seq = s // tl
    nsteps = bsz * steps_per_seq
    kw, vw = heads * dk, heads * dv
    nqkv = 2 * kw + vw
    width = rwkv_params[1].shape[1]
    nchunks = tl // CHUNK
    tri = _chunk_tri(tl)
    hid = jnp.arange(LANES) // RWKV_HEAD
    bdp = (hid[:, None] == hid[None, :]).astype(BF16)
    gla_params = (*gla_params, tri)
    rwkv_params = (*rwkv_params, tri, bdp)
    assert (len(gla_params), len(rwkv_params), len(tail_params)) == (N_GLA, N_RWKV, N_TAIL)

    def nxt(t):
        tn = jnp.minimum(t + 1, nsteps - 1)
        return (tn // steps_per_seq, tn % steps_per_seq, 0)

    def prev(t):
        tp = jnp.maximum(t - 1, 0)
        return (tp // steps_per_seq, tp % steps_per_seq, 0)

    whole = lambda a: pl.BlockSpec(a.shape, lambda t: (0,) * a.ndim, pipeline_mode=pl.Buffered(1))
    consts = (ln, wg_in, wr_in, *gla_params, *rwkv_params, *tail_params)
    kern = functools.partial(_block_kernel, heads=heads, dk=dk, dv=dv, steps_per_seq=steps_per_seq)
    return pl.pallas_call(
        kern,
        grid=(nsteps + 1,),
        in_specs=[pl.BlockSpec((1, tl, d), nxt), pl.BlockSpec((1, tl, d), prev),
                  pl.BlockSpec((1, tl, p.shape[2]), prev)] + [whole(a) for a in consts],
        out_specs=pl.BlockSpec((1, tl, d), prev),
        out_shape=jax.ShapeDtypeStruct((bsz, s, d), F32),
        scratch_shapes=[
            pltpu.VMEM((tl, vw + width), BF16),
            pltpu.VMEM((2, tl, wg_in.shape[1]), F32),
            pltpu.VMEM((2, tl, wr_in.shape[1]), F32),
            pltpu.VMEM((tl + HALO, nqkv), F32),
            pltpu.VMEM((kw // LANES, 2 * dv, LANES), F32),
            pltpu.VMEM((tl + HALO, wr_in.shape[1]), F32),
            pltpu.VMEM((width // LANES, LANES, LANES), F32),
        ] + [pltpu.VMEM((tl, width), BF16)] * 7 + [
            pltpu.VMEM((tl, width), F32),
            pltpu.VMEM((nchunks, HALO, width), F32),
            pltpu.VMEM((tl, width), F32),
        ],
        compiler_params=pltpu.CompilerParams(
            dimension_semantics=("arbitrary",), vmem_limit_bytes=VMEM_LIMIT_BYTES),
        name="hybrid_block",
    )(x, x, p, *consts)


def _pad_cols(w, n):
    return jnp.pad(w, ((0, 0), (0, n - w.shape[1])))


def _place_rows(w, start, total):
    return jnp.pad(w, ((start, total - start - w.shape[0]), (0, 0)))


def _layer(h, p_i, ln_mix, w_in, conv_w, gk_w, gk_b, norm_g, mu, w0, w2, a0, a2, g2, k_k, k_a,
           r_k, gn_w, gn_b, w_out, ln_mlp, w_ff1, w_ff2, ln_ple, w_gate, w_proj, ln_next):
    row = lambda a: a.reshape(1, -1)

    gate_rank, kw = gk_w.shape
    dv = norm_g.shape[0]
    vw = (conv_w.shape[1] - 2 * kw)
    heads = vw // dv
    dk = kw // heads
    gla_main = 2 * kw + 2 * vw
    gla_cols = gla_main + gate_rank
    width = w0.shape[0]
    low_pad = MXU_WIDTH
    rwkv_pad = 3 * width + low_pad
    dr, ar, gr = w2.shape[0], a2.shape[0], g2.shape[0]
    assert gla_main % MXU_WIDTH == 0 and dr + ar + gr + gate_rank <= low_pad

    w_in = w_in.astype(BF16)
    wg_in = w_in[:, :gla_main]
    wr_in = _pad_cols(jnp.concatenate([w_in[:, gla_cols:], w_in[:, gla_main:gla_cols]], axis=1), rwkv_pad)
    gla_params = (conv_w, _place_rows(gk_w, dr + ar + gr, low_pad).astype(BF16), row(gk_b), row(norm_g))
    rwkv_params = (
        _pad_cols(row(mu), rwkv_pad), row(w0), _place_rows(w2, 0, low_pad).astype(BF16), row(a0),
        _place_rows(a2, dr, low_pad).astype(BF16), _place_rows(g2, dr + ar, low_pad).astype(BF16),
        row(k_k), row(k_a),
        row(r_k), row(gn_w), row(gn_b))
    tail_params = (w_out.astype(BF16), row(ln_mlp), w_ff1.astype(BF16), w_ff2.astype(BF16),
                   row(ln_ple), w_gate.astype(BF16), w_proj.astype(BF16), row(ln_next))
    return _block(h, p_i, row(ln_mix), wg_in, wr_in, gla_params, rwkv_params, tail_params,
                  heads=heads, dk=dk, dv=dv)


def kernel(x, p, ln_mix, w_in, gla_conv_w, gla_gk_w, gla_gk_b, gla_norm_g, rwkv_mu, rwkv_w0,
           rwkv_w2, rwkv_a0, rwkv_a2, rwkv_g2, rwkv_k_k, rwkv_k_a, rwkv_r_k, rwkv_gn_w,
           rwkv_gn_b, w_out, ln_mlp, w_ff1, w_ff2, ln_ple, w_ple_gate, w_ple_proj, ln_final):
    depth = w_in.shape[0]
    assert depth == 1, "the fused tail applies the final norm, so exactly one layer is supported"
    assert x.shape[1] % TILE == 0 and TILE % (CHUNK * RWKV_UNROLL) == 0
    layer0 = lambda a: a.reshape(a.shape[1:])
    per_layer = [layer0(a) for a in (
        p, ln_mix, w_in, gla_conv_w, gla_gk_w, gla_gk_b, gla_norm_g, rwkv_mu, rwkv_w0, rwkv_w2, rwkv_a0,
        rwkv_a2, rwkv_g2, rwkv_k_k, rwkv_k_a, rwkv_r_k.reshape(depth, -1), rwkv_gn_w, rwkv_gn_b, w_out,
        ln_mlp, w_ff1, w_ff2, ln_ple, w_ple_gate, w_ple_proj)]
    return _layer(x, *per_layer, ln_final)
```

```python
import functools

import jax
import jax.numpy as jnp
from jax import lax
from jax.experimental import pallas as pl
from jax.experimental.pallas import tpu as pltpu

F32 = jnp.float32
BF16 = jnp.bfloat16

NORM_EPS = 1e-6
GLA_GATE_NORMALIZER = 16.0
RWKV_GN_EPS = 64e-5
RWKV_HEAD = 64
CHUNK = 64
GLA_CHUNK = 32
LANES = 128
HALO = 8
TILE = 256
RWKV_UNROLL = 4
MXU_WIDTH = 256
FF_UP_PIECES = 8
FF_DOWN_PIECES = 4
IN_PIECES = 2
STAGE_ORDER = "TITTIRG" "TTIR" "TTRG" "TTIR" "TIRG" "TRTRTRTR"
VMEM_LIMIT_BYTES = 60 * 1024 * 1024


def _sigmoid(x):
    return 1.0 / (1.0 + jnp.exp(-x))


def _softplus(x):
    return jnp.maximum(x, 0.0) + jnp.log(1.0 + jnp.exp(-jnp.abs(x)))


def _dot(a, b):
    return jnp.dot(a, b, preferred_element_type=F32)


def _dot_nt(a, b):
    return lax.dot_general(a, b, (((1,), (1,)), ((), ())), preferred_element_type=F32)


def _dot_tn(a, b):
    return lax.dot_general(a, b, (((0,), (0,)), ((), ())), preferred_element_type=F32)


def _split2(x):
    hi = x.astype(BF16)
    lo = (x - hi.astype(F32)).astype(BF16)
    return hi, lo


def _dot_exact_lhs(a_bf16, x):
    hi, lo = _split2(x)
    return _dot(a_bf16, hi) + _dot(a_bf16, lo)


def _dot_wide_rhs(a, b_hi, b_lo):
    a = a.astype(BF16)
    return _dot(a, b_hi) + _dot(a, b_lo)


def _rmsnorm(x, g):
    ms = jnp.mean(x * x, axis=-1, keepdims=True)
    return x * lax.rsqrt(ms + NORM_EPS) * g


def _chunk_tri(tl, chunk):
    idx = jnp.arange(tl)
    same = idx[:, None] // chunk == idx[None, :] // chunk
    return ((idx[:, None] >= idx[None, :]) & same).astype(BF16)


def _gla_body(z_ref, low_ref, cw_ref, gkw_ref, gkb_ref, ng_ref, tri_ref, xb_ref, st_ref, *, heads, dk, dv):
    kw = heads * dk
    vw = heads * dv
    nqkv = 2 * kw + vw
    taps = cw_ref.shape[0]
    pairs = kw // LANES
    tl = z_ref.shape[0]
    cg = GLA_CHUNK
    nchunks = tl // cg
    xb_ref[HALO:HALO + tl, :] = z_ref[:, :nqkv]
    conv = jnp.zeros((tl, nqkv), F32)
    for t in range(taps):
        off = HALO - (taps - 1) + t
        conv = conv + cw_ref[t:t + 1, :] * xb_ref[off:off + tl, :]
    xb_ref[0:HALO, :] = xb_ref[tl:tl + HALO, :]
    qkv = conv * _sigmoid(conv)
    q = qkv[:, :kw] * (dk ** -0.5)
    k = qkv[:, kw:2 * kw]
    v = qkv[:, 2 * kw:nqkv].astype(BF16)
    low = low_ref[...]

    gk_pre = _dot(low.astype(BF16), gkw_ref[...]) + gkb_ref[...]
    yield
    gk = -_softplus(-gk_pre) * (1.0 / GLA_GATE_NORMALIZER)
    b = _dot_exact_lhs(tri_ref[...], gk)
    yield
    last_rows = [b[c * cg + cg - 1:(c + 1) * cg, :] for c in range(nchunks)]
    b_last = jnp.concatenate([jnp.broadcast_to(row, (cg, kw)) for row in last_rows], axis=0)
    b_ref = 0.5 * b_last
    qe = (q * jnp.exp(b - b_ref)).astype(BF16)
    head0 = lax.rem(lax.broadcasted_iota(jnp.int32, (tl, kw), 1), LANES) < dk
    ke = k * jnp.exp(b_ref - b)
    ke0 = jnp.where(head0, ke, 0.0).astype(BF16)
    ke1 = jnp.where(head0, 0.0, ke).astype(BF16)
    qb = (q * jnp.exp(b)).astype(BF16)
    kd = (k * jnp.exp(b_last - b)).astype(BF16)
    e_last = [jnp.exp(row) for row in last_rows]

    lane = lax.broadcasted_iota(jnp.int32, (cg, 2 * cg), 1)
    trow = lax.broadcasted_iota(jnp.int32, (cg, 2 * cg), 0)
    causal = jnp.where(lane < cg, lane, lane - cg) <= trow
    vzero = jnp.zeros((cg, dv), BF16)
    srow = lax.broadcasted_iota(jnp.int32, (2 * dv, LANES), 0)
    scol = lax.broadcasted_iota(jnp.int32, (2 * dv, LANES), 1)
    st_mask = ((srow < dv) == (scol < dk)).astype(F32)

    items = [(c, p) for c in range(nchunks) for p in range(pairs)]
    rs = [slice(c * cg, (c + 1) * cg) for c in range(nchunks)]
    ls = [slice(p * LANES, (p + 1) * LANES) for p in range(pairs)]
    vs = [slice(2 * p * dv, 2 * (p + 1) * dv) for p in range(pairs)]
    s = [_dot_nt(qe[rs[c], ls[p]], jnp.concatenate([ke0[rs[c], ls[p]], ke1[rs[c], ls[p]]], axis=0))
         for c, p in items]
    s = [jnp.where(causal, s_, 0.0).astype(BF16) for s_ in s]
    v_bd = [jnp.concatenate(
        [jnp.concatenate([v[rs[c], 2 * p * dv:(2 * p + 1) * dv], vzero], axis=1),
         jnp.concatenate([vzero, v[rs[c], (2 * p + 1) * dv:(2 * p + 2) * dv]], axis=1)], axis=0)
        for c, p in items]
    o_intra = [_dot(s_, v_) for s_, v_ in zip(s, v_bd)]
    kv = [_dot_tn(v[rs[c], vs[p]], kd[rs[c], ls[p]]) for c, p in items]
    st = [st_ref[p] for p in range(pairs)]
    o_rows = []
    for c in range(nchunks):
        o_c = []
        for p in range(pairs):
            i = c * pairs + p
            o_c.append(o_intra[i] + _dot_nt(qb[rs[c], ls[p]], st[p].astype(BF16)))
            st[p] = st[p] * e_last[c][:, ls[p]] + st_mask * kv[i]
        o_rows.append(jnp.concatenate(o_c, axis=1))
    for p in range(pairs):
        st_ref[p] = st[p]
    o = jnp.concatenate(o_rows, axis=0)
    o_n = []
    for h in range(heads):
        o_h = o[:, h * dv:(h + 1) * dv]
        ms = jnp.mean(o_h * o_h, axis=-1, keepdims=True)
        o_n.append(o_h * lax.rsqrt(ms + NORM_EPS) * ng_ref[...])
    g = z_ref[:, nqkv:nqkv + vw]
    return (jnp.concatenate(o_n, axis=1) * (g * _sigmoid(g))).astype(BF16)


def _headsum(x, bd_pair):
    rows, width = x.shape
    groups = width // LANES
    xb = x.astype(BF16)
    stacked = jnp.concatenate([xb[:, i * LANES:(i + 1) * LANES] for i in range(groups)], axis=0)
    s = _dot(stacked, bd_pair)
    return jnp.concatenate([s[i * rows:(i + 1) * rows] for i in range(groups)], axis=1)


def _rwkv_body(z_ref, mu_ref, w0_ref, w2_ref, a0_ref, a2_ref, g2_ref, kk_ref, ka_ref,
               rk_ref, gnw_ref, gnb_ref, tri_ref, bdp_ref,
               xb_ref, st_ref, at_s, rt_s, bt_s, kt_s, bh_s, kh_s, v_s, rtf_s, gam_s, y_s):
    n = RWKV_HEAD
    width = w0_ref.shape[1]
    pairs = width // LANES
    tl = z_ref.shape[0]
    nchunks = tl // CHUNK
    levels = (CHUNK - 1).bit_length()

    z = z_ref[...]
    xb_ref[HALO:HALO + tl, :] = z
    zp = xb_ref[HALO - 1:HALO - 1 + tl, :]
    xb_ref[0:HALO, :] = xb_ref[tl:tl + HALO, :]
    zs = z + mu_ref[...] * (zp - z)
    r = zs[:, 0:width]
    k = zs[:, width:2 * width]
    v = zs[:, 2 * width:3 * width]
    low = zs[:, 3 * width:]

    w_pre = w0_ref[...] + _dot(jnp.tanh(low).astype(BF16), w2_ref[...])
    logw = -jnp.exp(-_softplus(-w_pre) - 0.5)
    a_sig = _sigmoid(a0_ref[...] + _dot(low.astype(BF16), a2_ref[...]))
    g = _dot(_sigmoid(low).astype(BF16), g2_ref[...])
    yield

    bdp = bdp_ref[...]
    kk = k * kk_ref[...]
    kk_sq = _headsum(kk * kk, bdp)
    yield
    kk = kk / jnp.maximum(jnp.sqrt(kk_sq), 1e-12)
    kf = k * (1.0 + (a_sig - 1.0) * ka_ref[...])
    bv = kk * a_sig
    bonus = _headsum(r * kf * rk_ref[...], bdp)
    yield

    lg = _dot_exact_lhs(tri_ref[...], logw)
    yield
    last_rows = [lg[c * CHUNK + CHUNK - 1:(c + 1) * CHUNK, :] for c in range(nchunks)]
    lg_last = jnp.concatenate([jnp.broadcast_to(row, (CHUNK, width)) for row in last_rows], axis=0)
    for c, row in enumerate(last_rows):
        gam_s[c] = jnp.broadcast_to(jnp.exp(row), (HALO, width))
    e_inv = jnp.exp(-lg)
    e_rem = jnp.exp(lg_last - lg)
    rt = r * jnp.exp(lg)
    at_s[...] = (-kk * jnp.exp(lg - logw)).astype(BF16)
    rt_s[...] = rt.astype(BF16)
    rtf_s[...] = rt
    bt_s[...] = (bv * e_inv).astype(BF16)
    kt_s[...] = (kf * e_inv).astype(BF16)
    bh_s[...] = (bv * e_rem).astype(BF16)
    kh_s[...] = (kf * e_rem).astype(BF16)
    v_s[...] = v.astype(BF16)
    yield

    lane = lax.broadcasted_iota(jnp.int32, (CHUNK, LANES), 1)
    trow = lax.broadcasted_iota(jnp.int32, (CHUNK, LANES), 0)
    tcol = jnp.where(lane < n, lane, lane - n)
    incl, strict = tcol <= trow, tcol < trow
    eye2 = (tcol == trow).astype(F32)
    m0 = (lane < n).astype(BF16)
    m1 = (lane >= n).astype(BF16)
    m0w = jnp.concatenate([m0, m0], axis=1)
    m1w = jnp.concatenate([m1, m1], axis=1)
    prow = lax.broadcasted_iota(jnp.int32, (LANES, LANES), 0)
    pcol = lax.broadcasted_iota(jnp.int32, (LANES, LANES), 1)
    blk = ((prow < n) == (pcol < n)).astype(F32)
    eye = (prow == pcol).astype(F32)
    lsl = [slice(p * LANES, (p + 1) * LANES) for p in range(pairs)]

    def per_head(xb, wide=False):
        return jnp.concatenate([xb * (m0w if wide else m0), xb * (m1w if wide else m1)], axis=0)

    for it in range(nchunks // RWKV_UNROLL):
        items = [(u, p) for u in range(RWKV_UNROLL) for p in range(pairs)]
        rows = [pl.ds((it * RWKV_UNROLL + u) * CHUNK, CHUNK) for u in range(RWKV_UNROLL)]
        at_b = [at_s[rows[u], lsl[p]] for u, p in items]
        v_b = [v_s[rows[u], lsl[p]] for u, p in items]
        lhs = [jnp.concatenate([a_, rt_s[rows[u], lsl[p]]], axis=0)
               for a_, (u, p) in zip(at_b, items)]
        sb = [_dot_nt(l_, per_head(bt_s[rows[u], lsl[p]])) for l_, (u, p) in zip(lhs, items)]
        sk = [_dot_nt(l_, per_head(kt_s[rows[u], lsl[p]])) for l_, (u, p) in zip(lhs, items)]
        yield
        a_ab = [jnp.where(strict, s_[:CHUNK], 0.0).astype(BF16) for s_ in sb]
        a_rb = [jnp.where(incl, s_[CHUNK:], 0.0).astype(BF16) for s_ in sb]
        a_kk = [jnp.concatenate([jnp.where(strict, s_[:CHUNK], 0.0),
                                 jnp.where(incl, s_[CHUNK:], 0.0)], axis=0).astype(BF16) for s_ in sk]
        akv = [_dot(a_, per_head(v_)) for a_, v_ in zip(a_kk, v_b)]
        yield
        pw = a_ab
        inv = [eye2 + p_.astype(F32) for p_ in pw]
        for lvl in range(1, levels):
            pw = [_dot(p_, per_head(p_)).astype(BF16) for p_ in pw]
            inv = [q_ + _dot(p_, per_head(q_.astype(BF16))) for q_, p_ in zip(inv, pw)]
            yield
        x0 = [jnp.concatenate([a_, k_[:CHUNK].astype(BF16)], axis=1) for a_, k_ in zip(at_b, akv)]
        xb = [_dot(q_.astype(BF16), per_head(x_, wide=True)).astype(BF16)
              for q_, x_ in zip(inv, x0)]
        zz = [_dot(a_, per_head(x_, wide=True)) for a_, x_ in zip(a_rb, xb)]
        mn = []
        for i, (u, p) in enumerate(items):
            lb = jnp.concatenate([bh_s[rows[u], lsl[p]], kh_s[rows[u], lsl[p]]], axis=0)
            rb = jnp.concatenate(
                [xb[i], jnp.concatenate([jnp.zeros_like(v_b[i]), v_b[i]], axis=1)], axis=0)
            mn.append(_dot_tn(lb, rb))
        yield
        t_cur = [st_ref[p] for p in range(pairs)]
        for i, (u, p) in enumerate(items):
            ls = lsl[p]
            gam = gam_s[it * RWKV_UNROLL + u][0:1, ls]
            rhat = rtf_s[rows[u], ls] + zz[i][:, :LANES]
            yzero = zz[i][:, LANES:] + akv[i][CHUNK:]
            m_mat = blk * mn[i][:, :LANES] + eye * gam
            n_mat = blk * mn[i][:, LANES:]
            t_hi, t_lo = _split2(t_cur[p])
            y_s[rows[u], ls] = _dot(rhat.astype(BF16), t_hi) + yzero
            t_cur[p] = _dot_wide_rhs(m_mat, t_hi, t_lo) + n_mat
        for p in range(pairs):
            st_ref[p] = t_cur[p]
        yield

    y = y_s[...]
    mean = _headsum(y, bdp) * (1.0 / n)
    d = y - mean
    var = _headsum(d * d, bdp) * (1.0 / n)
    yn = d * lax.rsqrt(var + RWKV_GN_EPS) * gnw_ref[...] + gnb_ref[...]
    return ((yn + bonus * v) * g).astype(BF16)


def _tail_body(x, o, p, wo_ref, lnm_ref, w1_ref, w2_ref, lnp_ref, wg_ref, wp_ref, lnf_ref):
    h = x + _dot(o, wo_ref[...])
    n1 = _rmsnorm(h, lnm_ref[...]).astype(BF16)
    yield
    d_ff, d = w1_ref.shape[1], w2_ref.shape[1]
    hid = []
    for cs in _lane_blocks(d_ff, FF_UP_PIECES):
        a = jnp.maximum(_dot(n1, w1_ref[:, cs]).astype(BF16), 0.0)
        hid.append(a * a)
        yield
    hid = jnp.concatenate(hid, axis=1)
    down = []
    for cs in _lane_blocks(d, FF_DOWN_PIECES):
        down.append(_dot(hid, w2_ref[:, cs]))
        yield
    h = h + jnp.concatenate(down, axis=1)
    n2 = _rmsnorm(h, lnp_ref[...]).astype(BF16)
    gate = _sigmoid(_dot(n2, wg_ref[...]))
    h = h + gate * _dot(p.astype(BF16), wp_ref[...])
    return _rmsnorm(h, lnf_ref[...])


N_GLA, N_RWKV, N_TAIL = 5, 13, 8


def _lane_blocks(n, k):
    groups = n // MXU_WIDTH
    cuts = [(groups * i // k) * MXU_WIDTH for i in range(k)] + [n]
    return [slice(a, b) for a, b in zip(cuts[:-1], cuts[1:])]


def _in_proj_body(x, ln_ref, wg_ref, wr_ref, zg_ref, zr_ref):
    xn = _rmsnorm(x, ln_ref[...]).astype(BF16)
    for w_ref, z_ref in ((wg_ref, zg_ref), (wr_ref, zr_ref)):
        for cs in _lane_blocks(w_ref.shape[1], IN_PIECES):
            yield
            z_ref[:, cs] = _dot(xn, w_ref[:, cs])


def _interleave(gens, order):
    results = {}
    live = dict(gens)

    def advance(name):
        if name in live:
            try:
                next(live[name])
            except StopIteration as done:
                results[name] = done.value
                del live[name]

    for name in order:
        advance(name)
    while live:
        for name in list(live):
            advance(name)
    return results


def _block_kernel(*refs, heads, dk, dv, steps_per_seq):
    xn_ref, xp_ref, pp_ref, ln_ref, wg_ref, wr_ref = refs[:6]
    i = 6
    gla_p, rwkv_p, tail_p = refs[i:i + N_GLA], refs[i + N_GLA:i + N_GLA + N_RWKV], \
        refs[i + N_GLA + N_RWKV:i + N_GLA + N_RWKV + N_TAIL]
    i += N_GLA + N_RWKV + N_TAIL
    out_ref = refs[i]
    o_s, zg_s, zr_s, gxb_ref, gst_ref, rxb_ref, rst_ref = refs[i + 1:i + 8]
    rwkv_s = refs[i + 8:]
    t = pl.program_id(0)
    cur = lax.rem(t, 2)

    @pl.when(t % steps_per_seq == 0)
    def _():
        gxb_ref[0:HALO, :] = jnp.zeros((HALO, gxb_ref.shape[1]), F32)
        gst_ref[...] = jnp.zeros_like(gst_ref)
        rxb_ref[0:HALO, :] = jnp.zeros((HALO, rxb_ref.shape[1]), F32)
        rst_ref[...] = jnp.zeros_like(rst_ref)

    @pl.when(t == 0)
    def _():
        o_s[...] = jnp.zeros_like(o_s)
        for _ in _in_proj_body(xp_ref[0], ln_ref, wg_ref, wr_ref, zg_s.at[0], zr_s.at[0]):
            pass

    res = _interleave({
        "T": _tail_body(xp_ref[0], o_s[...], pp_ref[0], *tail_p),
        "I": _in_proj_body(xn_ref[0], ln_ref, wg_ref, wr_ref, zg_s.at[1 - cur], zr_s.at[1 - cur]),
        "R": _rwkv_body(zr_s.at[cur], *rwkv_p, rxb_ref, rst_ref, *rwkv_s),
        "G": _gla_body(zg_s.at[cur], zr_s.at[cur, :, pl.ds(zr_s.shape[2] - MXU_WIDTH, MXU_WIDTH)], *gla_p,
                       gxb_ref, gst_ref, heads=heads, dk=dk, dv=dv),
    }, STAGE_ORDER)
    out_ref[0] = res["T"]
    o_s[...] = jnp.concatenate([res["G"], res["R"]], axis=1)


def _block(x, p, ln, wg_in, wr_in, gla_params, rwkv_params, tail_params, *, heads, dk, dv):
    bsz, s, d = x.shape
    tl = TILE
    steps_per_seq = s // tl
    nsteps = bsz * steps_per_seq
    kw, vw = heads * dk, heads * dv
    nqkv = 2 * kw + vw
    width = rwkv_params[1].shape[1]
    nchunks = tl // CHUNK
    hid = jnp.arange(LANES) // RWKV_HEAD
    bdp = (hid[:, None] == hid[None, :]).astype(BF16)
    gla_params = (*gla_params, _chunk_tri(tl, GLA_CHUNK))
    rwkv_params = (*rwkv_params, _chunk_tri(tl, CHUNK), bdp)
    assert (len(gla_params), len(rwkv_params), len(tail_params)) == (N_GLA, N_RWKV, N_TAIL)

    def nxt(t):
        tn = jnp.minimum(t + 1, nsteps - 1)
        return (tn // steps_per_seq, tn % steps_per_seq, 0)

    def prev(t):
        tp = jnp.maximum(t - 1, 0)
        return (tp // steps_per_seq, tp % steps_per_seq, 0)

    whole = lambda a: pl.BlockSpec(a.shape, lambda t: (0,) * a.ndim, pipeline_mode=pl.Buffered(1))
    consts = (ln, wg_in, wr_in, *gla_params, *rwkv_params, *tail_params)
    kern = functools.partial(_block_kernel, heads=heads, dk=dk, dv=dv, steps_per_seq=steps_per_seq)
    return pl.pallas_call(
        kern,
        grid=(nsteps + 1,),
        in_specs=[pl.BlockSpec((1, tl, d), nxt), pl.BlockSpec((1, tl, d), prev),
                  pl.BlockSpec((1, tl, p.shape[2]), prev)] + [whole(a) for a in consts],
        out_specs=pl.BlockSpec((1, tl, d), prev),
        out_shape=jax.ShapeDtypeStruct((bsz, s, d), F32),
        scratch_shapes=[
            pltpu.VMEM((tl, vw + width), BF16),
            pltpu.VMEM((2, tl, wg_in.shape[1]), F32),
            pltpu.VMEM((2, tl, wr_in.shape[1]), F32),
            pltpu.VMEM((tl + HALO, nqkv), F32),
            pltpu.VMEM((kw // LANES, 2 * dv, LANES), F32),
            pltpu.VMEM((tl + HALO, wr_in.shape[1]), F32),
            pltpu.VMEM((width // LANES, LANES, LANES), F32),
        ] + [pltpu.VMEM((tl, width), BF16)] * 7 + [
            pltpu.VMEM((tl, width), F32),
            pltpu.VMEM((nchunks, HALO, width), F32),
            pltpu.VMEM((tl, width), F32),
        ],
        compiler_params=pltpu.CompilerParams(
            dimension_semantics=("arbitrary",), vmem_limit_bytes=VMEM_LIMIT_BYTES),
        name="hybrid_block",
    )(x, x, p, *consts)


def _pad_cols(w, n):
    return jnp.pad(w, ((0, 0), (0, n - w.shape[1])))


def _place_rows(w, start, total):
    return jnp.pad(w, ((start, total - start - w.shape[0]), (0, 0)))


def _layer(h, p_i, ln_mix, w_in, conv_w, gk_w, gk_b, norm_g, mu, w0, w2, a0, a2, g2, k_k, k_a,
           r_k, gn_w, gn_b, w_out, ln_mlp, w_ff1, w_ff2, ln_ple, w_gate, w_proj, ln_next):
    row = lambda a: a.reshape(1, -1)

    gate_rank, kw = gk_w.shape
    dv = norm_g.shape[0]
    vw = (conv_w.shape[1] - 2 * kw)
    heads = vw // dv
    dk = kw // heads
    gla_main = 2 * kw + 2 * vw
    gla_cols = gla_main + gate_rank
    width = w0.shape[0]
    low_pad = MXU_WIDTH
    rwkv_pad = 3 * width + low_pad
    dr, ar, gr = w2.shape[0], a2.shape[0], g2.shape[0]
    assert gla_main % MXU_WIDTH == 0 and dr + ar + gr + gate_rank <= low_pad

    wg_in = w_in[:, :gla_main].astype(BF16)
    wr_in = _pad_cols(jnp.concatenate([w_in[:, gla_cols:], w_in[:, gla_main:gla_cols]], axis=1),
                      rwkv_pad).astype(BF16)
    gla_params = (conv_w, _place_rows(gk_w, dr + ar + gr, low_pad).astype(BF16), row(gk_b), row(norm_g))
    rwkv_params = (
        _pad_cols(row(mu), rwkv_pad), row(w0), _place_rows(w2, 0, low_pad).astype(BF16), row(a0),
        _place_rows(a2, dr, low_pad).astype(BF16), _place_rows(g2, dr + ar, low_pad).astype(BF16),
        row(k_k), row(k_a),
        row(r_k), row(gn_w), row(gn_b))
    tail_params = (w_out.astype(BF16), row(ln_mlp), w_ff1.astype(BF16), w_ff2.astype(BF16),
                   row(ln_ple), w_gate.astype(BF16), w_proj.astype(BF16), row(ln_next))
    return _block(h, p_i, row(ln_mix), wg_in, wr_in, gla_params, rwkv_params, tail_params,
                  heads=heads, dk=dk, dv=dv)


def kernel(x, p, ln_mix, w_in, gla_conv_w, gla_gk_w, gla_gk_b, gla_norm_g, rwkv_mu, rwkv_w0,
           rwkv_w2, rwkv_a0, rwkv_a2, rwkv_g2, rwkv_k_k, rwkv_k_a, rwkv_r_k, rwkv_gn_w,
           rwkv_gn_b, w_out, ln_mlp, w_ff1, w_ff2, ln_ple, w_ple_gate, w_ple_proj, ln_final):
    depth = w_in.shape[0]
    assert depth == 1, "the fused tail applies the final norm, so exactly one layer is supported"
    assert x.shape[1] % TILE == 0 and TILE % (CHUNK * RWKV_UNROLL) == 0
    i = 0
    return _layer(x, p.reshape(p.shape[1:]), ln_mix[i], w_in[i], gla_conv_w[i], gla_gk_w[i], gla_gk_b[i],
                  gla_norm_g[i], rwkv_mu[i], rwkv_w0[i], rwkv_w2[i], rwkv_a0[i], rwkv_a2[i],
                  rwkv_g2[i], rwkv_k_k[i], rwkv_k_a[i], rwkv_r_k[i].reshape(-1), rwkv_gn_w[i],
                  rwkv_gn_b[i], w_out[i], ln_mlp[i], w_ff1[i], w_ff2[i], ln_ple[i],
                  w_ple_gate[i], w_ple_proj[i], ln_final)
```

```python
import functools

import jax
import jax.numpy as jnp
from jax import lax
from jax.experimental import pallas as pl
from jax.experimental.pallas import tpu as pltpu

F32 = jnp.float32
BF16 = jnp.bfloat16

NORM_EPS = 1e-6
GLA_GATE_NORMALIZER = 16.0
RWKV_GN_EPS = 64e-5
RWKV_HEAD = 64
CHUNK = 64
LANES = 128
HALO = 8
TILE = 256
RWKV_UNROLL = 4
MXU_WIDTH = 256
FF_UP_PIECES = 8
FF_DOWN_PIECES = 4
IN_PIECES = 2
STAGE_ORDER = "TITTIRG" "TTIR" "TTRG" "TTIR" "TIRG" "TRTRTRTR"
VMEM_LIMIT_BYTES = 60 * 1024 * 1024


def _sigmoid(x):
    return 1.0 / (1.0 + jnp.exp(-x))


def _softplus(x):
    return jnp.maximum(x, 0.0) + jnp.log(1.0 + jnp.exp(-jnp.abs(x)))


def _dot(a, b):
    return jnp.dot(a, b, preferred_element_type=F32)


def _dot_nt(a, b):
    return lax.dot_general(a, b, (((1,), (1,)), ((), ())), preferred_element_type=F32)


def _dot_tn(a, b):
    return lax.dot_general(a, b, (((0,), (0,)), ((), ())), preferred_element_type=F32)


def _split2(x):
    hi = x.astype(BF16)
    lo = (x - hi.astype(F32)).astype(BF16)
    return hi, lo


def _dot_exact_lhs(a_bf16, x):
    hi, lo = _split2(x)
    return _dot(a_bf16, hi) + _dot(a_bf16, lo)


def _dot_wide_rhs(a, b_hi, b_lo):
    a = a.astype(BF16)
    return _dot(a, b_hi) + _dot(a, b_lo)


def _rmsnorm(x, g):
    ms = jnp.mean(x * x, axis=-1, keepdims=True)
    return x * lax.rsqrt(ms + NORM_EPS) * g


def _chunk_tri(tl):
    idx = jnp.arange(tl)
    same = idx[:, None] // CHUNK == idx[None, :] // CHUNK
    return ((idx[:, None] >= idx[None, :]) & same).astype(BF16)


def _gla_body(z_ref, low_ref, cw_ref, gkw_ref, gkb_ref, ng_ref, tri_ref, xb_ref, st_ref, *, heads, dk, dv):
    kw = heads * dk
    vw = heads * dv
    nqkv = 2 * kw + vw
    taps = cw_ref.shape[0]
    pairs = kw // LANES
    tl = z_ref.shape[0]
    nchunks = tl // CHUNK
    xb_ref[HALO:HALO + tl, :] = z_ref[:, :nqkv]
    conv = jnp.zeros((tl, nqkv), F32)
    for t in range(taps):
        off = HALO - (taps - 1) + t
        conv = conv + cw_ref[t:t + 1, :] * xb_ref[off:off + tl, :]
    xb_ref[0:HALO, :] = xb_ref[tl:tl + HALO, :]
    qkv = conv * _sigmoid(conv)
    q = qkv[:, :kw] * (dk ** -0.5)
    k = qkv[:, kw:2 * kw]
    v = qkv[:, 2 * kw:nqkv].astype(BF16)
    low = low_ref[...]

    gk_pre = _dot(low.astype(BF16), gkw_ref[...]) + gkb_ref[...]
    yield
    gk = -_softplus(-gk_pre) * (1.0 / GLA_GATE_NORMALIZER)
    b = _dot_exact_lhs(tri_ref[...], gk)
    yield
    last_rows = [b[c * CHUNK + CHUNK - 1:(c + 1) * CHUNK, :] for c in range(nchunks)]
    b_last = jnp.concatenate([jnp.broadcast_to(row, (CHUNK, kw)) for row in last_rows], axis=0)
    b_ref = 0.5 * b_last
    qe = (q * jnp.exp(b - b_ref)).astype(BF16)
    head0 = lax.rem(lax.broadcasted_iota(jnp.int32, (tl, kw), 1), LANES) < dk
    ke = k * jnp.exp(b_ref - b)
    ke0 = jnp.where(head0, ke, 0.0).astype(BF16)
    ke1 = jnp.where(head0, 0.0, ke).astype(BF16)
    qb = (q * jnp.exp(b)).astype(BF16)
    kd = (k * jnp.exp(b_last - b)).astype(BF16)
    e_last = [jnp.exp(row) for row in last_rows]

    lane = lax.broadcasted_iota(jnp.int32, (CHUNK, LANES), 1)
    trow = lax.broadcasted_iota(jnp.int32, (CHUNK, LANES), 0)
    causal = jnp.where(lane < dk, lane, lane - dk) <= trow
    vzero = jnp.zeros((CHUNK, dv), BF16)
    srow = lax.broadcasted_iota(jnp.int32, (2 * dv, LANES), 0)
    scol = lax.broadcasted_iota(jnp.int32, (2 * dv, LANES), 1)
    st_mask = ((srow < dv) == (scol < dk)).astype(F32)

    items = [(c, p) for c in range(nchunks) for p in range(pairs)]
    rs = [slice(c * CHUNK, (c + 1) * CHUNK) for c in range(nchunks)]
    ls = [slice(p * LANES, (p + 1) * LANES) for p in range(pairs)]
    vs = [slice(2 * p * dv, 2 * (p + 1) * dv) for p in range(pairs)]
    s = [_dot_nt(qe[rs[c], ls[p]], jnp.concatenate([ke0[rs[c], ls[p]], ke1[rs[c], ls[p]]], axis=0))
         for c, p in items]
    s = [jnp.where(causal, s_, 0.0).astype(BF16) for s_ in s]
    v_bd = [jnp.concatenate(
        [jnp.concatenate([v[rs[c], 2 * p * dv:(2 * p + 1) * dv], vzero], axis=1),
         jnp.concatenate([vzero, v[rs[c], (2 * p + 1) * dv:(2 * p + 2) * dv]], axis=1)], axis=0)
        for c, p in items]
    o_intra = [_dot(s_, v_) for s_, v_ in zip(s, v_bd)]
    kv = [_dot_tn(v[rs[c], vs[p]], kd[rs[c], ls[p]]) for c, p in items]
    st = [st_ref[p] for p in range(pairs)]
    o_rows = []
    for c in range(nchunks):
        o_c = []
        for p in range(pairs):
            i = c * pairs + p
            o_c.append(o_intra[i] + _dot_nt(qb[rs[c], ls[p]], st[p].astype(BF16)))
            st[p] = st[p] * e_last[c][:, ls[p]] + st_mask * kv[i]
        o_rows.append(jnp.concatenate(o_c, axis=1))
    for p in range(pairs):
        st_ref[p] = st[p]
    o = jnp.concatenate(o_rows, axis=0)
    o_n = []
    for h in range(heads):
        o_h = o[:, h * dv:(h + 1) * dv]
        ms = jnp.mean(o_h * o_h, axis=-1, keepdims=True)
        o_n.append(o_h * lax.rsqrt(ms + NORM_EPS) * ng_ref[...])
    g = z_ref[:, nqkv:nqkv + vw]
    return (jnp.concatenate(o_n, axis=1) * (g * _sigmoid(g))).astype(BF16)


def _headsum(x, bd_pair):
    rows, width = x.shape
    groups = width // LANES
    xb = x.astype(BF16)
    stacked = jnp.concatenate([xb[:, i * LANES:(i + 1) * LANES] for i in range(groups)], axis=0)
    s = _dot(stacked, bd_pair)
    return jnp.concatenate([s[i * rows:(i + 1) * rows] for i in range(groups)], axis=1)


def _rwkv_body(z_ref, mu_ref, w0_ref, w2_ref, a0_ref, a2_ref, g2_ref, kk_ref, ka_ref,
               rk_ref, gnw_ref, gnb_ref, tri_ref, bdp_ref,
               xb_ref, st_ref, at_s, rt_s, bt_s, kt_s, bh_s, kh_s, v_s, rtf_s, gam_s, y_s):
    n = RWKV_HEAD
    width = w0_ref.shape[1]
    pairs = width // LANES
    tl = z_ref.shape[0]
    nchunks = tl // CHUNK
    levels = (CHUNK - 1).bit_length()

    z = z_ref[...]
    xb_ref[HALO:HALO + tl, :] = z
    zp = xb_ref[HALO - 1:HALO - 1 + tl, :]
    xb_ref[0:HALO, :] = xb_ref[tl:tl + HALO, :]
    zs = z + mu_ref[...] * (zp - z)
    r = zs[:, 0:width]
    k = zs[:, width:2 * width]
    v = zs[:, 2 * width:3 * width]
    low = zs[:, 3 * width:]

    w_pre = w0_ref[...] + _dot(jnp.tanh(low).astype(BF16), w2_ref[...])
    logw = -jnp.exp(-_softplus(-w_pre) - 0.5)
    a_sig = _sigmoid(a0_ref[...] + _dot(low.astype(BF16), a2_ref[...]))
    g = _dot(_sigmoid(low).astype(BF16), g2_ref[...])
    yield

    bdp = bdp_ref[...]
    kk = k * kk_ref[...]
    kf = k * (1.0 + (a_sig - 1.0) * ka_ref[...])
    sums = _headsum(jnp.concatenate([kk * kk, r * kf * rk_ref[...]], axis=0), bdp)
    yield
    kk = kk / jnp.maximum(jnp.sqrt(sums[:tl]), 1e-12)
    bv = kk * a_sig
    bonus = sums[tl:]
    yield

    lg = _dot_exact_lhs(tri_ref[...], logw)
    yield
    last_rows = [lg[c * CHUNK + CHUNK - 1:(c + 1) * CHUNK, :] for c in range(nchunks)]
    lg_last = jnp.concatenate([jnp.broadcast_to(row, (CHUNK, width)) for row in last_rows], axis=0)
    for c, row in enumerate(last_rows):
        gam_s[c] = jnp.broadcast_to(jnp.exp(row), (HALO, width))
    e_inv = jnp.exp(-lg)
    e_rem = jnp.exp(lg_last - lg)
    rt = r * jnp.exp(lg)
    at_s[...] = (-kk * jnp.exp(lg - logw)).astype(BF16)
    rt_s[...] = rt.astype(BF16)
    rtf_s[...] = rt
    bt_s[...] = (bv * e_inv).astype(BF16)
    kt_s[...] = (kf * e_inv).astype(BF16)
    bh_s[...] = (bv * e_rem).astype(BF16)
    kh_s[...] = (kf * e_rem).astype(BF16)
    v_s[...] = v.astype(BF16)
    yield

    lane = lax.broadcasted_iota(jnp.int32, (CHUNK, LANES), 1)
    trow = lax.broadcasted_iota(jnp.int32, (CHUNK, LANES), 0)
    tcol = jnp.where(lane < n, lane, lane - n)
    incl, strict = tcol <= trow, tcol < trow
    eye2 = (tcol == trow).astype(F32)
    m0 = (lane < n).astype(BF16)
    m1 = (lane >= n).astype(BF16)
    m0w = jnp.concatenate([m0, m0], axis=1)
    m1w = jnp.concatenate([m1, m1], axis=1)
    prow = lax.broadcasted_iota(jnp.int32, (LANES, LANES), 0)
    pcol = lax.broadcasted_iota(jnp.int32, (LANES, LANES), 1)
    blk = ((prow < n) == (pcol < n)).astype(F32)
    eye = (prow == pcol).astype(F32)
    lsl = [slice(p * LANES, (p + 1) * LANES) for p in range(pairs)]

    def per_head(xb, wide=False):
        return jnp.concatenate([xb * (m0w if wide else m0), xb * (m1w if wide else m1)], axis=0)

    for it in range(nchunks // RWKV_UNROLL):
        items = [(u, p) for u in range(RWKV_UNROLL) for p in range(pairs)]
        rows = [pl.ds((it * RWKV_UNROLL + u) * CHUNK, CHUNK) for u in range(RWKV_UNROLL)]
        at_b = [at_s[rows[u], lsl[p]] for u, p in items]
        v_b = [v_s[rows[u], lsl[p]] for u, p in items]
        lhs = [jnp.concatenate([a_, rt_s[rows[u], lsl[p]]], axis=0)
               for a_, (u, p) in zip(at_b, items)]
        sb = [_dot_nt(l_, per_head(bt_s[rows[u], lsl[p]])) for l_, (u, p) in zip(lhs, items)]
        sk = [_dot_nt(l_, per_head(kt_s[rows[u], lsl[p]])) for l_, (u, p) in zip(lhs, items)]
        yield
        a_ab = [jnp.where(strict, s_[:CHUNK], 0.0).astype(BF16) for s_ in sb]
        a_rb = [jnp.where(incl, s_[CHUNK:], 0.0).astype(BF16) for s_ in sb]
        a_kk = [jnp.concatenate([jnp.where(strict, s_[:CHUNK], 0.0),
                                 jnp.where(incl, s_[CHUNK:], 0.0)], axis=0).astype(BF16) for s_ in sk]
        akv = [_dot(a_, per_head(v_)) for a_, v_ in zip(a_kk, v_b)]
        yield
        pw = a_ab
        inv = [eye2 + p_.astype(F32) for p_ in pw]
        for lvl in range(1, levels):
            pw = [_dot(p_, per_head(p_)).astype(BF16) for p_ in pw]
            inv = [q_ + _dot(p_, per_head(q_.astype(BF16))) for q_, p_ in zip(inv, pw)]
            yield
        x0 = [jnp.concatenate([a_, k_[:CHUNK].astype(BF16)], axis=1) for a_, k_ in zip(at_b, akv)]
        xb = [_dot(q_.astype(BF16), per_head(x_, wide=True)).astype(BF16)
              for q_, x_ in zip(inv, x0)]
        zz = [_dot(a_, per_head(x_, wide=True)) for a_, x_ in zip(a_rb, xb)]
        mn = []
        for i, (u, p) in enumerate(items):
            lb = jnp.concatenate([bh_s[rows[u], lsl[p]], kh_s[rows[u], lsl[p]]], axis=0)
            rb = jnp.concatenate(
                [xb[i], jnp.concatenate([jnp.zeros_like(v_b[i]), v_b[i]], axis=1)], axis=0)
            mn.append(_dot_tn(lb, rb))
        yield
        t_cur = [st_ref[p] for p in range(pairs)]
        for i, (u, p) in enumerate(items):
            ls = lsl[p]
            gam = gam_s[it * RWKV_UNROLL + u][0:1, ls]
            rhat = rtf_s[rows[u], ls] + zz[i][:, :LANES]
            yzero = zz[i][:, LANES:] + akv[i][CHUNK:]
            m_mat = blk * mn[i][:, :LANES] + eye * gam
            n_mat = blk * mn[i][:, LANES:]
            t_hi, t_lo = _split2(t_cur[p])
            y_s[rows[u], ls] = _dot(rhat.astype(BF16), t_hi) + yzero
            t_cur[p] = _dot_wide_rhs(m_mat, t_hi, t_lo) + n_mat
        for p in range(pairs):
            st_ref[p] = t_cur[p]
        yield

    y = y_s[...]
    moments = _headsum(jnp.concatenate([y, y * y], axis=0), bdp) * (1.0 / n)
    mean = moments[:tl]
    var = moments[tl:] - mean * mean
    yn = (y - mean) * lax.rsqrt(var + RWKV_GN_EPS) * gnw_ref[...] + gnb_ref[...]
    return ((yn + bonus * v) * g).astype(BF16)


def _tail_body(x, o, p, wo_ref, lnm_ref, w1_ref, w2_ref, lnp_ref, wg_ref, wp_ref, lnf_ref):
    h = x + _dot(o, wo_ref[...])
    n1 = _rmsnorm(h, lnm_ref[...]).astype(BF16)
    yield
    d_ff, d = w1_ref.shape[1], w2_ref.shape[1]
    hid = []
    for cs in _lane_blocks(d_ff, FF_UP_PIECES):
        a = jnp.maximum(_dot(n1, w1_ref[:, cs]).astype(BF16), 0.0)
        hid.append(a * a)
        yield
    hid = jnp.concatenate(hid, axis=1)
    down = []
    for cs in _lane_blocks(d, FF_DOWN_PIECES):
        down.append(_dot(hid, w2_ref[:, cs]))
        yield
    h = h + jnp.concatenate(down, axis=1)
    n2 = _rmsnorm(h, lnp_ref[...]).astype(BF16)
    gate = _sigmoid(_dot(n2, wg_ref[...]))
    h = h + gate * _dot(p.astype(BF16), wp_ref[...])
    return _rmsnorm(h, lnf_ref[...])


N_GLA, N_RWKV, N_TAIL = 5, 13, 8


def _lane_blocks(n, k):
    groups = n // MXU_WIDTH
    cuts = [(groups * i // k) * MXU_WIDTH for i in range(k)] + [n]
    return [slice(a, b) for a, b in zip(cuts[:-1], cuts[1:])]


def _in_proj_body(x, ln_ref, wg_ref, wr_ref, zg_ref, zr_ref):
    xn = _rmsnorm(x, ln_ref[...]).astype(BF16)
    for w_ref, z_ref in ((wg_ref, zg_ref), (wr_ref, zr_ref)):
        for cs in _lane_blocks(w_ref.shape[1], IN_PIECES):
            yield
            z_ref[:, cs] = _dot(xn, w_ref[:, cs])


def _interleave(gens, order):
    results = {}
    live = dict(gens)

    def advance(name):
        if name in live:
            try:
                next(live[name])
            except StopIteration as done:
                results[name] = done.value
                del live[name]

    for name in order:
        advance(name)
    while live:
        for name in list(live):
            advance(name)
    return results


def _block_kernel(*refs, heads, dk, dv, steps_per_seq):
    xn_ref, xp_ref, pp_ref, ln_ref, wg_ref, wr_ref = refs[:6]
    i = 6
    gla_p, rwkv_p, tail_p = refs[i:i + N_GLA], refs[i + N_GLA:i + N_GLA + N_RWKV], \
        refs[i + N_GLA + N_RWKV:i + N_GLA + N_RWKV + N_TAIL]
    i += N_GLA + N_RWKV + N_TAIL
    out_ref = refs[i]
    o_s, zg_s, zr_s, gxb_ref, gst_ref, rxb_ref, rst_ref = refs[i + 1:i + 8]
    rwkv_s = refs[i + 8:]
    t = pl.program_id(0)
    cur = lax.rem(t, 2)

    @pl.when(t % steps_per_seq == 0)
    def _():
        gxb_ref[0:HALO, :] = jnp.zeros((HALO, gxb_ref.shape[1]), F32)
        gst_ref[...] = jnp.zeros_like(gst_ref)
        rxb_ref[0:HALO, :] = jnp.zeros((HALO, rxb_ref.shape[1]), F32)
        rst_ref[...] = jnp.zeros_like(rst_ref)

    @pl.when(t == 0)
    def _():
        o_s[...] = jnp.zeros_like(o_s)
        for _ in _in_proj_body(xp_ref[0], ln_ref, wg_ref, wr_ref, zg_s.at[0], zr_s.at[0]):
            pass

    res = _interleave({
        "T": _tail_body(xp_ref[0], o_s[...], pp_ref[0], *tail_p),
        "I": _in_proj_body(xn_ref[0], ln_ref, wg_ref, wr_ref, zg_s.at[1 - cur], zr_s.at[1 - cur]),
        "R": _rwkv_body(zr_s.at[cur], *rwkv_p, rxb_ref, rst_ref, *rwkv_s),
        "G": _gla_body(zg_s.at[cur], zr_s.at[cur, :, pl.ds(zr_s.shape[2] - MXU_WIDTH, MXU_WIDTH)], *gla_p,
                       gxb_ref, gst_ref, heads=heads, dk=dk, dv=dv),
    }, STAGE_ORDER)
    out_ref[0] = res["T"]
    o_s[...] = jnp.concatenate([res["G"], res["R"]], axis=1)


def _block(x, p, ln, wg_in, wr_in, gla_params, rwkv_params, tail_params, *, heads, dk, dv):
    bsz, s, d = x.shape
    tl = TILE
    steps_per_seq = s // tl
    nsteps = bsz * steps_per_seq
    kw, vw = heads * dk, heads * dv
    nqkv = 2 * kw + vw
    width = rwkv_params[1].shape[1]
    nchunks = tl // CHUNK
    tri = _chunk_tri(tl)
    hid = jnp.arange(LANES) // RWKV_HEAD
    bdp = (hid[:, None] == hid[None, :]).astype(BF16)
    gla_params = (*gla_params, tri)
    rwkv_params = (*rwkv_params, tri, bdp)
    assert (len(gla_params), len(rwkv_params), len(tail_params)) == (N_GLA, N_RWKV, N_TAIL)

    def nxt(t):
        tn = jnp.minimum(t + 1, nsteps - 1)
        return (tn // steps_per_seq, tn % steps_per_seq, 0)

    def prev(t):
        tp = jnp.maximum(t - 1, 0)
        return (tp // steps_per_seq, tp % steps_per_seq, 0)

    whole = lambda a: pl.BlockSpec(a.shape, lambda t: (0,) * a.ndim, pipeline_mode=pl.Buffered(1))
    consts = (ln, wg_in, wr_in, *gla_params, *rwkv_params, *tail_params)
    kern = functools.partial(_block_kernel, heads=heads, dk=dk, dv=dv, steps_per_seq=steps_per_seq)
    return pl.pallas_call(
        kern,
        grid=(nsteps + 1,),
        in_specs=[pl.BlockSpec((1, tl, d), nxt), pl.BlockSpec((1, tl, d), prev),
                  pl.BlockSpec((1, tl, p.shape[2]), prev)] + [whole(a) for a in consts],
        out_specs=pl.BlockSpec((1, tl, d), prev),
        out_shape=jax.ShapeDtypeStruct((bsz, s, d), F32),
        scratch_shapes=[
            pltpu.VMEM((tl, vw + width), BF16),
            pltpu.VMEM((2, tl, wg_in.shape[1]), F32),
            pltpu.VMEM((2, tl, wr_in.shape[1]), F32),
            pltpu.VMEM((tl + HALO, nqkv), F32),
            pltpu.VMEM((kw // LANES, 2 * dv, LANES), F32),
            pltpu.VMEM((tl + HALO, wr_in.shape[1]), F32),
            pltpu.VMEM((width // LANES, LANES, LANES), F32),
        ] + [pltpu.VMEM((tl, width), BF16)] * 7 + [
            pltpu.VMEM((tl, width), F32),
            pltpu.VMEM((nchunks, HALO, width), F32),
            pltpu.VMEM((tl, width), F32),
        ],
        compiler_params=pltpu.CompilerParams(
            dimension_semantics=("arbitrary",), vmem_limit_bytes=VMEM_LIMIT_BYTES),
        name="hybrid_block",
    )(x, x, p, *consts)


def _pad_cols(w, n):
    return jnp.pad(w, ((0, 0), (0, n - w.shape[1])))


def _place_rows(w, start, total):
    return jnp.pad(w, ((start, total - start - w.shape[0]), (0, 0)))


def _layer(h, p_i, ln_mix, w_in, conv_w, gk_w, gk_b, norm_g, mu, w0, w2, a0, a2, g2, k_k, k_a,
           r_k, gn_w, gn_b, w_out, ln_mlp, w_ff1, w_ff2, ln_ple, w_gate, w_proj, ln_next):
    row = lambda a: a.reshape(1, -1)

    gate_rank, kw = gk_w.shape
    dv = norm_g.shape[0]
    vw = (conv_w.shape[1] - 2 * kw)
    heads = vw // dv
    dk = kw // heads
    gla_main = 2 * kw + 2 * vw
    gla_cols = gla_main + gate_rank
    width = w0.shape[0]
    low_pad = MXU_WIDTH
    rwkv_pad = 3 * width + low_pad
    dr, ar, gr = w2.shape[0], a2.shape[0], g2.shape[0]
    assert gla_main % MXU_WIDTH == 0 and dr + ar + gr + gate_rank <= low_pad

    wg_in = w_in[:, :gla_main].astype(BF16)
    wr_in = _pad_cols(jnp.concatenate([w_in[:, gla_cols:], w_in[:, gla_main:gla_cols]], axis=1),
                      rwkv_pad).astype(BF16)
    gla_params = (conv_w, _place_rows(gk_w, dr + ar + gr, low_pad).astype(BF16), row(gk_b), row(norm_g))
    rwkv_params = (
        _pad_cols(row(mu), rwkv_pad), row(w0), _place_rows(w2, 0, low_pad).astype(BF16), row(a0),
        _place_rows(a2, dr, low_pad).astype(BF16), _place_rows(g2, dr + ar, low_pad).astype(BF16),
        row(k_k), row(k_a),
        row(r_k), row(gn_w), row(gn_b))
    tail_params = (w_out.astype(BF16), row(ln_mlp), w_ff1.astype(BF16), w_ff2.astype(BF16),
                   row(ln_ple), w_gate.astype(BF16), w_proj.astype(BF16), row(ln_next))
    return _block(h, p_i, row(ln_mix), wg_in, wr_in, gla_params, rwkv_params, tail_params,
                  heads=heads, dk=dk, dv=dv)


def kernel(x, p, ln_mix, w_in, gla_conv_w, gla_gk_w, gla_gk_b, gla_norm_g, rwkv_mu, rwkv_w0,
           rwkv_w2, rwkv_a0, rwkv_a2, rwkv_g2, rwkv_k_k, rwkv_k_a, rwkv_r_k, rwkv_gn_w,
           rwkv_gn_b, w_out, ln_mlp, w_ff1, w_ff2, ln_ple, w_ple_gate, w_ple_proj, ln_final):
    depth = w_in.shape[0]
    assert depth == 1, "the fused tail applies the final norm, so exactly one layer is supported"
    assert x.shape[1] % TILE == 0 and TILE % (CHUNK * RWKV_UNROLL) == 0
    i = 0
    return _layer(x, p.reshape(p.shape[1:]), ln_mix[i], w_in[i], gla_conv_w[i], gla_gk_w[i], gla_gk_b[i],
                  gla_norm_g[i], rwkv_mu[i], rwkv_w0[i], rwkv_w2[i], rwkv_a0[i], rwkv_a2[i],
                  rwkv_g2[i], rwkv_k_k[i], rwkv_k_a[i], rwkv_r_k[i].reshape(-1), rwkv_gn_w[i],
                  rwkv_gn_b[i], w_out[i], ln_mlp[i], w_ff1[i], w_ff2[i], ln_ple[i],
                  w_ple_gate[i], w_ple_proj[i], ln_final)
```

```python
import functools

import jax
import jax.numpy as jnp
from jax import lax
from jax.experimental import pallas as pl
from jax.experimental.pallas import tpu as pltpu

F32 = jnp.float32
BF16 = jnp.bfloat16

NORM_EPS = 1e-6
GLA_GATE_NORMALIZER = 16.0
RWKV_GN_EPS = 64e-5
RWKV_HEAD = 64
CHUNK = 64
LANES = 128
HALO = 8
TILE = 256
RWKV_UNROLL = 4
EPILOGUE_CHUNKS = 2
MXU_WIDTH = 256
FF_UP_PIECES = 8
FF_DOWN_PIECES = 4
IN_PIECES = 2
STAGE_ORDER = "TITTIRG" "TTIR" "TTRG" "TTIR" "TIRG" "TRTRTRTR"
VMEM_LIMIT_BYTES = 60 * 1024 * 1024


def _sigmoid(x):
    return 1.0 / (1.0 + jnp.exp(-x))


def _softplus(x):
    return jnp.maximum(x, 0.0) + jnp.log(1.0 + jnp.exp(-jnp.abs(x)))


def _dot(a, b):
    return jnp.dot(a, b, preferred_element_type=F32)


def _dot_nt(a, b):
    return lax.dot_general(a, b, (((1,), (1,)), ((), ())), preferred_element_type=F32)


def _dot_tn(a, b):
    return lax.dot_general(a, b, (((0,), (0,)), ((), ())), preferred_element_type=F32)


def _split2(x):
    hi = x.astype(BF16)
    lo = (x - hi.astype(F32)).astype(BF16)
    return hi, lo


def _dot_exact_lhs(a_bf16, x):
    hi, lo = _split2(x)
    return _dot(a_bf16, hi) + _dot(a_bf16, lo)


def _dot_wide_rhs(a, b_hi, b_lo):
    a = a.astype(BF16)
    return _dot(a, b_hi) + _dot(a, b_lo)


def _rmsnorm(x, g):
    ms = jnp.mean(x * x, axis=-1, keepdims=True)
    return x * lax.rsqrt(ms + NORM_EPS) * g


def _chunk_tri(tl):
    idx = jnp.arange(tl)
    same = idx[:, None] // CHUNK == idx[None, :] // CHUNK
    return ((idx[:, None] >= idx[None, :]) & same).astype(BF16)


def _gla_body(z_ref, low_ref, cw_ref, gkw_ref, gkb_ref, ng_ref, tri_ref, xb_ref, st_ref, *, heads, dk, dv):
    kw = heads * dk
    vw = heads * dv
    nqkv = 2 * kw + vw
    taps = cw_ref.shape[0]
    pairs = kw // LANES
    tl = z_ref.shape[0]
    nchunks = tl // CHUNK
    xb_ref[HALO:HALO + tl, :] = z_ref[:, :nqkv]
    conv = jnp.zeros((tl, nqkv), F32)
    for t in range(taps):
        off = HALO - (taps - 1) + t
        conv = conv + cw_ref[t:t + 1, :] * xb_ref[off:off + tl, :]
    xb_ref[0:HALO, :] = xb_ref[tl:tl + HALO, :]
    qkv = conv * _sigmoid(conv)
    q = qkv[:, :kw] * (dk ** -0.5)
    k = qkv[:, kw:2 * kw]
    v = qkv[:, 2 * kw:nqkv].astype(BF16)
    low = low_ref[...]

    gk_pre = _dot(low.astype(BF16), gkw_ref[...]) + gkb_ref[...]
    yield
    gk = -_softplus(-gk_pre) * (1.0 / GLA_GATE_NORMALIZER)
    b = _dot_exact_lhs(tri_ref[...], gk)
    yield
    last_rows = [b[c * CHUNK + CHUNK - 1:(c + 1) * CHUNK, :] for c in range(nchunks)]
    b_last = jnp.concatenate([jnp.broadcast_to(row, (CHUNK, kw)) for row in last_rows], axis=0)
    b_ref = 0.5 * b_last
    qe = (q * jnp.exp(b - b_ref)).astype(BF16)
    head0 = lax.rem(lax.broadcasted_iota(jnp.int32, (tl, kw), 1), LANES) < dk
    ke = k * jnp.exp(b_ref - b)
    ke0 = jnp.where(head0, ke, 0.0).astype(BF16)
    ke1 = jnp.where(head0, 0.0, ke).astype(BF16)
    qb = (q * jnp.exp(b)).astype(BF16)
    kd = (k * jnp.exp(b_last - b)).astype(BF16)
    e_last = [jnp.exp(row) for row in last_rows]

    lane = lax.broadcasted_iota(jnp.int32, (CHUNK, LANES), 1)
    trow = lax.broadcasted_iota(jnp.int32, (CHUNK, LANES), 0)
    causal = jnp.where(lane < dk, lane, lane - dk) <= trow
    vzero = jnp.zeros((CHUNK, dv), BF16)
    srow = lax.broadcasted_iota(jnp.int32, (2 * dv, LANES), 0)
    scol = lax.broadcasted_iota(jnp.int32, (2 * dv, LANES), 1)
    st_mask = ((srow < dv) == (scol < dk)).astype(F32)

    items = [(c, p) for c in range(nchunks) for p in range(pairs)]
    rs = [slice(c * CHUNK, (c + 1) * CHUNK) for c in range(nchunks)]
    ls = [slice(p * LANES, (p + 1) * LANES) for p in range(pairs)]
    vs = [slice(2 * p * dv, 2 * (p + 1) * dv) for p in range(pairs)]
    s = [_dot_nt(qe[rs[c], ls[p]], jnp.concatenate([ke0[rs[c], ls[p]], ke1[rs[c], ls[p]]], axis=0))
         for c, p in items]
    s = [jnp.where(causal, s_, 0.0).astype(BF16) for s_ in s]
    v_bd = [jnp.concatenate(
        [jnp.concatenate([v[rs[c], 2 * p * dv:(2 * p + 1) * dv], vzero], axis=1),
         jnp.concatenate([vzero, v[rs[c], (2 * p + 1) * dv:(2 * p + 2) * dv]], axis=1)], axis=0)
        for c, p in items]
    o_intra = [_dot(s_, v_) for s_, v_ in zip(s, v_bd)]
    kv = [_dot_tn(v[rs[c], vs[p]], kd[rs[c], ls[p]]) for c, p in items]
    st = [st_ref[p] for p in range(pairs)]
    o_rows = []
    for c in range(nchunks):
        o_c = []
        for p in range(pairs):
            i = c * pairs + p
            o_c.append(o_intra[i] + _dot_nt(qb[rs[c], ls[p]], st[p].astype(BF16)))
            st[p] = st[p] * e_last[c][:, ls[p]] + st_mask * kv[i]
        o_rows.append(jnp.concatenate(o_c, axis=1))
    for p in range(pairs):
        st_ref[p] = st[p]
    o = jnp.concatenate(o_rows, axis=0)
    o_n = []
    for h in range(heads):
        o_h = o[:, h * dv:(h + 1) * dv]
        ms = jnp.mean(o_h * o_h, axis=-1, keepdims=True)
        o_n.append(o_h * lax.rsqrt(ms + NORM_EPS) * ng_ref[...])
    g = z_ref[:, nqkv:nqkv + vw]
    return (jnp.concatenate(o_n, axis=1) * (g * _sigmoid(g))).astype(BF16)


def _headsum(x, bd_pair):
    rows, width = x.shape
    groups = width // LANES
    xb = x.astype(BF16)
    stacked = jnp.concatenate([xb[:, i * LANES:(i + 1) * LANES] for i in range(groups)], axis=0)
    s = _dot(stacked, bd_pair)
    return jnp.concatenate([s[i * rows:(i + 1) * rows] for i in range(groups)], axis=1)


def _rwkv_body(z_ref, mu_ref, w0_ref, w2_ref, a0_ref, a2_ref, g2_ref, kk_ref, ka_ref,
               rk_ref, gnw_ref, gnb_ref, tri_ref, bdp_ref,
               xb_ref, st_ref, at_s, rt_s, bt_s, kt_s, bh_s, kh_s, v_s, rtf_s, gam_s, y_s):
    n = RWKV_HEAD
    width = w0_ref.shape[1]
    pairs = width // LANES
    tl = z_ref.shape[0]
    nchunks = tl // CHUNK
    levels = (CHUNK - 1).bit_length()

    z = z_ref[...]
    xb_ref[HALO:HALO + tl, :] = z
    zp = xb_ref[HALO - 1:HALO - 1 + tl, :]
    xb_ref[0:HALO, :] = xb_ref[tl:tl + HALO, :]
    zs = z + mu_ref[...] * (zp - z)
    r = zs[:, 0:width]
    k = zs[:, width:2 * width]
    v = zs[:, 2 * width:3 * width]
    low = zs[:, 3 * width:]

    w_pre = w0_ref[...] + _dot(jnp.tanh(low).astype(BF16), w2_ref[...])
    logw = -jnp.exp(-_softplus(-w_pre) - 0.5)
    a_sig = _sigmoid(a0_ref[...] + _dot(low.astype(BF16), a2_ref[...]))
    g = _dot(_sigmoid(low).astype(BF16), g2_ref[...])
    yield

    bdp = bdp_ref[...]
    kk = k * kk_ref[...]
    kf = k * (1.0 + (a_sig - 1.0) * ka_ref[...])
    sums = _headsum(jnp.concatenate([kk * kk, r * kf * rk_ref[...]], axis=0), bdp)
    yield
    kk = kk / jnp.maximum(jnp.sqrt(sums[:tl]), 1e-12)
    bv = kk * a_sig
    bonus = sums[tl:]
    yield

    lg = _dot_exact_lhs(tri_ref[...], logw)
    yield
    last_rows = [lg[c * CHUNK + CHUNK - 1:(c + 1) * CHUNK, :] for c in range(nchunks)]
    lg_last = jnp.concatenate([jnp.broadcast_to(row, (CHUNK, width)) for row in last_rows], axis=0)
    for c, row in enumerate(last_rows):
        gam_s[c] = jnp.broadcast_to(jnp.exp(row), (HALO, width))
    e_inv = jnp.exp(-lg)
    e_rem = jnp.exp(lg_last - lg)
    rt = r * jnp.exp(lg)
    at_s[...] = (-kk * jnp.exp(lg - logw)).astype(BF16)
    rt_s[...] = rt.astype(BF16)
    rtf_s[...] = rt
    bt_s[...] = (bv * e_inv).astype(BF16)
    kt_s[...] = (kf * e_inv).astype(BF16)
    bh_s[...] = (bv * e_rem).astype(BF16)
    kh_s[...] = (kf * e_rem).astype(BF16)
    v_s[...] = v.astype(BF16)
    yield

    lane = lax.broadcasted_iota(jnp.int32, (CHUNK, LANES), 1)
    trow = lax.broadcasted_iota(jnp.int32, (CHUNK, LANES), 0)
    tcol = jnp.where(lane < n, lane, lane - n)
    incl, strict = tcol <= trow, tcol < trow
    eye2 = (tcol == trow).astype(F32)
    m0 = (lane < n).astype(BF16)
    m1 = (lane >= n).astype(BF16)
    m0w = jnp.concatenate([m0, m0], axis=1)
    m1w = jnp.concatenate([m1, m1], axis=1)
    prow = lax.broadcasted_iota(jnp.int32, (LANES, LANES), 0)
    pcol = lax.broadcasted_iota(jnp.int32, (LANES, LANES), 1)
    blk = ((prow < n) == (pcol < n)).astype(F32)
    eye = (prow == pcol).astype(F32)
    lsl = [slice(p * LANES, (p + 1) * LANES) for p in range(pairs)]
    outs = []

    def epilogue(r0, r1):
        y = y_s[r0:r1, :]
        moments = _headsum(jnp.concatenate([y, y * y], axis=0), bdp) * (1.0 / n)
        mean = moments[:r1 - r0]
        var = moments[r1 - r0:] - mean * mean
        yn = (y - mean) * lax.rsqrt(var + RWKV_GN_EPS) * gnw_ref[...] + gnb_ref[...]
        return ((yn + bonus[r0:r1] * v[r0:r1]) * g[r0:r1]).astype(BF16)

    def per_head(xb, wide=False):
        return jnp.concatenate([xb * (m0w if wide else m0), xb * (m1w if wide else m1)], axis=0)

    for it in range(nchunks // RWKV_UNROLL):
        items = [(u, p) for u in range(RWKV_UNROLL) for p in range(pairs)]
        rows = [pl.ds((it * RWKV_UNROLL + u) * CHUNK, CHUNK) for u in range(RWKV_UNROLL)]
        at_b = [at_s[rows[u], lsl[p]] for u, p in items]
        v_b = [v_s[rows[u], lsl[p]] for u, p in items]
        lhs = [jnp.concatenate([a_, rt_s[rows[u], lsl[p]]], axis=0)
               for a_, (u, p) in zip(at_b, items)]
        sb = [_dot_nt(l_, per_head(bt_s[rows[u], lsl[p]])) for l_, (u, p) in zip(lhs, items)]
        sk = [_dot_nt(l_, per_head(kt_s[rows[u], lsl[p]])) for l_, (u, p) in zip(lhs, items)]
        yield
        a_ab = [jnp.where(strict, s_[:CHUNK], 0.0).astype(BF16) for s_ in sb]
        a_rb = [jnp.where(incl, s_[CHUNK:], 0.0).astype(BF16) for s_ in sb]
        a_kk = [jnp.concatenate([jnp.where(strict, s_[:CHUNK], 0.0),
                                 jnp.where(incl, s_[CHUNK:], 0.0)], axis=0).astype(BF16) for s_ in sk]
        akv = [_dot(a_, per_head(v_)) for a_, v_ in zip(a_kk, v_b)]
        yield
        pw = a_ab
        inv = [eye2 + p_.astype(F32) for p_ in pw]
        for lvl in range(1, levels):
            pw = [_dot(p_, per_head(p_)).astype(BF16) for p_ in pw]
            inv = [q_ + _dot(p_, per_head(q_.astype(BF16))) for q_, p_ in zip(inv, pw)]
            yield
        x0 = [jnp.concatenate([a_, k_[:CHUNK].astype(BF16)], axis=1) for a_, k_ in zip(at_b, akv)]
        xb = [_dot(q_.astype(BF16), per_head(x_, wide=True)).astype(BF16)
              for q_, x_ in zip(inv, x0)]
        zz = [_dot(a_, per_head(x_, wide=True)) for a_, x_ in zip(a_rb, xb)]
        mn = []
        for i, (u, p) in enumerate(items):
            lb = jnp.concatenate([bh_s[rows[u], lsl[p]], kh_s[rows[u], lsl[p]]], axis=0)
            rb = jnp.concatenate(
                [xb[i], jnp.concatenate([jnp.zeros_like(v_b[i]), v_b[i]], axis=1)], axis=0)
            mn.append(_dot_tn(lb, rb))
        yield
        t_cur = [st_ref[p] for p in range(pairs)]
        for i, (u, p) in enumerate(items):
            ls = lsl[p]
            gam = gam_s[it * RWKV_UNROLL + u][0:1, ls]
            rhat = rtf_s[rows[u], ls] + zz[i][:, :LANES]
            yzero = zz[i][:, LANES:] + akv[i][CHUNK:]
            m_mat = blk * mn[i][:, :LANES] + eye * gam
            n_mat = blk * mn[i][:, LANES:]
            t_hi, t_lo = _split2(t_cur[p])
            y_s[rows[u], ls] = _dot(rhat.astype(BF16), t_hi) + yzero
            t_cur[p] = _dot_wide_rhs(m_mat, t_hi, t_lo) + n_mat
            if p == pairs - 1 and (u + 1) % EPILOGUE_CHUNKS == 0:
                c_end = it * RWKV_UNROLL + u + 1
                outs.append(epilogue((c_end - EPILOGUE_CHUNKS) * CHUNK, c_end * CHUNK))
        for p in range(pairs):
            st_ref[p] = t_cur[p]
        yield

    return jnp.concatenate(outs, axis=0)


def _tail_body(x, o, p, wo_ref, lnm_ref, w1_ref, w2_ref, lnp_ref, wg_ref, wp_ref, lnf_ref):
    h = x + _dot(o, wo_ref[...])
    n1 = _rmsnorm(h, lnm_ref[...]).astype(BF16)
    yield
    d_ff, d = w1_ref.shape[1], w2_ref.shape[1]
    hid = []
    for cs in _lane_blocks(d_ff, FF_UP_PIECES):
        a = jnp.maximum(_dot(n1, w1_ref[:, cs]).astype(BF16), 0.0)
        hid.append(a * a)
        yield
    hid = jnp.concatenate(hid, axis=1)
    down = []
    for cs in _lane_blocks(d, FF_DOWN_PIECES):
        down.append(_dot(hid, w2_ref[:, cs]))
        yield
    h = h + jnp.concatenate(down, axis=1)
    n2 = _rmsnorm(h, lnp_ref[...]).astype(BF16)
    gate = _sigmoid(_dot(n2, wg_ref[...]))
    h = h + gate * _dot(p.astype(BF16), wp_ref[...])
    return _rmsnorm(h, lnf_ref[...])


N_GLA, N_RWKV, N_TAIL = 5, 13, 8


def _lane_blocks(n, k):
    groups = n // MXU_WIDTH
    cuts = [(groups * i // k) * MXU_WIDTH for i in range(k)] + [n]
    return [slice(a, b) for a, b in zip(cuts[:-1], cuts[1:])]


def _in_proj_body(x, ln_ref, wg_ref, wr_ref, zg_ref, zr_ref):
    xn = _rmsnorm(x, ln_ref[...]).astype(BF16)
    for w_ref, z_ref in ((wg_ref, zg_ref), (wr_ref, zr_ref)):
        for cs in _lane_blocks(w_ref.shape[1], IN_PIECES):
            yield
            z_ref[:, cs] = _dot(xn, w_ref[:, cs])


def _interleave(gens, order):
    results = {}
    live = dict(gens)

    def advance(name):
        if name in live:
            try:
                next(live[name])
            except StopIteration as done:
                results[name] = done.value
                del live[name]

    for name in order:
        advance(name)
    while live:
        for name in list(live):
            advance(name)
    return results


def _block_kernel(*refs, heads, dk, dv, steps_per_seq):
    xn_ref, xp_ref, pp_ref, ln_ref, wg_ref, wr_ref = refs[:6]
    i = 6
    gla_p, rwkv_p, tail_p = refs[i:i + N_GLA], refs[i + N_GLA:i + N_GLA + N_RWKV], \
        refs[i + N_GLA + N_RWKV:i + N_GLA + N_RWKV + N_TAIL]
    i += N_GLA + N_RWKV + N_TAIL
    out_ref = refs[i]
    o_s, zg_s, zr_s, gxb_ref, gst_ref, rxb_ref, rst_ref = refs[i + 1:i + 8]
    rwkv_s = refs[i + 8:]
    t = pl.program_id(0)
    cur = lax.rem(t, 2)

    @pl.when(t % steps_per_seq == 0)
    def _():
        gxb_ref[0:HALO, :] = jnp.zeros((HALO, gxb_ref.shape[1]), F32)
        gst_ref[...] = jnp.zeros_like(gst_ref)
        rxb_ref[0:HALO, :] = jnp.zeros((HALO, rxb_ref.shape[1]), F32)
        rst_ref[...] = jnp.zeros_like(rst_ref)

    @pl.when(t == 0)
    def _():
        o_s[...] = jnp.zeros_like(o_s)
        for _ in _in_proj_body(xp_ref[0], ln_ref, wg_ref, wr_ref, zg_s.at[0], zr_s.at[0]):
            pass

    res = _interleave({
        "T": _tail_body(xp_ref[0], o_s[...], pp_ref[0], *tail_p),
        "I": _in_proj_body(xn_ref[0], ln_ref, wg_ref, wr_ref, zg_s.at[1 - cur], zr_s.at[1 - cur]),
        "R": _rwkv_body(zr_s.at[cur], *rwkv_p, rxb_ref, rst_ref, *rwkv_s),
        "G": _gla_body(zg_s.at[cur], zr_s.at[cur, :, pl.ds(zr_s.shape[2] - MXU_WIDTH, MXU_WIDTH)], *gla_p,
                       gxb_ref, gst_ref, heads=heads, dk=dk, dv=dv),
    }, STAGE_ORDER)
    out_ref[0] = res["T"]
    o_s[...] = jnp.concatenate([res["G"], res["R"]], axis=1)


def _block(x, p, ln, wg_in, wr_in, gla_params, rwkv_params, tail_params, *, heads, dk, dv):
    bsz, s, d = x.shape
    tl = TILE
    steps_per_seq = s // tl
    nsteps = bsz * steps_per_seq
    kw, vw = heads * dk, heads * dv
    nqkv = 2 * kw + vw
    width = rwkv_params[1].shape[1]
    nchunks = tl // CHUNK
    tri = _chunk_tri(tl)
    hid = jnp.arange(LANES) // RWKV_HEAD
    bdp = (hid[:, None] == hid[None, :]).astype(BF16)
    gla_params = (*gla_params, tri)
    rwkv_params = (*rwkv_params, tri, bdp)
    assert (len(gla_params), len(rwkv_params), len(tail_params)) == (N_GLA, N_RWKV, N_TAIL)

    def nxt(t):
        tn = jnp.minimum(t + 1, nsteps - 1)
        return (tn // steps_per_seq, tn % steps_per_seq, 0)

    def prev(t):
        tp = jnp.maximum(t - 1, 0)
        return (tp // steps_per_seq, tp % steps_per_seq, 0)

    whole = lambda a: pl.BlockSpec(a.shape, lambda t: (0,) * a.ndim, pipeline_mode=pl.Buffered(1))
    consts = (ln, wg_in, wr_in, *gla_params, *rwkv_params, *tail_params)
    kern = functools.partial(_block_kernel, heads=heads, dk=dk, dv=dv, steps_per_seq=steps_per_seq)
    return pl.pallas_call(
        kern,
        grid=(nsteps + 1,),
        in_specs=[pl.BlockSpec((1, tl, d), nxt), pl.BlockSpec((1, tl, d), prev),
                  pl.BlockSpec((1, tl, p.shape[2]), prev)] + [whole(a) for a in consts],
        out_specs=pl.BlockSpec((1, tl, d), prev),
        out_shape=jax.ShapeDtypeStruct((bsz, s, d), F32),
        scratch_shapes=[
            pltpu.VMEM((tl, vw + width), BF16),
            pltpu.VMEM((2, tl, wg_in.shape[1]), F32),
            pltpu.VMEM((2, tl, wr_in.shape[1]), F32),
            pltpu.VMEM((tl + HALO, nqkv), F32),
            pltpu.VMEM((kw // LANES, 2 * dv, LANES), F32),
            pltpu.VMEM((tl + HALO, wr_in.shape[1]), F32),
            pltpu.VMEM((width // LANES, LANES, LANES), F32),
        ] + [pltpu.VMEM((tl, width), BF16)] * 7 + [
            pltpu.VMEM((tl, width), F32),
            pltpu.VMEM((nchunks, HALO, width), F32),
            pltpu.VMEM((tl, width), F32),
        ],
        compiler_params=pltpu.CompilerParams(
            dimension_semantics=("arbitrary",), vmem_limit_bytes=VMEM_LIMIT_BYTES),
        name="hybrid_block",
    )(x, x, p, *consts)


def _pad_cols(w, n):
    return jnp.pad(w, ((0, 0), (0, n - w.shape[1])))


def _place_rows(w, start, total):
    return jnp.pad(w, ((start, total - start - w.shape[0]), (0, 0)))


def _layer(h, p_i, ln_mix, w_in, conv_w, gk_w, gk_b, norm_g, mu, w0, w2, a0, a2, g2, k_k, k_a,
           r_k, gn_w, gn_b, w_out, ln_mlp, w_ff1, w_ff2, ln_ple, w_gate, w_proj, ln_next):
    row = lambda a: a.reshape(1, -1)

    gate_rank, kw = gk_w.shape
    dv = norm_g.shape[0]
    vw = (conv_w.shape[1] - 2 * kw)
    heads = vw // dv
    dk = kw // heads
    gla_main = 2 * kw + 2 * vw
    gla_cols = gla_main + gate_rank
    width = w0.shape[0]
    low_pad = MXU_WIDTH
    rwkv_pad = 3 * width + low_pad
    dr, ar, gr = w2.shape[0], a2.shape[0], g2.shape[0]
    assert gla_main % MXU_WIDTH == 0 and dr + ar + gr + gate_rank <= low_pad

    wg_in = w_in[:, :gla_main].astype(BF16)
    wr_in = _pad_cols(jnp.concatenate([w_in[:, gla_cols:], w_in[:, gla_main:gla_cols]], axis=1),
                      rwkv_pad).astype(BF16)
    gla_params = (conv_w, _place_rows(gk_w, dr + ar + gr, low_pad).astype(BF16), row(gk_b), row(norm_g))
    rwkv_params = (
        _pad_cols(row(mu), rwkv_pad), row(w0), _place_rows(w2, 0, low_pad).astype(BF16), row(a0),
        _place_rows(a2, dr, low_pad).astype(BF16), _place_rows(g2, dr + ar, low_pad).astype(BF16),
        row(k_k), row(k_a),
        row(r_k), row(gn_w), row(gn_b))
    tail_params = (w_out.astype(BF16), row(ln_mlp), w_ff1.astype(BF16), w_ff2.astype(BF16),
                   row(ln_ple), w_gate.astype(BF16), w_proj.astype(BF16), row(ln_next))
    return _block(h, p_i, row(ln_mix), wg_in, wr_in, gla_params, rwkv_params, tail_params,
                  heads=heads, dk=dk, dv=dv)


def kernel(x, p, ln_mix, w_in, gla_conv_w, gla_gk_w, gla_gk_b, gla_norm_g, rwkv_mu, rwkv_w0,
           rwkv_w2, rwkv_a0, rwkv_a2, rwkv_g2, rwkv_k_k, rwkv_k_a, rwkv_r_k, rwkv_gn_w,
           rwkv_gn_b, w_out, ln_mlp, w_ff1, w_ff2, ln_ple, w_ple_gate, w_ple_proj, ln_final):
    depth = w_in.shape[0]
    assert depth == 1, "the fused tail applies the final norm, so exactly one layer is supported"
    assert x.shape[1] % TILE == 0 and TILE % (CHUNK * RWKV_UNROLL) == 0
    i = 0
    return _layer(x, p.reshape(p.shape[1:]), ln_mix[i], w_in[i], gla_conv_w[i], gla_gk_w[i], gla_gk_b[i],
                  gla_norm_g[i], rwkv_mu[i], rwkv_w0[i], rwkv_w2[i], rwkv_a0[i], rwkv_a2[i],
                  rwkv_g2[i], rwkv_k_k[i], rwkv_k_a[i], rwkv_r_k[i].reshape(-1), rwkv_gn_w[i],
                  rwkv_gn_b[i], w_out[i], ln_mlp[i], w_ff1[i], w_ff2[i], ln_ple[i],
                  w_ple_gate[i], w_ple_proj[i], ln_final)
```

```python
import functools

import jax
import jax.numpy as jnp
from jax import lax
from jax.experimental import pallas as pl
from jax.experimental.pallas import tpu as pltpu

F32 = jnp.float32
BF16 = jnp.bfloat16

NORM_EPS = 1e-6
GLA_GATE_NORMALIZER = 16.0
RWKV_GN_EPS = 64e-5
RWKV_HEAD = 64
CHUNK = 64
LANES = 128
HALO = 8
TILE = 256
RWKV_UNROLL = 4
EPILOGUE_CHUNKS = 1
MXU_WIDTH = 256
FF_UP_PIECES = 8
FF_DOWN_PIECES = 4
IN_PIECES = 2
STAGE_ORDER = "TITTIRG" "TTIR" "TTRG" "TTIR" "TIRG" "TRTRTRTR"
VMEM_LIMIT_BYTES = 60 * 1024 * 1024


def _sigmoid(x):
    return 1.0 / (1.0 + jnp.exp(-x))


def _softplus(x):
    return jnp.maximum(x, 0.0) + jnp.log(1.0 + jnp.exp(-jnp.abs(x)))


def _dot(a, b):
    return jnp.dot(a, b, preferred_element_type=F32)


def _dot_nt(a, b):
    return lax.dot_general(a, b, (((1,), (1,)), ((), ())), preferred_element_type=F32)


def _dot_tn(a, b):
    return lax.dot_general(a, b, (((0,), (0,)), ((), ())), preferred_element_type=F32)


def _split2(x):
    hi = x.astype(BF16)
    lo = (x - hi.astype(F32)).astype(BF16)
    return hi, lo


def _dot_exact_lhs(a_bf16, x):
    hi, lo = _split2(x)
    return _dot(a_bf16, hi) + _dot(a_bf16, lo)


def _dot_wide_rhs(a, b_hi, b_lo):
    a = a.astype(BF16)
    return _dot(a, b_hi) + _dot(a, b_lo)


def _rmsnorm(x, g):
    ms = jnp.mean(x * x, axis=-1, keepdims=True)
    return x * lax.rsqrt(ms + NORM_EPS) * g


def _chunk_tri(tl):
    idx = jnp.arange(tl)
    same = idx[:, None] // CHUNK == idx[None, :] // CHUNK
    return ((idx[:, None] >= idx[None, :]) & same).astype(BF16)


def _gla_body(z_ref, low_ref, cw_ref, gkw_ref, gkb_ref, ng_ref, tri_ref, xb_ref, st_ref, *, heads, dk, dv):
    kw = heads * dk
    vw = heads * dv
    nqkv = 2 * kw + vw
    taps = cw_ref.shape[0]
    pairs = kw // LANES
    tl = z_ref.shape[0]
    nchunks = tl // CHUNK
    xb_ref[HALO:HALO + tl, :] = z_ref[:, :nqkv]
    conv = jnp.zeros((tl, nqkv), F32)
    for t in range(taps):
        off = HALO - (taps - 1) + t
        conv = conv + cw_ref[t:t + 1, :] * xb_ref[off:off + tl, :]
    xb_ref[0:HALO, :] = xb_ref[tl:tl + HALO, :]
    qkv = conv * _sigmoid(conv)
    q = qkv[:, :kw] * (dk ** -0.5)
    k = qkv[:, kw:2 * kw]
    v = qkv[:, 2 * kw:nqkv].astype(BF16)
    low = low_ref[...]

    gk_pre = _dot(low.astype(BF16), gkw_ref[...]) + gkb_ref[...]
    yield
    gk = -_softplus(-gk_pre) * (1.0 / GLA_GATE_NORMALIZER)
    b = _dot_exact_lhs(tri_ref[...], gk)
    yield
    last_rows = [b[c * CHUNK + CHUNK - 1:(c + 1) * CHUNK, :] for c in range(nchunks)]
    b_last = jnp.concatenate([jnp.broadcast_to(row, (CHUNK, kw)) for row in last_rows], axis=0)
    b_ref = 0.5 * b_last
    qe = (q * jnp.exp(b - b_ref)).astype(BF16)
    head0 = lax.rem(lax.broadcasted_iota(jnp.int32, (tl, kw), 1), LANES) < dk
    ke = k * jnp.exp(b_ref - b)
    ke0 = jnp.where(head0, ke, 0.0).astype(BF16)
    ke1 = jnp.where(head0, 0.0, ke).astype(BF16)
    qb = (q * jnp.exp(b)).astype(BF16)
    kd = (k * jnp.exp(b_last - b)).astype(BF16)
    e_last = [jnp.exp(row) for row in last_rows]

    lane = lax.broadcasted_iota(jnp.int32, (CHUNK, LANES), 1)
    trow = lax.broadcasted_iota(jnp.int32, (CHUNK, LANES), 0)
    causal = jnp.where(lane < dk, lane, lane - dk) <= trow
    vzero = jnp.zeros((CHUNK, dv), BF16)
    srow = lax.broadcasted_iota(jnp.int32, (2 * dv, LANES), 0)
    scol = lax.broadcasted_iota(jnp.int32, (2 * dv, LANES), 1)
    st_mask = ((srow < dv) == (scol < dk)).astype(F32)

    items = [(c, p) for c in range(nchunks) for p in range(pairs)]
    rs = [slice(c * CHUNK, (c + 1) * CHUNK) for c in range(nchunks)]
    ls = [slice(p * LANES, (p + 1) * LANES) for p in range(pairs)]
    vs = [slice(2 * p * dv, 2 * (p + 1) * dv) for p in range(pairs)]
    s = [_dot_nt(qe[rs[c], ls[p]], jnp.concatenate([ke0[rs[c], ls[p]], ke1[rs[c], ls[p]]], axis=0))
         for c, p in items]
    s = [jnp.where(causal, s_, 0.0).astype(BF16) for s_ in s]
    v_bd = [jnp.concatenate(
        [jnp.concatenate([v[rs[c], 2 * p * dv:(2 * p + 1) * dv], vzero], axis=1),
         jnp.concatenate([vzero, v[rs[c], (2 * p + 1) * dv:(2 * p + 2) * dv]], axis=1)], axis=0)
        for c, p in items]
    o_intra = [_dot(s_, v_) for s_, v_ in zip(s, v_bd)]
    kv = [_dot_tn(v[rs[c], vs[p]], kd[rs[c], ls[p]]) for c, p in items]
    st = [st_ref[p] for p in range(pairs)]
    o_rows = []
    for c in range(nchunks):
        o_c = []
        for p in range(pairs):
            i = c * pairs + p
            o_c.append(o_intra[i] + _dot_nt(qb[rs[c], ls[p]], st[p].astype(BF16)))
            st[p] = st[p] * e_last[c][:, ls[p]] + st_mask * kv[i]
        o_rows.append(jnp.concatenate(o_c, axis=1))
    for p in range(pairs):
        st_ref[p] = st[p]
    o = jnp.concatenate(o_rows, axis=0)
    o_n = []
    for h in range(heads):
        o_h = o[:, h * dv:(h + 1) * dv]
        ms = jnp.mean(o_h * o_h, axis=-1, keepdims=True)
        o_n.append(o_h * lax.rsqrt(ms + NORM_EPS) * ng_ref[...])
    g = z_ref[:, nqkv:nqkv + vw]
    return (jnp.concatenate(o_n, axis=1) * (g * _sigmoid(g))).astype(BF16)


def _headsum(x, bd_pair):
    rows, width = x.shape
    groups = width // LANES
    xb = x.astype(BF16)
    stacked = jnp.concatenate([xb[:, i * LANES:(i + 1) * LANES] for i in range(groups)], axis=0)
    s = _dot(stacked, bd_pair)
    return jnp.concatenate([s[i * rows:(i + 1) * rows] for i in range(groups)], axis=1)


def _rwkv_body(z_ref, mu_ref, w0_ref, w2_ref, a0_ref, a2_ref, g2_ref, kk_ref, ka_ref,
               rk_ref, gnw_ref, gnb_ref, tri_ref, bdp_ref,
               xb_ref, st_ref, at_s, rt_s, bt_s, kt_s, bh_s, kh_s, v_s, rtf_s, gam_s, y_s):
    n = RWKV_HEAD
    width = w0_ref.shape[1]
    pairs = width // LANES
    tl = z_ref.shape[0]
    nchunks = tl // CHUNK
    levels = (CHUNK - 1).bit_length()

    z = z_ref[...]
    xb_ref[HALO:HALO + tl, :] = z
    zp = xb_ref[HALO - 1:HALO - 1 + tl, :]
    xb_ref[0:HALO, :] = xb_ref[tl:tl + HALO, :]
    zs = z + mu_ref[...] * (zp - z)
    r = zs[:, 0:width]
    k = zs[:, width:2 * width]
    v = zs[:, 2 * width:3 * width]
    low = zs[:, 3 * width:]

    w_pre = w0_ref[...] + _dot(jnp.tanh(low).astype(BF16), w2_ref[...])
    logw = -jnp.exp(-_softplus(-w_pre) - 0.5)
    a_sig = _sigmoid(a0_ref[...] + _dot(low.astype(BF16), a2_ref[...]))
    g = _dot(_sigmoid(low).astype(BF16), g2_ref[...])
    yield

    bdp = bdp_ref[...]
    kk = k * kk_ref[...]
    kf = k * (1.0 + (a_sig - 1.0) * ka_ref[...])
    sums = _headsum(jnp.concatenate([kk * kk, r * kf * rk_ref[...]], axis=0), bdp)
    yield
    kk = kk / jnp.maximum(jnp.sqrt(sums[:tl]), 1e-12)
    bv = kk * a_sig
    bonus = sums[tl:]
    yield

    lg = _dot_exact_lhs(tri_ref[...], logw)
    yield
    last_rows = [lg[c * CHUNK + CHUNK - 1:(c + 1) * CHUNK, :] for c in range(nchunks)]
    lg_last = jnp.concatenate([jnp.broadcast_to(row, (CHUNK, width)) for row in last_rows], axis=0)
    for c, row in enumerate(last_rows):
        gam_s[c] = jnp.broadcast_to(jnp.exp(row), (HALO, width))
    e_inv = jnp.exp(-lg)
    e_rem = jnp.exp(lg_last - lg)
    rt = r * jnp.exp(lg)
    at_s[...] = (-kk * jnp.exp(lg - logw)).astype(BF16)
    rt_s[...] = rt.astype(BF16)
    rtf_s[...] = rt
    bt_s[...] = (bv * e_inv).astype(BF16)
    kt_s[...] = (kf * e_inv).astype(BF16)
    bh_s[...] = (bv * e_rem).astype(BF16)
    kh_s[...] = (kf * e_rem).astype(BF16)
    v_s[...] = v.astype(BF16)
    yield

    lane = lax.broadcasted_iota(jnp.int32, (CHUNK, LANES), 1)
    trow = lax.broadcasted_iota(jnp.int32, (CHUNK, LANES), 0)
    tcol = jnp.where(lane < n, lane, lane - n)
    incl, strict = tcol <= trow, tcol < trow
    eye2 = (tcol == trow).astype(F32)
    m0 = (lane < n).astype(BF16)
    m1 = (lane >= n).astype(BF16)
    m0w = jnp.concatenate([m0, m0], axis=1)
    m1w = jnp.concatenate([m1, m1], axis=1)
    prow = lax.broadcasted_iota(jnp.int32, (LANES, LANES), 0)
    pcol = lax.broadcasted_iota(jnp.int32, (LANES, LANES), 1)
    blk = ((prow < n) == (pcol < n)).astype(F32)
    eye = (prow == pcol).astype(F32)
    lsl = [slice(p * LANES, (p + 1) * LANES) for p in range(pairs)]
    outs = []

    def epilogue(r0, r1):
        y = y_s[r0:r1, :]
        moments = _headsum(jnp.concatenate([y, y * y], axis=0), bdp) * (1.0 / n)
        mean = moments[:r1 - r0]
        var = moments[r1 - r0:] - mean * mean
        yn = (y - mean) * lax.rsqrt(var + RWKV_GN_EPS) * gnw_ref[...] + gnb_ref[...]
        return ((yn + bonus[r0:r1] * v[r0:r1]) * g[r0:r1]).astype(BF16)

    def per_head(xb, wide=False):
        return jnp.concatenate([xb * (m0w if wide else m0), xb * (m1w if wide else m1)], axis=0)

    for it in range(nchunks // RWKV_UNROLL):
        items = [(u, p) for u in range(RWKV_UNROLL) for p in range(pairs)]
        rows = [pl.ds((it * RWKV_UNROLL + u) * CHUNK, CHUNK) for u in range(RWKV_UNROLL)]
        at_b = [at_s[rows[u], lsl[p]] for u, p in items]
        v_b = [v_s[rows[u], lsl[p]] for u, p in items]
        lhs = [jnp.concatenate([a_, rt_s[rows[u], lsl[p]]], axis=0)
               for a_, (u, p) in zip(at_b, items)]
        sb = [_dot_nt(l_, per_head(bt_s[rows[u], lsl[p]])) for l_, (u, p) in zip(lhs, items)]
        sk = [_dot_nt(l_, per_head(kt_s[rows[u], lsl[p]])) for l_, (u, p) in zip(lhs, items)]
        yield
        a_ab = [jnp.where(strict, s_[:CHUNK], 0.0).astype(BF16) for s_ in sb]
        a_rb = [jnp.where(incl, s_[CHUNK:], 0.0).astype(BF16) for s_ in sb]
        a_kk = [jnp.concatenate([jnp.where(strict, s_[:CHUNK], 0.0),
                                 jnp.where(incl, s_[CHUNK:], 0.0)], axis=0).astype(BF16) for s_ in sk]
        akv = [_dot(a_, per_head(v_)) for a_, v_ in zip(a_kk, v_b)]
        yield
        pw = a_ab
        inv = [eye2 + p_.astype(F32) for p_ in pw]
        for lvl in range(1, levels):
            pw = [_dot(p_, per_head(p_)).astype(BF16) for p_ in pw]
            inv = [q_ + _dot(p_, per_head(q_.astype(BF16))) for q_, p_ in zip(inv, pw)]
            yield
        x0 = [jnp.concatenate([a_, k_[:CHUNK].astype(BF16)], axis=1) for a_, k_ in zip(at_b, akv)]
        xb = [_dot(q_.astype(BF16), per_head(x_, wide=True)).astype(BF16)
              for q_, x_ in zip(inv, x0)]
        zz = [_dot(a_, per_head(x_, wide=True)) for a_, x_ in zip(a_rb, xb)]
        mn = []
        for i, (u, p) in enumerate(items):
            lb = jnp.concatenate([bh_s[rows[u], lsl[p]], kh_s[rows[u], lsl[p]]], axis=0)
            rb = jnp.concatenate(
                [xb[i], jnp.concatenate([jnp.zeros_like(v_b[i]), v_b[i]], axis=1)], axis=0)
            mn.append(_dot_tn(lb, rb))
        yield
        t_cur = [st_ref[p] for p in range(pairs)]
        for i, (u, p) in enumerate(items):
            ls = lsl[p]
            gam = gam_s[it * RWKV_UNROLL + u][0:1, ls]
            rhat = rtf_s[rows[u], ls] + zz[i][:, :LANES]
            yzero = zz[i][:, LANES:] + akv[i][CHUNK:]
            m_mat = blk * mn[i][:, :LANES] + eye * gam
            n_mat = blk * mn[i][:, LANES:]
            t_hi, t_lo = _split2(t_cur[p])
            y_s[rows[u], ls] = _dot(rhat.astype(BF16), t_hi) + yzero
            t_cur[p] = _dot_wide_rhs(m_mat, t_hi, t_lo) + n_mat
            if p == pairs - 1 and (u + 1) % EPILOGUE_CHUNKS == 0:
                c_end = it * RWKV_UNROLL + u + 1
                outs.append(epilogue((c_end - EPILOGUE_CHUNKS) * CHUNK, c_end * CHUNK))
        for p in range(pairs):
            st_ref[p] = t_cur[p]
        yield

    return jnp.concatenate(outs, axis=0)


def _tail_body(x, o, p, wo_ref, lnm_ref, w1_ref, w2_ref, lnp_ref, wg_ref, wp_ref, lnf_ref):
    h = x + _dot(o, wo_ref[...])
    n1 = _rmsnorm(h, lnm_ref[...]).astype(BF16)
    yield
    d_ff, d = w1_ref.shape[1], w2_ref.shape[1]
    hid = []
    for cs in _lane_blocks(d_ff, FF_UP_PIECES):
        a = jnp.maximum(_dot(n1, w1_ref[:, cs]).astype(BF16), 0.0)
        hid.append(a * a)
        yield
    hid = jnp.concatenate(hid, axis=1)
    down = []
    for cs in _lane_blocks(d, FF_DOWN_PIECES):
        down.append(_dot(hid, w2_ref[:, cs]))
        yield
    h = h + jnp.concatenate(down, axis=1)
    n2 = _rmsnorm(h, lnp_ref[...]).astype(BF16)
    gate = _sigmoid(_dot(n2, wg_ref[...]))
    h = h + gate * _dot(p.astype(BF16), wp_ref[...])
    return _rmsnorm(h, lnf_ref[...])


N_GLA, N_RWKV, N_TAIL = 5, 13, 8


def _lane_blocks(n, k):
    groups = n // MXU_WIDTH
    cuts = [(groups * i // k) * MXU_WIDTH for i in range(k)] + [n]
    return [slice(a, b) for a, b in zip(cuts[:-1], cuts[1:])]


def _in_proj_body(x, ln_ref, wg_ref, wr_ref, zg_ref, zr_ref):
    xn = _rmsnorm(x, ln_ref[...]).astype(BF16)
    for w_ref, z_ref in ((wg_ref, zg_ref), (wr_ref, zr_ref)):
        for cs in _lane_blocks(w_ref.shape[1], IN_PIECES):
            yield
            z_ref[:, cs] = _dot(xn, w_ref[:, cs])


def _interleave(gens, order):
    results = {}
    live = dict(gens)

    def advance(name):
        if name in live:
            try:
                next(live[name])
            except StopIteration as done:
                results[name] = done.value
                del live[name]

    for name in order:
        advance(name)
    while live:
        for name in list(live):
            advance(name)
    return results


def _block_kernel(*refs, heads, dk, dv, steps_per_seq):
    xn_ref, xp_ref, pp_ref, ln_ref, wg_ref, wr_ref = refs[:6]
    i = 6
    gla_p, rwkv_p, tail_p = refs[i:i + N_GLA], refs[i + N_GLA:i + N_GLA + N_RWKV], \
        refs[i + N_GLA + N_RWKV:i + N_GLA + N_RWKV + N_TAIL]
    i += N_GLA + N_RWKV + N_TAIL
    out_ref = refs[i]
    o_s, zg_s, zr_s, gxb_ref, gst_ref, rxb_ref, rst_ref = refs[i + 1:i + 8]
    rwkv_s = refs[i + 8:]
    t = pl.program_id(0)
    cur = lax.rem(t, 2)

    @pl.when(t % steps_per_seq == 0)
    def _():
        gxb_ref[0:HALO, :] = jnp.zeros((HALO, gxb_ref.shape[1]), F32)
        gst_ref[...] = jnp.zeros_like(gst_ref)
        rxb_ref[0:HALO, :] = jnp.zeros((HALO, rxb_ref.shape[1]), F32)
        rst_ref[...] = jnp.zeros_like(rst_ref)

    @pl.when(t == 0)
    def _():
        o_s[...] = jnp.zeros_like(o_s)
        for _ in _in_proj_body(xp_ref[0], ln_ref, wg_ref, wr_ref, zg_s.at[0], zr_s.at[0]):
            pass

    res = _interleave({
        "T": _tail_body(xp_ref[0], o_s[...], pp_ref[0], *tail_p),
        "I": _in_proj_body(xn_ref[0], ln_ref, wg_ref, wr_ref, zg_s.at[1 - cur], zr_s.at[1 - cur]),
        "R": _rwkv_body(zr_s.at[cur], *rwkv_p, rxb_ref, rst_ref, *rwkv_s),
        "G": _gla_body(zg_s.at[cur], zr_s.at[cur, :, pl.ds(zr_s.shape[2] - MXU_WIDTH, MXU_WIDTH)], *gla_p,
                       gxb_ref, gst_ref, heads=heads, dk=dk, dv=dv),
    }, STAGE_ORDER)
    out_ref[0] = res["T"]
    o_s[...] = jnp.concatenate([res["G"], res["R"]], axis=1)


def _block(x, p, ln, wg_in, wr_in, gla_params, rwkv_params, tail_params, *, heads, dk, dv):
    bsz, s, d = x.shape
    tl = TILE
    steps_per_seq = s // tl
    nsteps = bsz * steps_per_seq
    kw, vw = heads * dk, heads * dv
    nqkv = 2 * kw + vw
    width = rwkv_params[1].shape[1]
    nchunks = tl // CHUNK
    tri = _chunk_tri(tl)
    hid = jnp.arange(LANES) // RWKV_HEAD
    bdp = (hid[:, None] == hid[None, :]).astype(BF16)
    gla_params = (*gla_params, tri)
    rwkv_params = (*rwkv_params, tri, bdp)
    assert (len(gla_params), len(rwkv_params), len(tail_params)) == (N_GLA, N_RWKV, N_TAIL)

    def nxt(t):
        tn = jnp.minimum(t + 1, nsteps - 1)
        return (tn // steps_per_seq, tn % steps_per_seq, 0)

    def prev(t):
        tp = jnp.maximum(t - 1, 0)
        return (tp // steps_per_seq, tp % steps_per_seq, 0)

    whole = lambda a: pl.BlockSpec(a.shape, lambda t: (0,) * a.ndim, pipeline_mode=pl.Buffered(1))
    consts = (ln, wg_in, wr_in, *gla_params, *rwkv_params, *tail_params)
    kern = functools.partial(_block_kernel, heads=heads, dk=dk, dv=dv, steps_per_seq=steps_per_seq)
    return pl.pallas_call(
        kern,
        grid=(nsteps + 1,),
        in_specs=[pl.BlockSpec((1, tl, d), nxt), pl.BlockSpec((1, tl, d), prev),
                  pl.BlockSpec((1, tl, p.shape[2]), prev)] + [whole(a) for a in consts],
        out_specs=pl.BlockSpec((1, tl, d), prev),
        out_shape=jax.ShapeDtypeStruct((bsz, s, d), F32),
        scratch_shapes=[
            pltpu.VMEM((tl, vw + width), BF16),
            pltpu.VMEM((2, tl, wg_in.shape[1]), F32),
            pltpu.VMEM((2, tl, wr_in.shape[1]), F32),
            pltpu.VMEM((tl + HALO, nqkv), F32),
            pltpu.VMEM((kw // LANES, 2 * dv, LANES), F32),
            pltpu.VMEM((tl + HALO, wr_in.shape[1]), F32),
            pltpu.VMEM((width // LANES, LANES, LANES), F32),
        ] + [pltpu.VMEM((tl, width), BF16)] * 7 + [
            pltpu.VMEM((tl, width), F32),
            pltpu.VMEM((nchunks, HALO, width), F32),
            pltpu.VMEM((tl, width), F32),
        ],
        compiler_params=pltpu.CompilerParams(
            dimension_semantics=("arbitrary",), vmem_limit_bytes=VMEM_LIMIT_BYTES),
        name="hybrid_block",
    )(x, x, p, *consts)


def _pad_cols(w, n):
    return jnp.pad(w, ((0, 0), (0, n - w.shape[1])))


def _place_rows(w, start, total):
    return jnp.pad(w, ((start, total - start - w.shape[0]), (0, 0)))


def _layer(h, p_i, ln_mix, w_in, conv_w, gk_w, gk_b, norm_g, mu, w0, w2, a0, a2, g2, k_k, k_a,
           r_k, gn_w, gn_b, w_out, ln_mlp, w_ff1, w_ff2, ln_ple, w_gate, w_proj, ln_next):
    row = lambda a: a.reshape(1, -1)

    gate_rank, kw = gk_w.shape
    dv = norm_g.shape[0]
    vw = (conv_w.shape[1] - 2 * kw)
    heads = vw // dv
    dk = kw // heads
    gla_main = 2 * kw + 2 * vw
    gla_cols = gla_main + gate_rank
    width = w0.shape[0]
    low_pad = MXU_WIDTH
    rwkv_pad = 3 * width + low_pad
    dr, ar, gr = w2.shape[0], a2.shape[0], g2.shape[0]
    assert gla_main % MXU_WIDTH == 0 and dr + ar + gr + gate_rank <= low_pad

    wg_in = w_in[:, :gla_main].astype(BF16)
    wr_in = _pad_cols(jnp.concatenate([w_in[:, gla_cols:], w_in[:, gla_main:gla_cols]], axis=1),
                      rwkv_pad).astype(BF16)
    gla_params = (conv_w, _place_rows(gk_w, dr + ar + gr, low_pad).astype(BF16), row(gk_b), row(norm_g))
    rwkv_params = (
        _pad_cols(row(mu), rwkv_pad), row(w0), _place_rows(w2, 0, low_pad).astype(BF16), row(a0),
        _place_rows(a2, dr, low_pad).astype(BF16), _place_rows(g2, dr + ar, low_pad).astype(BF16),
        row(k_k), row(k_a),
        row(r_k), row(gn_w), row(gn_b))
    tail_params = (w_out.astype(BF16), row(ln_mlp), w_ff1.astype(BF16), w_ff2.astype(BF16),
                   row(ln_ple), w_gate.astype(BF16), w_proj.astype(BF16), row(ln_next))
    return _block(h, p_i, row(ln_mix), wg_in, wr_in, gla_params, rwkv_params, tail_params,
                  heads=heads, dk=dk, dv=dv)


def kernel(x, p, ln_mix, w_in, gla_conv_w, gla_gk_w, gla_gk_b, gla_norm_g, rwkv_mu, rwkv_w0,
           rwkv_w2, rwkv_a0, rwkv_a2, rwkv_g2, rwkv_k_k, rwkv_k_a, rwkv_r_k, rwkv_gn_w,
           rwkv_gn_b, w_out, ln_mlp, w_ff1, w_ff2, ln_ple, w_ple_gate, w_ple_proj, ln_final):
    depth = w_in.shape[0]
    assert depth == 1, "the fused tail applies the final norm, so exactly one layer is supported"
    assert x.shape[1] % TILE == 0 and TILE % (CHUNK * RWKV_UNROLL) == 0
    i = 0
    return _layer(x, p.reshape(p.shape[1:]), ln_mix[i], w_in[i], gla_conv_w[i], gla_gk_w[i], gla_gk_b[i],
                  gla_norm_g[i], rwkv_mu[i], rwkv_w0[i], rwkv_w2[i], rwkv_a0[i], rwkv_a2[i],
                  rwkv_g2[i], rwkv_k_k[i], rwkv_k_a[i], rwkv_r_k[i].reshape(-1), rwkv_gn_w[i],
                  rwkv_gn_b[i], w_out[i], ln_mlp[i], w_ff1[i], w_ff2[i], ln_ple[i],
                  w_ple_gate[i], w_ple_proj[i], ln_final)
```
